```python
import math
import jax
import jax.numpy as jnp
from jax import lax
import numpy as np

D_MODEL = 2048
BATCH = 16
SEQ = 2048
DEPTH = 1
DEC_BATCH = 32
DEC_SEQ = 4
PAST_LEN = 16384
PAGE_SIZE = 128

SSM_EXPAND = 2
D_INNER = SSM_EXPAND * D_MODEL
SSM_HEAD_DIM = 64
SSM_HEADS = D_INNER // SSM_HEAD_DIM
SSM_GROUPS = 8
SSM_STATE = 128
CONV_W = 4
CONV_DIM = D_INNER + 2 * SSM_GROUPS * SSM_STATE
SSM_CHUNK = 128
RMS_EPS = 1e-5
DIL_CONFIGS = ((128, 1), (512, 4), (2048, 16))
N_DIL_GROUPS = 3
HEADS_PER_GROUP = 4
ATT_HEADS = N_DIL_GROUPS * HEADS_PER_GROUP
ATT_HEAD_DIM = 128
ATT_WIDTH = ATT_HEADS * ATT_HEAD_DIM
ATT_OUT_WIDTH = HEADS_PER_GROUP * ATT_HEAD_DIM
ATT_BLOCK = 128
ROT_DIM = ATT_HEAD_DIM // 4
ROPE_THETA = 500000.0
N_MEM = 256
MEM_HEADS = 4
MEM_HEAD_DIM = D_MODEL // MEM_HEADS
N_EXPERTS = 32
TOP_K = 4
D_EXPERT = D_MODEL
SWIGLU_LIMIT = 7.0
SWIGLU_ALPHA = 1.702
MOE_BLOCK = 128
DN_ALPHA = (2.0 * DEPTH) ** 0.25
DN_BETA = (8.0 * DEPTH) ** -0.25
LN_EPS = 1e-5
IN_WIDTHS = (D_INNER, CONV_DIM, SSM_HEADS, ATT_WIDTH, ATT_WIDTH, ATT_WIDTH, D_MODEL, D_MODEL)
IN_WIDTH = D_INNER + CONV_DIM + SSM_HEADS + 3 * ATT_WIDTH + 2 * D_MODEL

kernel_name = 'hybrid_ssd_dilated_attn_moe_decode_step'


def layer_norm(x, g, b):
    xf = x.astype(jnp.float32)
    mu = jnp.mean(xf, axis=-1, keepdims=True)
    var = jnp.mean(jnp.square(xf - mu), axis=-1, keepdims=True)
    return ((xf - mu) * lax.rsqrt(var + LN_EPS) * g + b).astype(x.dtype)


def split_cols(t, widths):
    parts, start = [], 0
    for w in widths:
        parts.append(t[..., start:start + w])
        start += w
    return parts


def rotary(t, pos):
    half = ROT_DIM // 2
    inv_freq = jnp.exp(-math.log(ROPE_THETA) * jnp.arange(half, dtype=jnp.float32) * (2.0 / ROT_DIM))
    ang = pos.astype(jnp.float32)[:, None] * inv_freq[None, :]
    cos = jnp.cos(ang)[None, :, None, :]
    sin = jnp.sin(ang)[None, :, None, :]
    t1 = t[..., :half].astype(jnp.float32)
    t2 = t[..., half:ROT_DIM].astype(jnp.float32)
    rot = jnp.concatenate([t1 * cos - t2 * sin, t2 * cos + t1 * sin], axis=-1).astype(t.dtype)
    return jnp.concatenate([rot, t[..., ROT_DIM:]], axis=-1)


def causal_dwconv_silu(xp, w, b):
    L = xp.shape[1] - (CONV_W - 1)
    y = b + sum(xp[:, k:k + L] * w[k] for k in range(CONV_W))
    return jax.nn.silu(y)


def ssd_scan(x, dt, A, B, C, h0):
    b, L, H, P = x.shape
    G, N = B.shape[2], B.shape[3]
    R = H // G
    Lc = math.gcd(L, SSM_CHUNK)
    nc = L // Lc
    f32 = jnp.float32
    a = dt * A
    xdt = x.astype(f32) * dt[..., None]

    def chunks(t):
        return jnp.moveaxis(t.reshape(b, nc, Lc, *t.shape[2:]), 1, 0)

    tri = jnp.tril(jnp.ones((Lc, Lc), dtype=bool))

    def step(h, inp):
        xc, ac, bc, cc = inp
        acum = jnp.cumsum(ac, axis=1)
        seg = acum[:, :, None, :] - acum[:, None, :, :]
        lmat = jnp.exp(jnp.where(tri[None, :, :, None], seg, -jnp.inf)).reshape(b, Lc, Lc, G, R)
        cb = jnp.einsum('btgn,bsgn->btsg', cc, bc)
        xg = xc.reshape(b, Lc, G, R, P)
        y_diag = jnp.einsum('btsg,btsgr,bsgrp->btgrp', cb, lmat, xg)
        hg = h.reshape(b, G, R, P, N)
        y_off = jnp.einsum('btgn,bgrpn->btgrp', cc, hg) * jnp.exp(acum).reshape(b, Lc, G, R)[..., None]
        decay_to_end = jnp.exp(acum[:, -1:, :] - acum).reshape(b, Lc, G, R)
        h_new = (hg * jnp.exp(acum[:, -1]).reshape(b, G, R)[..., None, None]
                 + jnp.einsum('bsgn,bsgr,bsgrp->bgrpn', bc, decay_to_end, xg))
        return h_new.reshape(b, H, P, N), (y_diag + y_off).reshape(b, Lc, H, P)

    h_last, ys = lax.scan(step, h0.astype(f32),
                          (chunks(xdt), chunks(a), chunks(B.astype(f32)), chunks(C.astype(f32))))
    return jnp.moveaxis(ys, 0, 1).reshape(b, L, H, P), h_last


def ssd_branch(z, xbc_raw, dt_raw, conv_prev, ssm_h0, conv_w, conv_b, dt_bias, a_log, d_skip, ssm_norm_w):
    bsz, L, _ = z.shape
    f32 = jnp.float32
    xp = jnp.concatenate([conv_prev.astype(xbc_raw.dtype), xbc_raw], axis=1)
    conv_new = xp[:, -(CONV_W - 1):]
    xbc = causal_dwconv_silu(xp, conv_w, conv_b)
    xs, bm, cm = split_cols(xbc, (D_INNER, SSM_GROUPS * SSM_STATE, SSM_GROUPS * SSM_STATE))
    xs = xs.reshape(bsz, L, SSM_HEADS, SSM_HEAD_DIM)
    bm = bm.reshape(bsz, L, SSM_GROUPS, SSM_STATE)
    cm = cm.reshape(bsz, L, SSM_GROUPS, SSM_STATE)
    dt = jax.nn.softplus(dt_raw.astype(f32) + dt_bias.astype(f32))
    A = -jnp.exp(a_log.astype(f32))
    y, h_new = ssd_scan(xs, dt, A, bm, cm, ssm_h0)
    y = y + d_skip.astype(f32)[:, None] * xs.astype(f32)
    y = y.reshape(bsz, L, D_INNER) * jax.nn.silu(z.astype(f32))
    yg = y.reshape(bsz, L, SSM_GROUPS, D_INNER // SSM_GROUPS)
    yg = yg * lax.rsqrt(jnp.mean(jnp.square(yg), axis=-1, keepdims=True) + RMS_EPS)
    y = yg.reshape(bsz, L, D_INNER) * ssm_norm_w.astype(f32)
    return y.astype(z.dtype), conv_new, h_new


def dilated_band_prompt(q, k, v, window, dil):
    b, L, h, e = q.shape
    span = window // dil
    M = L // dil
    nb = -(-M // ATT_BLOCK)
    Mp = nb * ATT_BLOCK

    def by_residue(t):
        return t.reshape(b, M, dil, h, e).transpose(0, 2, 1, 3, 4)

    qb = jnp.pad(by_residue(q), ((0, 0), (0, 0), (0, Mp - M), (0, 0), (0, 0))).reshape(b, dil, nb, ATT_BLOCK, h, e)

    def band(t):
        tp = jnp.pad(by_residue(t), ((0, 0), (0, 0), (ATT_BLOCK, Mp - M), (0, 0), (0, 0)))
        prev = tp[:, :, :Mp].reshape(b, dil, nb, ATT_BLOCK, h, e)
        cur = tp[:, :, ATT_BLOCK:].reshape(b, dil, nb, ATT_BLOCK, h, e)
        return jnp.concatenate([prev, cur], axis=3)

    kb, vb = band(k), band(v)
    s = jnp.einsum('brnqhe,brnkhe->brnhqk', qb, kb, preferred_element_type=jnp.float32) * (e ** -0.5)
    qm = jnp.arange(nb)[:, None, None] * ATT_BLOCK + jnp.arange(ATT_BLOCK)[None, :, None]
    km = jnp.arange(nb)[:, None, None] * ATT_BLOCK - ATT_BLOCK + jnp.arange(2 * ATT_BLOCK)[None, None, :]
    rel = qm - km
    valid = (rel >= 0) & (rel <= span) & (km >= 0)
    s = jnp.where(valid[None, None, :, None], s, -jnp.inf)
    m = jnp.max(s, axis=-1, keepdims=True)
    p = jnp.exp(s - m)
    den = jnp.sum(p, axis=-1)
    o = jnp.einsum('brnhqk,brnkhe->brnqhe', p, vb.astype(jnp.float32)) / jnp.moveaxis(den, 3, 4)[..., None]
    lse = jnp.moveaxis(m[..., 0] + jnp.log(den), 3, 4)
    o = o.reshape(b, dil, Mp, h, e)[:, :, :M].transpose(0, 2, 1, 3, 4).reshape(b, L, h, e)
    lse = lse.reshape(b, dil, Mp, h)[:, :, :M].transpose(0, 2, 1, 3).reshape(b, L, h)
    return o, lse


def dilated_gather_sample(q, k_new, v_new, k_buf, v_buf, window, dil):
    b, Lq, h, e = q.shape
    Lc = k_buf.shape[1]
    kk = jnp.concatenate([k_buf.astype(k_new.dtype), k_new], axis=1)
    vv = jnp.concatenate([v_buf.astype(v_new.dtype), v_new], axis=1)
    j = jnp.arange(window // dil + 1)
    idx = Lc + jnp.arange(Lq)[:, None] - dil * j[None, :]
    valid = idx >= 0
    idx = jnp.maximum(idx, 0)
    kg = kk[:, idx]
    vg = vv[:, idx]
    s = jnp.einsum('bqhe,bqjhe->bqhj', q, kg, preferred_element_type=jnp.float32) * (e ** -0.5)
    s = jnp.where(valid[None, :, None, :], s, -jnp.inf)
    m = jnp.max(s, axis=-1, keepdims=True)
    p = jnp.exp(s - m)
    den = jnp.sum(p, axis=-1)
    o = jnp.einsum('bqhj,bqjhe->bqhe', p, vg.astype(jnp.float32)) / den[..., None]
    return o, m[..., 0] + jnp.log(den)


def combine_by_denominator(outs):
    w = jax.nn.softmax(jnp.stack([l for _, l in outs]), axis=0)
    o = jnp.stack([o for o, _ in outs])
    return jnp.einsum('gblh,gblhe->blhe', w, o)


def token_mixing(h, pos, conv_prev, ssm_h0, win_kv, w_in, conv_w, conv_b, dt_bias, a_log, d_skip,
                 ssm_norm_w, w_branch_ssm, w_branch_att, w_mix_out):
    bsz, L, _ = h.shape
    z, xbc, dt_raw, q, k, v, g_ssm, g_att = split_cols(h @ w_in, IN_WIDTHS)
    y_ssm, conv_new, ssm_new = ssd_branch(z, xbc, dt_raw, conv_prev, ssm_h0, conv_w, conv_b,
                                          dt_bias, a_log, d_skip, ssm_norm_w)
    q = rotary(q.reshape(bsz, L, ATT_HEADS, ATT_HEAD_DIM), pos)
    k = rotary(k.reshape(bsz, L, ATT_HEADS, ATT_HEAD_DIM), pos)
    v = v.reshape(bsz, L, ATT_HEADS, ATT_HEAD_DIM)
    group_out, kv_new = [], []
    for g, (window, dil) in enumerate(DIL_CONFIGS):
        hs = slice(g * HEADS_PER_GROUP, (g + 1) * HEADS_PER_GROUP)
        qg, kg, vg = q[:, :, hs], k[:, :, hs], v[:, :, hs]
        if win_kv is None:
            group_out.append(dilated_band_prompt(qg, kg, vg, window, dil))
            keep = min(window, L)
            kv_new += [kg[:, L - keep:], vg[:, L - keep:]]
        else:
            group_out.append(dilated_gather_sample(qg, kg, vg, win_kv[2 * g], win_kv[2 * g + 1], window, dil))
            kv_new += [kg, vg]
    att = combine_by_denominator(group_out).reshape(bsz, L, ATT_OUT_WIDTH).astype(h.dtype)
    merged = jax.nn.sigmoid(g_ssm) * (y_ssm @ w_branch_ssm) + jax.nn.sigmoid(g_att) * (att @ w_branch_att)
    return merged @ w_mix_out, conv_new, ssm_new, kv_new


def memory_cross_attention(h, mem_k, mem_v, w_mem_q, w_mem_o):
    bsz, L, _ = h.shape
    q = (h @ w_mem_q).reshape(bsz, L, MEM_HEADS, MEM_HEAD_DIM)
    s = jnp.einsum('bqhe,bkhe->bhqk', q, mem_k.astype(q.dtype),
                   preferred_element_type=jnp.float32) * (MEM_HEAD_DIM ** -0.5)
    p = jax.nn.softmax(s, axis=-1)
    o = jnp.einsum('bhqk,bkhe->bqhe', p, mem_v.astype(jnp.float32))
    return o.reshape(bsz, L, D_MODEL).astype(h.dtype) @ w_mem_o


def moe_ffn(h, w_router, b_router, w_gate_up, b_gate_up, w_down, b_down):
    bsz, L, D = h.shape
    T = bsz * L
    x2 = h.reshape(T, D)
    logits = (x2 @ w_router).astype(jnp.float32) + b_router.astype(jnp.float32)
    top_val, top_idx = lax.top_k(logits, TOP_K)
    gates = jax.nn.softmax(top_val, axis=-1)
    A = T * TOP_K
    flat_e = top_idx.reshape(A)
    order = jnp.argsort(flat_e)
    sorted_e = flat_e[order]
    counts = jnp.bincount(flat_e, length=N_EXPERTS)
    padded = (counts + MOE_BLOCK - 1) // MOE_BLOCK * MOE_BLOCK
    pad_end = jnp.cumsum(padded)
    pad_start = pad_end - padded
    start = jnp.cumsum(counts) - counts
    slot_sorted = pad_start[sorted_e] + jnp.arange(A, dtype=jnp.int32) - start[sorted_e]
    n_blocks = -(-A // MOE_BLOCK) + N_EXPERTS
    n_slots = n_blocks * MOE_BLOCK
    slot_token = jnp.full((n_slots,), T, dtype=jnp.int32).at[slot_sorted].set((order // TOP_K).astype(jnp.int32))
    block_expert = jnp.minimum(jnp.searchsorted(pad_end, jnp.arange(n_blocks) * MOE_BLOCK, side='right'),
                               N_EXPERTS - 1)
    x_ext = jnp.concatenate([x2, jnp.zeros((1, D), x2.dtype)], axis=0)

    def expert_block(args):
        tok, e = args
        xb = x_ext[tok]
        gu = xb @ w_gate_up[e] + b_gate_up[e]
        gate = jnp.minimum(gu[:, :D_EXPERT], SWIGLU_LIMIT)
        up = jnp.clip(gu[:, D_EXPERT:], -SWIGLU_LIMIT, SWIGLU_LIMIT)
        act = (up + 1.0) * gate * jax.nn.sigmoid(gate * SWIGLU_ALPHA)
        return act @ w_down[e] + b_down[e]

    y_slots = lax.map(expert_block, (slot_token.reshape(n_blocks, MOE_BLOCK), block_expert)).reshape(n_slots, D)
    slot_of_assign = jnp.zeros((A,), dtype=jnp.int32).at[order].set(slot_sorted)
    y_assign = y_slots[slot_of_assign].reshape(T, TOP_K, D)
    y = jnp.einsum('tk,tkd->td', gates.astype(y_assign.dtype), y_assign)
    return y.reshape(bsz, L, D)


def decoder_layer(h, pos, conv_prev, ssm_h0, win_kv, mem_k, mem_v,
                  w_in, conv_w, conv_b, dt_bias, a_log, d_skip, ssm_norm_w, w_branch_ssm, w_branch_att,
                  w_mix_out, ln1_g, ln1_b, w_mem_q, w_mem_o, ln2_g, ln2_b, w_router, b_router,
                  w_gate_up, b_gate_up, w_down, b_down, ln3_g, ln3_b):
    mix, conv_new, ssm_new, kv_new = token_mixing(h, pos, conv_prev, ssm_h0, win_kv, w_in, conv_w, conv_b,
                                                  dt_bias, a_log, d_skip, ssm_norm_w, w_branch_ssm,
                                                  w_branch_att, w_mix_out)
    h = layer_norm(DN_ALPHA * h + mix, ln1_g, ln1_b)
    h = layer_norm(DN_ALPHA * h + memory_cross_attention(h, mem_k, mem_v, w_mem_q, w_mem_o), ln2_g, ln2_b)
    h = layer_norm(DN_ALPHA * h + moe_ffn(h, w_router, b_router, w_gate_up, b_gate_up, w_down, b_down),
                   ln3_g, ln3_b)
    return h, conv_new, ssm_new, kv_new


def setup_inputs(seed: int = 0) -> dict:
    key = jax.random.key(seed)
    keys = iter(jax.random.split(key, 64))
    f32 = jnp.float32

    def normal(shape, scale):
        return jax.random.normal(next(keys), shape, f32) * scale

    def gain(shape):
        return 1.0 + normal(shape, 0.1)

    def win_shape(window):
        return (DEPTH, DEC_BATCH, min(window, PAST_LEN), HEADS_PER_GROUP, ATT_HEAD_DIM)

    mem_shape = (DEPTH, DEC_BATCH, N_MEM, MEM_HEADS, MEM_HEAD_DIM)
    dt0 = jnp.exp(jax.random.uniform(next(keys), (DEPTH, SSM_HEADS), f32, math.log(1e-3), math.log(1e-1)))
    a_log = jnp.log(jax.random.uniform(next(keys), (DEPTH, SSM_HEADS), f32, 1.0, 16.0))
    return {
        'x_prompt': normal((BATCH, SEQ, D_MODEL), 1.0),
        'x_sample': normal((DEC_BATCH, DEC_SEQ, D_MODEL), 1.0),
        'state_conv': normal((DEPTH, DEC_BATCH, CONV_W - 1, CONV_DIM), 1.0),
        'state_ssm': normal((DEPTH, DEC_BATCH, SSM_HEADS, SSM_HEAD_DIM, SSM_STATE), 0.1),
        'cache_k_w128': normal(win_shape(128), 1.0),
        'cache_v_w128': normal(win_shape(128), 1.0),
        'cache_k_w512': normal(win_shape(512), 1.0),
        'cache_v_w512': normal(win_shape(512), 1.0),
        'cache_k_w2048': normal(win_shape(2048), 1.0),
        'cache_v_w2048': normal(win_shape(2048), 1.0),
        'cache_mem_k': normal(mem_shape, 1.0),
        'cache_mem_v': normal(mem_shape, DN_BETA),
        'mem_prompt': normal((BATCH, N_MEM, D_MODEL), 1.0),
        'ln_in_g': gain((D_MODEL,)),
        'ln_in_b': normal((D_MODEL,), 0.02),
        'w_in': normal((DEPTH, D_MODEL, IN_WIDTH), D_MODEL ** -0.5),
        'conv_w': normal((DEPTH, CONV_W, CONV_DIM), CONV_W ** -0.5),
        'conv_b': normal((DEPTH, CONV_DIM), 0.02),
        'dt_bias': dt0 + jnp.log(-jnp.expm1(-dt0)),
        'a_log': a_log,
        'd_skip': gain((DEPTH, SSM_HEADS)),
        'ssm_norm_w': gain((DEPTH, D_INNER)),
        'w_branch_ssm': normal((DEPTH, D_INNER, D_MODEL), DN_BETA * D_INNER ** -0.5),
        'w_branch_att': normal((DEPTH, ATT_OUT_WIDTH, D_MODEL), DN_BETA * ATT_OUT_WIDTH ** -0.5),
        'w_mix_out': normal((DEPTH, D_MODEL, D_MODEL), DN_BETA * D_MODEL ** -0.5),
        'ln1_g': gain((DEPTH, D_MODEL)),
        'ln1_b': normal((DEPTH, D_MODEL), 0.02),
        'w_mem_q': normal((DEPTH, D_MODEL, D_MODEL), D_MODEL ** -0.5),
        'w_mem_k': normal((DEPTH, D_MODEL, D_MODEL), D_MODEL ** -0.5),
        'w_mem_v': normal((DEPTH, D_MODEL, D_MODEL), DN_BETA * D_MODEL ** -0.5),
        'w_mem_o': normal((DEPTH, D_MODEL, D_MODEL), DN_BETA * D_MODEL ** -0.5),
        'ln2_g': gain((DEPTH, D_MODEL)),
        'ln2_b': normal((DEPTH, D_MODEL), 0.02),
        'w_router': normal((DEPTH, D_MODEL, N_EXPERTS), D_MODEL ** -0.5),
        'b_router': normal((DEPTH, N_EXPERTS), 0.01),
        'w_gate_up': normal((DEPTH, N_EXPERTS, D_MODEL, 2 * D_EXPERT), DN_BETA * D_MODEL ** -0.5),
        'b_gate_up': normal((DEPTH, N_EXPERTS, 2 * D_EXPERT), 0.01),
        'w_down': normal((DEPTH, N_EXPERTS, D_EXPERT, D_MODEL), DN_BETA * D_EXPERT ** -0.5),
        'b_down': normal((DEPTH, N_EXPERTS, D_MODEL), 0.01),
        'ln3_g': gain((DEPTH, D_MODEL)),
        'ln3_b': normal((DEPTH, D_MODEL), 0.02),
    }


def reference(x_prompt, x_sample, state_conv, state_ssm, cache_k_w128, cache_v_w128, cache_k_w512,
              cache_v_w512, cache_k_w2048, cache_v_w2048, cache_mem_k, cache_mem_v, mem_prompt,
              ln_in_g, ln_in_b, w_in, conv_w, conv_b, dt_bias, a_log, d_skip, ssm_norm_w, w_branch_ssm,
              w_branch_att, w_mix_out, ln1_g, ln1_b, w_mem_q, w_mem_k, w_mem_v, w_mem_o, ln2_g, ln2_b,
              w_router, b_router, w_gate_up, b_gate_up, w_down, b_down, ln3_g, ln3_b):
    bp, lp, _ = x_prompt.shape
    ls = x_sample.shape[1]
    pos_p = jnp.arange(lp, dtype=jnp.int32)
    pos_s = PAST_LEN + jnp.arange(ls, dtype=jnp.int32)
    hp = layer_norm(x_prompt, ln_in_g, ln_in_b)
    hs = layer_norm(x_sample, ln_in_g, ln_in_b)
    conv_p_l, ssm_p_l, kv_p_l, mem_k_l, mem_v_l = [], [], [], [], []
    conv_s_l, ssm_s_l, kv_s_l = [], [], []
    for l in range(DEPTH):
        layer_w = (w_in[l], conv_w[l], conv_b[l], dt_bias[l], a_log[l], d_skip[l], ssm_norm_w[l],
                   w_branch_ssm[l], w_branch_att[l], w_mix_out[l], ln1_g[l], ln1_b[l], w_mem_q[l], w_mem_o[l],
                   ln2_g[l], ln2_b[l], w_router[l], b_router[l], w_gate_up[l], b_gate_up[l], w_down[l],
                   b_down[l], ln3_g[l], ln3_b[l])
        mk = (mem_prompt @ w_mem_k[l]).reshape(bp, N_MEM, MEM_HEADS, MEM_HEAD_DIM)
        mv = (mem_prompt @ w_mem_v[l]).reshape(bp, N_MEM, MEM_HEADS, MEM_HEAD_DIM)
        conv0 = jnp.zeros((bp, CONV_W - 1, CONV_DIM), hp.dtype)
        ssm0 = jnp.zeros((bp, SSM_HEADS, SSM_HEAD_DIM, SSM_STATE), jnp.float32)
        hp, c, s, kv = decoder_layer(hp, pos_p, conv0, ssm0, None, mk, mv, *layer_w)
        conv_p_l.append(c)
        ssm_p_l.append(s)
        kv_p_l.append(kv)
        mem_k_l.append(mk)
        mem_v_l.append(mv)
        win = (cache_k_w128[l], cache_v_w128[l], cache_k_w512[l], cache_v_w512[l],
               cache_k_w2048[l], cache_v_w2048[l])
        hs, c, s, kv = decoder_layer(hs, pos_s, state_conv[l], state_ssm[l], win,
                                     cache_mem_k[l], cache_mem_v[l], *layer_w)
        conv_s_l.append(c)
        ssm_s_l.append(s)
        kv_s_l.append(kv)
    y_prompt = hp
    y_sample = hs
    conv_p = jnp.stack(conv_p_l)
    ssm_p = jnp.stack(ssm_p_l)
    mem_k_p = jnp.stack(mem_k_l)
    mem_v_p = jnp.stack(mem_v_l)
    k128_p, v128_p, k512_p, v512_p, k2048_p, v2048_p = [jnp.stack(t) for t in zip(*kv_p_l)]
    conv_s = jnp.stack(conv_s_l)
    ssm_s = jnp.stack(ssm_s_l)
    k128_s, v128_s, k512_s, v512_s, k2048_s, v2048_s = [jnp.stack(t) for t in zip(*kv_s_l)]
    return (y_prompt, y_sample, conv_p, ssm_p, k128_p, v128_p, k512_p, v512_p, k2048_p, v2048_p,
            mem_k_p, mem_v_p, conv_s, ssm_s, k128_s, v128_s, k512_s, v512_s, k2048_s, v2048_s)
```

```python
import functools
import math

import jax
import jax.numpy as jnp
from jax import lax
from jax.experimental import pallas as pl
from jax.experimental.pallas import tpu as pltpu

F32 = jnp.float32
BF16 = jnp.bfloat16

PAST_LEN = 16384
SSM_HEAD_DIM = 64
SSM_GROUPS = 8
SSM_STATE = 128
CONV_W = 4
SSM_CHUNK = 128
RMS_EPS = 1e-5
DIL_CONFIGS = ((128, 1), (512, 4), (2048, 16))
HEADS_PER_GROUP = 4
ATT_HEAD_DIM = 128
ATT_BLOCK = 128
ROT_DIM = ATT_HEAD_DIM // 4
ROPE_THETA = 500000.0
MEM_HEADS = 4
TOP_K = 4
SWIGLU_LIMIT = 7.0
SWIGLU_ALPHA = 1.702
LN_EPS = 1e-5
DEPTH = 1
DN_ALPHA = (2.0 * DEPTH) ** 0.25

LANES = 128
SUBLANES = 8
ROW_TILE = 512
MOE_TM = 512
MOE_TN = 512
VMEM_LIMIT = 56 * 1024 * 1024
HI = lax.Precision.HIGHEST


def _params(*sem):
    return pltpu.CompilerParams(dimension_semantics=sem, vmem_limit_bytes=VMEM_LIMIT)


def _sigmoid(x):
    return 1.0 / (1.0 + jnp.exp(-x))


def _layer_norm_rows(x, g, b):
    mu = jnp.mean(x, axis=-1, keepdims=True)
    xc = x - mu
    var = jnp.mean(xc * xc, axis=-1, keepdims=True)
    return xc * lax.rsqrt(var + LN_EPS) * g + b


def _ln_kernel(x_ref, g_ref, b_ref, of_ref, ob_ref):
    y = _layer_norm_rows(x_ref[...], g_ref[...], b_ref[...])
    of_ref[...] = y
    ob_ref[...] = y.astype(BF16)


def _ln_call(x, g, b, tm=256):
    t, d = x.shape
    row = pl.BlockSpec((tm, d), lambda i: (i, 0))
    vec = pl.BlockSpec((1, d), lambda i: (0, 0))
    return pl.pallas_call(
        _ln_kernel, grid=(t // tm,), in_specs=[row, vec, vec], out_specs=[row, row],
        out_shape=[jax.ShapeDtypeStruct((t, d), F32), jax.ShapeDtypeStruct((t, d), BF16)],
        compiler_params=_params("parallel"), name="ln_in",
    )(x, g.reshape(1, d), b.reshape(1, d))


def _mm_kernel(a_ref, w_ref, o_ref):
    o_ref[...] = jnp.dot(a_ref[...], w_ref[...], preferred_element_type=F32).astype(o_ref.dtype)


def _mm_call(a, w, out_dtype, tm, tn, name):
    m, k = a.shape
    n = w.shape[1]
    tm = min(tm, m)
    return pl.pallas_call(
        _mm_kernel, grid=(n // tn, m // tm),
        in_specs=[pl.BlockSpec((tm, k), lambda j, i: (i, 0)), pl.BlockSpec((k, tn), lambda j, i: (0, j))],
        out_specs=pl.BlockSpec((tm, tn), lambda j, i: (i, j)),
        out_shape=jax.ShapeDtypeStruct((m, n), out_dtype),
        compiler_params=_params("parallel", "parallel"), name=name,
    )(a, w)


def _mm_res_ln_kernel(a_ref, w_ref, h_ref, g_ref, b_ref, of_ref, ob_ref):
    mix = jnp.dot(a_ref[...], w_ref[...], preferred_element_type=F32)
    y = _layer_norm_rows(DN_ALPHA * h_ref[...] + mix, g_ref[...], b_ref[...])
    of_ref[...] = y
    ob_ref[...] = y.astype(BF16)


def _mm_res_ln_call(a, w, h, g, b, name, tm=256):
    m, k = a.shape
    d = w.shape[1]
    row = pl.BlockSpec((tm, d), lambda i: (i, 0))
    vec = pl.BlockSpec((1, d), lambda i: (0, 0))
    return pl.pallas_call(
        _mm_res_ln_kernel, grid=(m // tm,),
        in_specs=[pl.BlockSpec((tm, k), lambda i: (i, 0)), pl.BlockSpec((k, d), lambda i: (0, 0)), row, vec, vec],
        out_specs=[row, row],
        out_shape=[jax.ShapeDtypeStruct((m, d), F32), jax.ShapeDtypeStruct((m, d), BF16)],
        compiler_params=_params("parallel"), name=name,
    )(a, w, h, g.reshape(1, d), b.reshape(1, d))


def _branch_kernel(y_ref, att_ref, gs_ref, ga_ref, ws_ref, wa_ref, o_ref):
    bs = jnp.dot(y_ref[...], ws_ref[...], preferred_element_type=F32)
    ba = jnp.dot(att_ref[...], wa_ref[...], preferred_element_type=F32)
    merged = _sigmoid(gs_ref[...].astype(F32)) * bs + _sigmoid(ga_ref[...].astype(F32)) * ba
    o_ref[...] = merged.astype(o_ref.dtype)


def _branch_call(y_ssm, att, gates, ws, wa, tm=512, tn=1024):
    m, ks = y_ssm.shape
    ka = att.shape[1]
    d = ws.shape[1]
    nj = d // tn
    return pl.pallas_call(
        _branch_kernel, grid=(nj, m // tm),
        in_specs=[pl.BlockSpec((tm, ks), lambda j, i: (i, 0)),
                  pl.BlockSpec((tm, ka), lambda j, i: (i, 0)),
                  pl.BlockSpec((tm, tn), lambda j, i: (i, j)),
                  pl.BlockSpec((tm, tn), lambda j, i: (i, nj + j)),
                  pl.BlockSpec((ks, tn), lambda j, i: (0, j)),
                  pl.BlockSpec((ka, tn), lambda j, i: (0, j))],
        out_specs=pl.BlockSpec((tm, tn), lambda j, i: (i, j)),
        out_shape=jax.ShapeDtypeStruct((m, d), BF16),
        compiler_params=_params("parallel", "parallel"), name="branch_merge",
    )(y_ssm, att, gates, gates, ws, wa)


def _rotary_kernel(qkv_ref, cos_ref, sin_ref, *out_refs):
    n_grp = len(DIL_CONFIGS)
    q_refs, k_refs, v_refs = out_refs[:n_grp], out_refs[n_grp:2 * n_grp], out_refs[2 * n_grp:]
    cos = cos_ref[...]
    sin = sin_ref[...]
    half = ROT_DIM // 2
    lane = lax.broadcasted_iota(jnp.int32, cos.shape, 1)
    width = n_grp * HEADS_PER_GROUP * ATT_HEAD_DIM

    def rot(x):
        partner = jnp.where(lane < half, pltpu.roll(x, ATT_HEAD_DIM - half, 1), pltpu.roll(x, half, 1))
        return x * cos + partner * sin

    for g in range(n_grp):
        for h in range(HEADS_PER_GROUP):
            src = (g * HEADS_PER_GROUP + h) * ATT_HEAD_DIM
            dst = slice(h * ATT_HEAD_DIM, (h + 1) * ATT_HEAD_DIM)
            q_refs[g][:, dst] = rot(qkv_ref[:, src:src + ATT_HEAD_DIM])
            k_refs[g][:, dst] = rot(qkv_ref[:, width + src:width + src + ATT_HEAD_DIM])
            v_refs[g][:, dst] = qkv_ref[:, 2 * width + src:2 * width + src + ATT_HEAD_DIM]


def _rotary_call(qkv, cos_t, sin_t, tm=256):
    t, w3 = qkv.shape
    gw = HEADS_PER_GROUP * ATT_HEAD_DIM
    n_out = 3 * len(DIL_CONFIGS)
    row = pl.BlockSpec((tm, gw), lambda i: (i, 0))
    tab = pl.BlockSpec((tm, ATT_HEAD_DIM), lambda i: (i, 0))
    return pl.pallas_call(
        _rotary_kernel, grid=(t // tm,),
        in_specs=[pl.BlockSpec((tm, w3), lambda i: (i, 0)), tab, tab],
        out_specs=[row] * n_out,
        out_shape=[jax.ShapeDtypeStruct((t, gw), F32)] * n_out,
        compiler_params=_params("parallel"), name="rotary_split",
    )(qkv, cos_t, sin_t)


def _rotary_tables(pos):
    half = ROT_DIM // 2
    inv_freq = jnp.exp(-math.log(ROPE_THETA) * jnp.arange(half, dtype=F32) * (2.0 / ROT_DIM))
    ang = pos.astype(F32)[:, None] * inv_freq[None, :]
    cos, sin = jnp.cos(ang), jnp.sin(ang)
    rest = ATT_HEAD_DIM - ROT_DIM
    ones = jnp.ones((pos.shape[0], rest), F32)
    cos_t = jnp.concatenate([cos, cos, ones], axis=1)
    sin_t = jnp.concatenate([-sin, sin, jnp.zeros_like(ones)], axis=1)
    return cos_t, sin_t


def _band_attn_kernel(q_ref, kc_ref, kp_ref, vc_ref, vp_ref, o_ref, lse_ref, *, n_res, span, has_prev):
    n = pl.program_id(1)
    scale = ATT_HEAD_DIM ** -0.5
    qi = lax.broadcasted_iota(jnp.int32, (ATT_BLOCK, ATT_BLOCK), 0)
    ki = lax.broadcasted_iota(jnp.int32, (ATT_BLOCK, ATT_BLOCK), 1)
    rel_c = qi - ki
    valid_c = (rel_c >= 0) & (rel_c <= span)
    rel_p = rel_c + ATT_BLOCK
    valid_p = (rel_p <= span) & (n > 0)
    nt = (((1,), (1,)), ((), ()))
    for rh in range(n_res * HEADS_PER_GROUP):
        cs = slice(rh * ATT_HEAD_DIM, (rh + 1) * ATT_HEAD_DIM)
        q = q_ref[:, cs].astype(BF16)
        s_c = lax.dot_general(q, kc_ref[:, cs].astype(BF16), nt, preferred_element_type=F32) * scale
        s_c = jnp.where(valid_c, s_c, -jnp.inf)
        m = jnp.max(s_c, axis=-1, keepdims=True)
        if has_prev:
            s_p = lax.dot_general(q, kp_ref[:, cs].astype(BF16), nt, preferred_element_type=F32) * scale
            s_p = jnp.where(valid_p, s_p, -jnp.inf)
            m = jnp.maximum(m, jnp.max(s_p, axis=-1, keepdims=True))
        p_c = jnp.exp(s_c - m)
        den = jnp.sum(p_c, axis=-1, keepdims=True)
        o = jnp.dot(p_c.astype(BF16), vc_ref[:, cs].astype(BF16), preferred_element_type=F32)
        if has_prev:
            p_p = jnp.exp(s_p - m)
            den = den + jnp.sum(p_p, axis=-1, keepdims=True)
            o = o + jnp.dot(p_p.astype(BF16), vp_ref[:, cs].astype(BF16), preferred_element_type=F32)
        o_ref[:, cs] = o / den
        lse_ref[:, cs] = jnp.broadcast_to(m + jnp.log(den), (ATT_BLOCK, ATT_HEAD_DIM))


def _band_attn_call(q, k, v, bsz, seq, window, dil, g):
    t_pad, gw = q.shape
    span = window // dil
    assert span <= ATT_BLOCK and seq % (dil * ATT_BLOCK) == 0 and t_pad % dil == 0
    nb = seq // (dil * ATT_BLOCK)
    n_res = min(dil, 4)
    n_cb = dil // n_res
    cw = n_res * gw
    view = lambda a: a.reshape(t_pad // dil, dil * gw)
    cur = pl.BlockSpec((ATT_BLOCK, cw), lambda b, n, c: (b * nb + n, c))
    prev = pl.BlockSpec((ATT_BLOCK, cw), lambda b, n, c: (b * nb + jnp.maximum(n - 1, 0), c))
    kern = functools.partial(_band_attn_kernel, n_res=n_res, span=span, has_prev=nb > 1)
    o, lse = pl.pallas_call(
        kern, grid=(bsz, nb, n_cb),
        in_specs=[cur, cur, prev, cur, prev], out_specs=[cur, cur],
        out_shape=[jax.ShapeDtypeStruct((t_pad // dil, dil * gw), F32)] * 2,
        compiler_params=_params("parallel", "parallel", "parallel"), name=f"band_attn_g{g}",
    )(view(q), view(k), view(k), view(v), view(v))
    return o.reshape(t_pad, gw), lse.reshape(t_pad, gw)


def _sample_attn_kernel(q_ref, kc_ref, vc_ref, kn_ref, vn_ref, o_ref, lse_ref, *, dil, lq):
    scale = ATT_HEAD_DIM ** -0.5
    n_cache = kc_ref.shape[1]
    row = lax.broadcasted_iota(jnp.int32, (n_cache, 1), 0)
    for i in range(lq):
        res = i if dil > 1 else 0
        new_rows = range(i + 1) if dil == 1 else (i,)
        for h in range(HEADS_PER_GROUP):
            hs = slice(h * ATT_HEAD_DIM, (h + 1) * ATT_HEAD_DIM)
            cs = slice((res * HEADS_PER_GROUP + h) * ATT_HEAD_DIM, (res * HEADS_PER_GROUP + h + 1) * ATT_HEAD_DIM)
            qv = q_ref[0, i:i + 1, hs]
            s_col = jnp.sum(kc_ref[0, :, cs] * qv, axis=1, keepdims=True) * scale
            if dil == 1:
                s_col = jnp.where(row >= i, s_col, -jnp.inf)
            s_new = [jnp.sum(kn_ref[0, j:j + 1, hs] * qv, axis=1, keepdims=True) * scale for j in new_rows]
            m = jnp.max(s_col, axis=0, keepdims=True)
            for s in s_new:
                m = jnp.maximum(m, s)
            p_col = jnp.exp(s_col - m)
            den = jnp.sum(p_col, axis=0, keepdims=True)
            o = jnp.sum(p_col * vc_ref[0, :, cs], axis=0, keepdims=True)
            for j, s in zip(new_rows, s_new):
                p = jnp.exp(s - m)
                den = den + p
                o = o + p * vn_ref[0, j:j + 1, hs]
            o_ref[0, i:i + 1, hs] = o / den
            lse_ref[0, i:i + 1, hs] = jnp.broadcast_to(m + jnp.log(den), (1, ATT_HEAD_DIM))


def _sample_attn_call(q, k_new, v_new, k_buf, v_buf, window, dil, g):
    b, lq, gw = q.shape
    assert k_buf.shape[1] == window and window % dil == 0 and (dil == 1 or dil >= lq) and lq <= window // dil
    n_res = min(dil, lq)
    n_cache = window // dil
    cache = lambda a: a.reshape(b, n_cache, dil * gw)
    new = pl.BlockSpec((1, lq, gw), lambda i: (i, 0, 0))
    buf = pl.BlockSpec((1, n_cache, n_res * gw), lambda i: (i, 0, 0))
    kern = functools.partial(_sample_attn_kernel, dil=dil, lq=lq)
    return pl.pallas_call(
        kern, grid=(b,), in_specs=[new, buf, buf, new, new], out_specs=[new, new],
        out_shape=[jax.ShapeDtypeStruct((b, lq, gw), F32)] * 2,
        compiler_params=_params("parallel"), name=f"sample_attn_g{g}",
    )(q, cache(k_buf), cache(v_buf), k_new, v_new)


def _combine_kernel(*refs):
    n_grp = len(DIL_CONFIGS)
    o_refs, l_refs, out_ref = refs[:n_grp], refs[n_grp:2 * n_grp], refs[2 * n_grp]
    ls = [r[...] for r in l_refs]
    m = functools.reduce(jnp.maximum, ls)
    ws = [jnp.exp(l - m) for l in ls]
    tot = functools.reduce(lambda a, b: a + b, ws)
    acc = functools.reduce(lambda a, b: a + b, [w * r[...] for w, r in zip(ws, o_refs)])
    out_ref[...] = (acc / tot).astype(out_ref.dtype)


def _combine_call(outs, lses, tm=512):
    t, gw = outs[0].shape
    row = pl.BlockSpec((tm, gw), lambda i: (i, 0))
    return pl.pallas_call(
        _combine_kernel, grid=(t // tm,), in_specs=[row] * (2 * len(outs)), out_specs=row,
        out_shape=jax.ShapeDtypeStruct((t, gw), BF16),
        compiler_params=_params("parallel"), name="combine_groups",
    )(*outs, *lses)


def _ssd_kernel(*refs, n_chunks, valid_len, has_state):
    if has_state:
        (z_ref, xbc_ref, dt_ref, h0_ref, cprev_ref, cw_ref, cb_ref, dtb_ref, alog_ref, dskip_ref, nw_ref,
         y_ref, hfin_ref, ht_ref, xext_ref) = refs
    else:
        (z_ref, xbc_ref, dt_ref, cw_ref, cb_ref, dtb_ref, alog_ref, dskip_ref, nw_ref,
         y_ref, hfin_ref, ht_ref, xext_ref) = refs
    c = pl.program_id(1)
    lc = SSM_CHUNK
    n_st = SSM_STATE
    gw = ht_ref.shape[1] // SSM_GROUPS
    d_inner = ht_ref.shape[1]
    heads_per_group = gw // SSM_HEAD_DIM
    n_tr = d_inner // LANES

    @pl.when(c == 0)
    def _():
        if has_state:
            xext_ref[0:SUBLANES, :] = cprev_ref[0]
            for k in range(n_tr):
                ht_ref[:, k * LANES:(k + 1) * LANES] = h0_ref[0, k * LANES:(k + 1) * LANES, :].T
        else:
            xext_ref[0:SUBLANES, :] = jnp.zeros((SUBLANES, xext_ref.shape[1]), F32)
            ht_ref[...] = jnp.zeros(ht_ref.shape, F32)

    xext_ref[SUBLANES:SUBLANES + lc, :] = xbc_ref[...]

    def conv_silu(c0, width):
        acc = cb_ref[:, c0:c0 + width]
        for s in range(CONV_W):
            acc = acc + cw_ref[CONV_W - 1 - s:CONV_W - s, c0:c0 + width] * xext_ref[pl.ds(SUBLANES - s, lc), c0:c0 + width]
        return acc * _sigmoid(acc)

    dtr = dt_ref[...] + dtb_ref[...]
    dt = jnp.maximum(dtr, 0.0) + jnp.log(1.0 + jnp.exp(-jnp.abs(dtr)))
    if valid_len < lc:
        trow = lax.broadcasted_iota(jnp.int32, dt.shape, 0)
        dt = jnp.where(trow < valid_len, dt, 0.0)
    a = dt * (-jnp.exp(alog_ref[...]))
    ti = lax.broadcasted_iota(jnp.int32, (lc, lc), 0)
    si = lax.broadcasted_iota(jnp.int32, (lc, lc), 1)
    tri = si <= ti
    acum = jnp.dot(tri.astype(F32), a, precision=HI, preferred_element_type=F32)
    acum_t = jnp.dot(a.T, (ti <= si).astype(F32), precision=HI, preferred_element_type=F32)
    lane = lax.broadcasted_iota(jnp.int32, (lc, LANES), 1)
    nt = (((1,), (1,)), ((), ()))

    for g in range(SSM_GROUPS):
        c0 = g * gw
        xs = conv_silu(c0, gw)
        bm = conv_silu(d_inner + g * n_st, n_st)
        cm = conv_silu(d_inner + SSM_GROUPS * n_st + g * n_st, n_st)
        eh = lax.broadcasted_iota(jnp.int32, (LANES, gw), 0)
        ej = lax.broadcasted_iota(jnp.int32, (LANES, gw), 1)
        expand = (eh == g * heads_per_group + (ej >> (SSM_HEAD_DIM.bit_length() - 1))).astype(F32)
        dt_x = jnp.dot(dt, expand, precision=HI, preferred_element_type=F32)
        ac_x = jnp.dot(acum, expand, precision=HI, preferred_element_type=F32)
        xdt = xs * dt_x
        xdt_b = xdt.astype(BF16)
        cm_b = cm.astype(BF16)
        cb = lax.dot_general(cm_b, bm.astype(BF16), nt, preferred_element_type=F32)
        h_old = ht_ref[:, c0:c0 + gw]
        y = jnp.dot(cm_b, h_old.astype(BF16), preferred_element_type=F32) * jnp.exp(ac_x)
        diag = []
        for k in range(gw // LANES):
            x_pair = xdt_b[:, k * LANES:(k + 1) * LANES]
            y_pair = None
            for hh in range(LANES // SSM_HEAD_DIM):
                h = g * heads_per_group + k * (LANES // SSM_HEAD_DIM) + hh
                seg = acum[:, h:h + 1] - acum_t[h:h + 1, :]
                lmat = jnp.exp(jnp.where(tri, seg, -jnp.inf))
                m_b = (cb * lmat).astype(BF16)
                in_head = (lane >= hh * SSM_HEAD_DIM) & (lane < (hh + 1) * SSM_HEAD_DIM)
                part = jnp.dot(m_b, jnp.where(in_head, x_pair, jnp.zeros_like(x_pair)), preferred_element_type=F32)
                y_pair = part if y_pair is None else y_pair + part
            diag.append(y_pair)
        y = y + jnp.concatenate(diag, axis=1) + dskip_ref[:, c0:c0 + gw] * xs
        zg = z_ref[:, c0:c0 + gw].astype(F32)
        y = y * (zg * _sigmoid(zg))
        y = y * lax.rsqrt(jnp.mean(y * y, axis=-1, keepdims=True) + RMS_EPS)
        y_ref[:, c0:c0 + gw] = (y * nw_ref[:, c0:c0 + gw]).astype(y_ref.dtype)
        a_last = ac_x[lc - 1:lc, :]
        xw = (xdt * jnp.exp(a_last - ac_x)).astype(BF16)
        ht_ref[:, c0:c0 + gw] = h_old * jnp.exp(a_last) + jnp.dot(bm.T.astype(BF16), xw, preferred_element_type=F32)

    xext_ref[0:SUBLANES, :] = xext_ref[lc:lc + SUBLANES, :]

    @pl.when(c == n_chunks - 1)
    def _():
        for k in range(n_tr):
            hfin_ref[0, k * LANES:(k + 1) * LANES, :] = ht_ref[:, k * LANES:(k + 1) * LANES].T


def _ssd_call(z, xbc, dt_raw, n_seq, n_chunks, valid_len, state, conv_w, conv_b, dt_bias, a_log, d_skip, norm_w,
              out_rows, name):
    d_inner = z.shape[1]
    conv_dim = xbc.shape[1]
    n_heads = d_inner // SSM_HEAD_DIM
    lc = SSM_CHUNK
    pad_h = lambda v: jnp.pad(v.astype(F32), (0, LANES - n_heads)).reshape(1, LANES)
    rows = lambda w: pl.BlockSpec((lc, w), lambda b, c: (b * n_chunks + c, 0))
    const = lambda r, w: pl.BlockSpec((r, w), lambda b, c: (0, 0))
    per_seq = lambda r, w: pl.BlockSpec((1, r, w), lambda b, c: (b, 0, 0))
    args = [z, xbc, dt_raw]
    specs = [rows(d_inner), rows(conv_dim), rows(LANES)]
    if state is not None:
        args += list(state)
        specs += [per_seq(d_inner, SSM_STATE), per_seq(SUBLANES, conv_dim)]
    args += [conv_w, conv_b.reshape(1, conv_dim), pad_h(dt_bias), pad_h(a_log),
             jnp.repeat(d_skip.astype(F32), SSM_HEAD_DIM).reshape(1, d_inner), norm_w.reshape(1, d_inner)]
    specs += [const(CONV_W, conv_dim), const(1, conv_dim), const(1, LANES), const(1, LANES),
              const(1, d_inner), const(1, d_inner)]
    kern = functools.partial(_ssd_kernel, n_chunks=n_chunks, valid_len=valid_len, has_state=state is not None)
    return pl.pallas_call(
        kern, grid=(n_seq, n_chunks), in_specs=specs,
        out_specs=[rows(d_inner), per_seq(d_inner, SSM_STATE)],
        out_shape=[jax.ShapeDtypeStruct((out_rows, d_inner), BF16),
                   jax.ShapeDtypeStruct((n_seq, d_inner, SSM_STATE), F32)],
        scratch_shapes=[pltpu.VMEM((SSM_STATE, d_inner), F32), pltpu.VMEM((lc + 2 * SUBLANES, conv_dim), F32)],
        compiler_params=_params("parallel", "arbitrary"), name=name,
    )(*args)


def _mem_attn_kernel(q_ref, k_ref, v_ref, o_ref):
    scale = q_ref.shape[1] ** -0.5
    s = lax.dot_general(q_ref[...], k_ref[...].astype(BF16), (((1,), (1,)), ((), ())),
                        preferred_element_type=F32) * scale
    m = jnp.max(s, axis=-1, keepdims=True)
    p = jnp.exp(s - m)
    den = jnp.sum(p, axis=-1, keepdims=True)
    o = jnp.dot(p.astype(BF16), v_ref[...].astype(BF16), preferred_element_type=F32)
    o_ref[...] = (o / den).astype(o_ref.dtype)


def _mem_attn_call(q, mem_k, mem_v, n_seq, lq, tq, out_rows, name):
    d = q.shape[1]
    hd = d // MEM_HEADS
    n_mem = mem_k.shape[0] // n_seq
    nq = lq // tq
    qs = pl.BlockSpec((tq, hd), lambda b, h, i: (b * nq + i, h))
    ks = pl.BlockSpec((n_mem, hd), lambda b, h, i: (b, h))
    return pl.pallas_call(
        _mem_attn_kernel, grid=(n_seq, MEM_HEADS, nq), in_specs=[qs, ks, ks], out_specs=qs,
        out_shape=jax.ShapeDtypeStruct((out_rows, d), BF16),
        compiler_params=_params("parallel", "parallel", "parallel"), name=name,
    )(q, mem_k, mem_v)


def _router_kernel(x_ref, w_ref, b_ref, idx_ref, gate_ref):
    logits = jnp.dot(x_ref[...], w_ref[...], precision=HI, preferred_element_type=F32) + b_ref[...]
    lane = lax.broadcasted_iota(jnp.int32, logits.shape, 1)
    idx_out = jnp.zeros(logits.shape, jnp.int32)
    vals = []
    for k in range(TOP_K):
        m = jnp.max(logits, axis=-1, keepdims=True)
        pick = jnp.min(jnp.where(logits == m, lane, LANES), axis=-1, keepdims=True)
        idx_out = jnp.where(lane == k, pick, idx_out)
        logits = jnp.where(lane == pick, -jnp.inf, logits)
        vals.append(m)
    exps = [jnp.exp(v - vals[0]) for v in vals]
    tot = functools.reduce(lambda a, b: a + b, exps)
    gate_out = jnp.zeros(logits.shape, F32)
    for k in range(TOP_K):
        gate_out = jnp.where(lane == k, exps[k] / tot, gate_out)
    idx_ref[...] = idx_out
    gate_ref[...] = gate_out


def _router_call(x, w_router, b_router, tm=256):
    t, d = x.shape
    n_exp = w_router.shape[1]
    w = jnp.pad(w_router.astype(F32), ((0, 0), (0, LANES - n_exp)))
    b = jnp.pad(b_router.astype(F32), (0, LANES - n_exp), constant_values=-jnp.inf).reshape(1, LANES)
    out = pl.BlockSpec((tm, LANES), lambda i: (i, 0))
    return pl.pallas_call(
        _router_kernel, grid=(t // tm,),
        in_specs=[pl.BlockSpec((tm, d), lambda i: (i, 0)), pl.BlockSpec((d, LANES), lambda i: (0, 0)),
                  pl.BlockSpec((1, LANES), lambda i: (0, 0))],
        out_specs=[out, out],
        out_shape=[jax.ShapeDtypeStruct((t, LANES), jnp.int32), jax.ShapeDtypeStruct((t, LANES), F32)],
        compiler_params=_params("parallel"), name="router",
    )(x, w, b)


def _moe_kernel(nused_ref, bexp_ref, tokc_ref, tokn_ref, dest_ref, x_hbm, wg_ref, wu_ref, wd_ref,
                bg_ref, bu_ref, bd_ref, y_hbm, xbuf, xb, acc, obuf, gsem, ssem, *, nj):
    i = pl.program_id(0)
    j = pl.program_id(1)
    n_used = nused_ref[0]
    slot = i % 2
    tm = xb.shape[0]

    def gather_copy(tok, r, s):
        return pltpu.make_async_copy(x_hbm.at[pl.ds(tok, 1)], xbuf.at[s, pl.ds(r, 1)], gsem.at[s])

    def scatter_copy(dst, r, s):
        return pltpu.make_async_copy(obuf.at[s, pl.ds(r, 1)], y_hbm.at[pl.ds(dst, 1)], ssem.at[s])

    def start_gather(tok_ref, s):
        def body(r, carry):
            gather_copy(tok_ref[0, 0, r], r, s).start()
            return carry
        lax.fori_loop(0, tm, body, 0)

    def wait_gather(s):
        pltpu.make_async_copy(x_hbm.at[pl.ds(0, tm)], xbuf.at[s], gsem.at[s]).wait()

    def wait_scatter(s):
        pltpu.make_async_copy(obuf.at[s], y_hbm.at[pl.ds(0, tm)], ssem.at[s]).wait()

    @pl.when(i < n_used)
    def _():
        @pl.when(j == 0)
        def _():
            @pl.when(i == 0)
            def _():
                start_gather(tokc_ref, 0)

            @pl.when(i + 1 < n_used)
            def _():
                start_gather(tokn_ref, 1 - slot)

            wait_gather(slot)
            xb[...] = xbuf[slot].astype(BF16)
            acc[...] = jnp.zeros(acc.shape, F32)

        x = xb[...]
        gate = jnp.minimum(jnp.dot(x, wg_ref[0], preferred_element_type=F32) + bg_ref[0], SWIGLU_LIMIT)
        up = jnp.clip(jnp.dot(x, wu_ref[0], preferred_element_type=F32) + bu_ref[0], -SWIGLU_LIMIT, SWIGLU_LIMIT)
        act = (up + 1.0) * gate * _sigmoid(gate * SWIGLU_ALPHA)
        acc[...] += jnp.dot(act.astype(BF16), wd_ref[0], preferred_element_type=F32)

        @pl.when(j == nj - 1)
        def _():
            obuf[slot] = acc[...] + bd_ref[0]

            def body(r, carry):
                scatter_copy(dest_ref[0, 0, r], r, slot).start()
                return carry
            lax.fori_loop(0, tm, body, 0)

            @pl.when(i >= 1)
            def _():
                wait_scatter(1 - slot)

            @pl.when(i == n_used - 1)
            def _():
                wait_scatter(slot)


def _moe_call(x, top_idx, w_gate_up, b_gate_up, w_down, b_down):
    t, d = x.shape
    n_exp, _, two_h = w_gate_up.shape
    d_exp = two_h // 2
    tm, tn = MOE_TM, MOE_TN
    nj = d_exp // tn
    n_assign = t * TOP_K
    n_blocks = n_assign // tm + n_exp
    n_slots = n_blocks * tm

    flat_e = top_idx.reshape(n_assign)
    onehot = (flat_e[:, None] == jnp.arange(n_exp, dtype=jnp.int32)[None, :]).astype(jnp.int32)
    counts = jnp.sum(onehot, axis=0)
    rank = jnp.take_along_axis(jnp.cumsum(onehot, axis=0), flat_e[:, None], axis=1)[:, 0] - 1
    padded = (counts + tm - 1) // tm * tm
    pad_end = jnp.cumsum(padded)
    slot = (pad_end - padded)[flat_e] + rank
    assign = jnp.arange(n_assign, dtype=jnp.int32)
    slot_token = jnp.zeros((n_slots,), jnp.int32).at[slot].set(assign // TOP_K)
    block_of_slot = jnp.arange(n_slots, dtype=jnp.int32) // tm
    dump = n_assign + (block_of_slot % 2) * tm + jnp.arange(n_slots, dtype=jnp.int32) % tm
    slot_dest = dump.at[slot].set((assign % TOP_K) * t + assign // TOP_K)
    n_used = (pad_end[-1] // tm).astype(jnp.int32)
    blk = jnp.minimum(jnp.arange(n_blocks, dtype=jnp.int32), n_used - 1) * tm
    block_expert = jnp.minimum(jnp.searchsorted(pad_end, blk, side='right'), n_exp - 1).astype(jnp.int32)

    live = lambda i, nu: i < nu[0]
    tok3 = slot_token.reshape(n_blocks, 1, tm)
    smem = lambda imap: pl.BlockSpec((1, 1, tm), imap, memory_space=pltpu.SMEM)
    grid_spec = pltpu.PrefetchScalarGridSpec(
        num_scalar_prefetch=2, grid=(n_blocks, nj),
        in_specs=[
            smem(lambda i, j, nu, be: (i, 0, 0)),
            smem(lambda i, j, nu, be: (jnp.minimum(i + 1, n_blocks - 1), 0, 0)),
            smem(lambda i, j, nu, be: (i, 0, 0)),
            pl.BlockSpec(memory_space=pl.ANY),
            pl.BlockSpec((1, d, tn), lambda i, j, nu, be: (be[i], 0, jnp.where(live(i, nu), j, nj - 1))),
            pl.BlockSpec((1, d, tn), lambda i, j, nu, be: (be[i], 0, nj + jnp.where(live(i, nu), j, nj - 1))),
            pl.BlockSpec((1, tn, d), lambda i, j, nu, be: (be[i], jnp.where(live(i, nu), j, nj - 1), 0)),
            pl.BlockSpec((1, 1, tn), lambda i, j, nu, be: (be[i], 0, jnp.where(live(i, nu), j, nj - 1))),
            pl.BlockSpec((1, 1, tn), lambda i, j, nu, be: (be[i], 0, nj + jnp.where(live(i, nu), j, nj - 1))),
            pl.BlockSpec((1, 1, d), lambda i, j, nu, be: (be[i], 0, 0)),
        ],
        out_specs=pl.BlockSpec(memory_space=pl.ANY),
        scratch_shapes=[pltpu.VMEM((2, tm, d), F32), pltpu.VMEM((tm, d), BF16), pltpu.VMEM((tm, d), F32),
                        pltpu.VMEM((2, tm, d), F32), pltpu.SemaphoreType.DMA((2,)), pltpu.SemaphoreType.DMA((2,))],
    )
    bgu = b_gate_up.astype(F32).reshape(n_exp, 1, two_h)
    return pl.pallas_call(
        functools.partial(_moe_kernel, nj=nj), grid_spec=grid_spec,
        out_shape=jax.ShapeDtypeStruct((n_assign + 2 * tm, d), F32),
        compiler_params=_params("arbitrary", "arbitrary"), name="moe_experts",
    )(n_used.reshape(1), block_expert, tok3, tok3, slot_dest.reshape(n_blocks, 1, tm), x,
      w_gate_up, w_gate_up, w_down, bgu, bgu, b_down.astype(F32).reshape(n_exp, 1, d))


def _moe_out_kernel(*refs):
    y_refs, (gate_ref, h_ref, g_ref, b_ref, o_ref) = refs[:TOP_K], refs[TOP_K:]
    gates = gate_ref[...]
    y = gates[:, 0:1] * y_refs[0][...]
    for k in range(1, TOP_K):
        y = y + gates[:, k:k + 1] * y_refs[k][...]
    o_ref[...] = _layer_norm_rows(DN_ALPHA * h_ref[...] + y, g_ref[...], b_ref[...])


def _moe_out_call(y, gates, h, g, b, tm=256):
    t, d = h.shape
    nb = t // tm
    row = pl.BlockSpec((tm, d), lambda i: (i, 0))
    vec = pl.BlockSpec((1, d), lambda i: (0, 0))
    planes = [pl.BlockSpec((tm, d), functools.partial(lambda k, i: (k * nb + i, 0), k)) for k in range(TOP_K)]
    return pl.pallas_call(
        _moe_out_kernel, grid=(nb,),
        in_specs=planes + [pl.BlockSpec((tm, LANES), lambda i: (i, 0)), row, vec, vec], out_specs=row,
        out_shape=jax.ShapeDtypeStruct((t, d), F32),
        compiler_params=_params("parallel"), name="moe_combine_ln",
    )(*([y] * TOP_K), gates, h, g.reshape(1, d), b.reshape(1, d))


def kernel(x_prompt, x_sample, state_conv, state_ssm, cache_k_w128, cache_v_w128, cache_k_w512, cache_v_w512, cache_k_w2048, cache_v_w2048, cache_mem_k, cache_mem_v, mem_prompt, ln_in_g, ln_in_b, w_in, conv_w, conv_b, dt_bias, a_log, d_skip, ssm_norm_w, w_branch_ssm, w_branch_att, w_mix_out, ln1_g, ln1_b, w_mem_q, w_mem_k, w_mem_v, w_mem_o, ln2_g, ln2_b, w_router, b_router, w_gate_up, b_gate_up, w_down, b_down, ln3_g, ln3_b):
    assert w_in.shape[0] == DEPTH
    bp, lp, d = x_prompt.shape
    bs, ls, _ = x_sample.shape
    n_p, n_s = bp * lp, bs * ls
    t_real = n_p + n_s
    t_pad = -(-t_real // ROW_TILE) * ROW_TILE
    d_inner = ssm_norm_w.shape[1]
    conv_dim = conv_w.shape[2]
    n_heads = d_inner // SSM_HEAD_DIM
    gw = HEADS_PER_GROUP * ATT_HEAD_DIM
    att_w = len(DIL_CONFIGS) * gw
    n_mem = mem_prompt.shape[1]
    lc = SSM_CHUNK
    assert lp % lc == 0 and ls <= lc and n_s % SUBLANES == 0

    def sample_rows(a):
        return a[n_p:n_p + n_s]

    def with_sample_rows(a, rows):
        tail = jnp.concatenate([rows.astype(a.dtype), jnp.zeros((t_pad - t_real, a.shape[1]), a.dtype)], axis=0)
        return lax.dynamic_update_slice(a, tail, (n_p, 0))

    x_all = jnp.concatenate([x_prompt.reshape(n_p, d), x_sample.reshape(n_s, d),
                             jnp.zeros((t_pad - t_real, d), x_prompt.dtype)], axis=0)
    h0, h0_b = _ln_call(x_all, ln_in_g, ln_in_b)

    w_in_b = w_in[0].astype(BF16)
    o_z, o_xbc, o_dt, o_qkv, o_g = 0, d_inner, d_inner + conv_dim, d_inner + conv_dim + n_heads, \
        d_inner + conv_dim + n_heads + 3 * att_w
    z = _mm_call(h0_b, w_in_b[:, o_z:o_xbc], BF16, 512, 1024, "in_z")
    xbc = _mm_call(h0_b, w_in_b[:, o_xbc:o_dt], F32, 512, 1024, "in_xbc")
    w_dt = jnp.pad(w_in_b[:, o_dt:o_qkv], ((0, 0), (0, LANES - n_heads)))
    dt_raw = _mm_call(h0_b, w_dt, F32, 512, LANES, "in_dt")
    qkv = _mm_call(h0_b, w_in_b[:, o_qkv:o_g], F32, 512, att_w, "in_qkv")
    gates = _mm_call(h0_b, w_in_b[:, o_g:], BF16, 512, 1024, "in_gates")

    ssd_w = (conv_w[0], conv_b[0], dt_bias[0], a_log[0], d_skip[0], ssm_norm_w[0])
    y_ssm, ssm_p = _ssd_call(z, xbc, dt_raw, bp, lp // lc, lc, None, *ssd_w, out_rows=t_pad, name="ssd_prompt")

    def pad_seq(a):
        return jnp.pad(sample_rows(a).reshape(bs, ls, a.shape[1]), ((0, 0), (0, lc - ls), (0, 0))).reshape(bs * lc, a.shape[1])

    conv_prev = jnp.pad(state_conv[0], ((0, 0), (SUBLANES - (CONV_W - 1), 0), (0, 0)))
    y_s, ssm_s = _ssd_call(pad_seq(z), pad_seq(xbc), pad_seq(dt_raw), bs, 1, ls,
                           (state_ssm[0].reshape(bs, d_inner, SSM_STATE), conv_prev), *ssd_w,
                           out_rows=bs * lc, name="ssd_sample")
    y_ssm = with_sample_rows(y_ssm, y_s.reshape(bs, lc, d_inner)[:, :ls].reshape(n_s, d_inner))

    pos = jnp.concatenate([jnp.tile(jnp.arange(lp, dtype=jnp.int32), bp),
                           jnp.tile(PAST_LEN + jnp.arange(ls, dtype=jnp.int32), bs),
                           jnp.zeros((t_pad - t_real,), jnp.int32)])
    cos_t, sin_t = _rotary_tables(pos)
    split = _rotary_call(qkv, cos_t, sin_t)
    n_grp = len(DIL_CONFIGS)
    qs, ks, vs = split[:n_grp], split[n_grp:2 * n_grp], split[2 * n_grp:]
    caches = ((cache_k_w128, cache_v_w128), (cache_k_w512, cache_v_w512), (cache_k_w2048, cache_v_w2048))
    outs, lses, kv_p, kv_s = [], [], [], []
    for g, (window, dil) in enumerate(DIL_CONFIGS):
        o_p, l_p = _band_attn_call(qs[g], ks[g], vs[g], bp, lp, window, dil, g)
        seq3 = lambda a: sample_rows(a).reshape(bs, ls, gw)
        o_s, l_s = _sample_attn_call(seq3(qs[g]), seq3(ks[g]), seq3(vs[g]), caches[g][0][0], caches[g][1][0],
                                     window, dil, g)
        outs.append(with_sample_rows(o_p, o_s.reshape(n_s, gw)))
        lses.append(with_sample_rows(l_p, l_s.reshape(n_s, gw)))
        keep = min(window, lp)
        for a in (ks[g], vs[g]):
            kv_p.append(a[:n_p].reshape(bp, lp, HEADS_PER_GROUP, ATT_HEAD_DIM)[:, lp - keep:][None])
            kv_s.append(sample_rows(a).reshape(bs, ls, HEADS_PER_GROUP, ATT_HEAD_DIM)[None])
    att = _combine_call(outs, lses)

    merged = _branch_call(y_ssm, att, gates, w_branch_ssm[0].astype(BF16), w_branch_att[0].astype(BF16))
    h1, h1_b = _mm_res_ln_call(merged, w_mix_out[0].astype(BF16), h0, ln1_g[0], ln1_b[0], "mix_out_ln1")

    mem_b = mem_prompt.reshape(bp * n_mem, d).astype(BF16)
    mem_k_p = _mm_call(mem_b, w_mem_k[0].astype(BF16), F32, 512, 1024, "mem_k")
    mem_v_p = _mm_call(mem_b, w_mem_v[0].astype(BF16), F32, 512, 1024, "mem_v")
    q_mem = _mm_call(h1_b, w_mem_q[0].astype(BF16), BF16, 512, 1024, "mem_q")
    o_mem = _mem_attn_call(q_mem, mem_k_p, mem_v_p, bp, lp, 512, t_pad, "mem_attn_prompt")
    ls8 = -(-ls // SUBLANES) * SUBLANES
    q_s = jnp.pad(sample_rows(q_mem).reshape(bs, ls, d), ((0, 0), (0, ls8 - ls), (0, 0))).reshape(bs * ls8, d)
    o_mem_s = _mem_attn_call(q_s, cache_mem_k[0].reshape(bs * n_mem, d), cache_mem_v[0].reshape(bs * n_mem, d),
                             bs, ls8, ls8, bs * ls8, "mem_attn_sample")
    o_mem = with_sample_rows(o_mem, o_mem_s.reshape(bs, ls8, d)[:, :ls].reshape(n_s, d))
    h2, _ = _mm_res_ln_call(o_mem, w_mem_o[0].astype(BF16), h1, ln2_g[0], ln2_b[0], "mem_o_ln2")

    idx_t, gate_t = _router_call(h2, w_router[0], b_router[0])
    y_moe = _moe_call(h2, idx_t[:, :TOP_K], w_gate_up[0].astype(BF16), b_gate_up[0], w_down[0].astype(BF16), b_down[0])
    h3 = _moe_out_call(y_moe, gate_t, h2, ln3_g[0], ln3_b[0])

    y_prompt = h3[:n_p].reshape(bp, lp, d)
    y_sample = sample_rows(h3).reshape(bs, ls, d)
    xbc_p = xbc[:n_p].reshape(bp, lp, conv_dim)
    conv_p = xbc_p[:, lp - (CONV_W - 1):][None]
    xp_s = jnp.concatenate([state_conv[0].astype(xbc.dtype), sample_rows(xbc).reshape(bs, ls, conv_dim)], axis=1)
    conv_s = xp_s[:, -(CONV_W - 1):][None]
    state_shape = (n_heads, SSM_HEAD_DIM, SSM_STATE)
    ssm_p = ssm_p.reshape(1, bp, *state_shape)
    ssm_s = ssm_s.reshape(1, bs, *state_shape)
    mem_shape = (1, bp, n_mem, MEM_HEADS, d // MEM_HEADS)
    return (y_prompt, y_sample, conv_p, ssm_p, *kv_p, mem_k_p.reshape(mem_shape), mem_v_p.reshape(mem_shape),
            conv_s, ssm_s, *kv_s)
```

```python
import functools
import math

import jax
import jax.numpy as jnp
from jax import lax
from jax.experimental import pallas as pl
from jax.experimental.pallas import tpu as pltpu

F32 = jnp.float32
BF16 = jnp.bfloat16

PAST_LEN = 16384
SSM_HEAD_DIM = 64
SSM_GROUPS = 8
SSM_STATE = 128
CONV_W = 4
SSM_CHUNK = 128
RMS_EPS = 1e-5
DIL_CONFIGS = ((128, 1), (512, 4), (2048, 16))
HEADS_PER_GROUP = 4
ATT_HEAD_DIM = 128
ATT_BLOCK = 128
ROT_DIM = ATT_HEAD_DIM // 4
ROPE_THETA = 500000.0
MEM_HEADS = 4
TOP_K = 4
SWIGLU_LIMIT = 7.0
SWIGLU_ALPHA = 1.702
LN_EPS = 1e-5
DEPTH = 1
DN_ALPHA = (2.0 * DEPTH) ** 0.25

LANES = 128
SUBLANES = 8
ROW_TILE = 512
MOE_TM = 512
MOE_TN = 512
VMEM_LIMIT = 56 * 1024 * 1024
HI = lax.Precision.HIGHEST


def _params(*sem):
    return pltpu.CompilerParams(dimension_semantics=sem, vmem_limit_bytes=VMEM_LIMIT)


def _sigmoid(x):
    return 1.0 / (1.0 + jnp.exp(-x))


def _layer_norm_rows(x, g, b):
    mu = jnp.mean(x, axis=-1, keepdims=True)
    xc = x - mu
    var = jnp.mean(xc * xc, axis=-1, keepdims=True)
    return xc * lax.rsqrt(var + LN_EPS) * g + b


def _ln_kernel(xa_ref, xb_ref, g_ref, b_ref, of_ref, ob_ref, *, n_head):
    def emit(x_ref):
        y = _layer_norm_rows(x_ref[...], g_ref[...], b_ref[...])
        of_ref[...] = y
        ob_ref[...] = y.astype(BF16)

    pl.when(pl.program_id(0) < n_head)(lambda: emit(xa_ref))
    pl.when(pl.program_id(0) >= n_head)(lambda: emit(xb_ref))


def _ln_call(x_head, x_tail, g, b, tm=256):
    d = x_head.shape[1]
    n_head, n_tail = x_head.shape[0] // tm, x_tail.shape[0] // tm
    t = (n_head + n_tail) * tm
    row = pl.BlockSpec((tm, d), lambda i: (i, 0))
    vec = pl.BlockSpec((1, d), lambda i: (0, 0))
    return pl.pallas_call(
        functools.partial(_ln_kernel, n_head=n_head), grid=(n_head + n_tail,),
        in_specs=[pl.BlockSpec((tm, d), lambda i: (jnp.minimum(i, n_head - 1), 0)),
                  pl.BlockSpec((tm, d), lambda i: (jnp.maximum(i - n_head, 0), 0)), vec, vec],
        out_specs=[row, row],
        out_shape=[jax.ShapeDtypeStruct((t, d), F32), jax.ShapeDtypeStruct((t, d), BF16)],
        compiler_params=_params("arbitrary"), name="ln_in",
    )(x_head, x_tail, g.reshape(1, d), b.reshape(1, d))


def _mm_kernel(a_ref, w_ref, o_ref):
    o_ref[...] = jnp.dot(a_ref[...], w_ref[...], preferred_element_type=F32).astype(o_ref.dtype)


def _mm_call(a, w, out_dtype, tm, tn, name):
    m, k = a.shape
    n = w.shape[1]
    tm = min(tm, m)
    return pl.pallas_call(
        _mm_kernel, grid=(n // tn, m // tm),
        in_specs=[pl.BlockSpec((tm, k), lambda j, i: (i, 0)), pl.BlockSpec((k, tn), lambda j, i: (0, j))],
        out_specs=pl.BlockSpec((tm, tn), lambda j, i: (i, j)),
        out_shape=jax.ShapeDtypeStruct((m, n), out_dtype),
        compiler_params=_params("parallel", "parallel"), name=name,
    )(a, w)


def _mm_heads_kernel(a_ref, w_ref, o_ref):
    res = jnp.dot(a_ref[...], w_ref[...], preferred_element_type=F32)
    for h in range(o_ref.shape[0]):
        o_ref[h] = res[:, h * LANES:(h + 1) * LANES].astype(o_ref.dtype)


def _mm_heads_call(a, w, out_dtype, tm, tn, name):
    m, k = a.shape
    n = w.shape[1]
    return pl.pallas_call(
        _mm_heads_kernel, grid=(n // tn, m // tm),
        in_specs=[pl.BlockSpec((tm, k), lambda j, i: (i, 0)), pl.BlockSpec((k, tn), lambda j, i: (0, j))],
        out_specs=pl.BlockSpec((tn // LANES, tm, LANES), lambda j, i: (j, i, 0)),
        out_shape=jax.ShapeDtypeStruct((n // LANES, m, LANES), out_dtype),
        compiler_params=_params("parallel", "parallel"), name=name,
    )(a, w)


def _mm_res_ln_kernel(a_ref, w_ref, h_ref, g_ref, b_ref, of_ref, ob_ref):
    mix = jnp.dot(a_ref[...], w_ref[...], preferred_element_type=F32)
    y = _layer_norm_rows(DN_ALPHA * h_ref[...] + mix, g_ref[...], b_ref[...])
    of_ref[...] = y
    ob_ref[...] = y.astype(BF16)


def _mm_res_ln_call(a, w, h, g, b, name, tm=256):
    m, k = a.shape
    d = w.shape[1]
    row = pl.BlockSpec((tm, d), lambda i: (i, 0))
    vec = pl.BlockSpec((1, d), lambda i: (0, 0))
    return pl.pallas_call(
        _mm_res_ln_kernel, grid=(m // tm,),
        in_specs=[pl.BlockSpec((tm, k), lambda i: (i, 0)), pl.BlockSpec((k, d), lambda i: (0, 0)), row, vec, vec],
        out_specs=[row, row],
        out_shape=[jax.ShapeDtypeStruct((m, d), F32), jax.ShapeDtypeStruct((m, d), BF16)],
        compiler_params=_params("parallel"), name=name,
    )(a, w, h, g.reshape(1, d), b.reshape(1, d))


def _branch_kernel(y_ref, att_ref, gs_ref, ga_ref, ws_ref, wa_ref, o_ref):
    bs = jnp.dot(y_ref[...], ws_ref[...], preferred_element_type=F32)
    ba = jnp.dot(att_ref[...], wa_ref[...], preferred_element_type=F32)
    merged = _sigmoid(gs_ref[...].astype(F32)) * bs + _sigmoid(ga_ref[...].astype(F32)) * ba
    o_ref[...] = merged.astype(o_ref.dtype)


def _branch_call(y_ssm, att, gates, ws, wa, tm=512, tn=1024):
    m, ks = y_ssm.shape
    ka = att.shape[1]
    d = ws.shape[1]
    nj = d // tn
    return pl.pallas_call(
        _branch_kernel, grid=(nj, m // tm),
        in_specs=[pl.BlockSpec((tm, ks), lambda j, i: (i, 0)),
                  pl.BlockSpec((tm, ka), lambda j, i: (i, 0)),
                  pl.BlockSpec((tm, tn), lambda j, i: (i, j)),
                  pl.BlockSpec((tm, tn), lambda j, i: (i, nj + j)),
                  pl.BlockSpec((ks, tn), lambda j, i: (0, j)),
                  pl.BlockSpec((ka, tn), lambda j, i: (0, j))],
        out_specs=pl.BlockSpec((tm, tn), lambda j, i: (i, j)),
        out_shape=jax.ShapeDtypeStruct((m, d), BF16),
        compiler_params=_params("parallel", "parallel"), name="branch_merge",
    )(y_ssm, att, gates, gates, ws, wa)


def _rotary_kernel(qkv_ref, cos_ref, sin_ref, *out_refs, n_head):
    n_grp = len(DIL_CONFIGS)
    kt_refs, vt_refs, qt_refs, qv_refs, kv_refs, vv_refs = [out_refs[a * n_grp:(a + 1) * n_grp] for a in range(6)]
    in_tail = pl.program_id(0) >= n_head
    tm = qkv_ref.shape[1]
    half = ROT_DIM // 2
    gw = HEADS_PER_GROUP * ATT_HEAD_DIM
    n_att = n_grp * HEADS_PER_GROUP

    def rot(x, cos, sin):
        lane = lax.broadcasted_iota(jnp.int32, x.shape, 1)
        partner = jnp.where(lane < half, pltpu.roll(x, ATT_HEAD_DIM - half, 1), pltpu.roll(x, half, 1))
        return x * cos + partner * sin

    for g, (_, dil) in enumerate(DIL_CONFIGS):
        for h in range(HEADS_PER_GROUP):
            hd = g * HEADS_PER_GROUP + h
            tok = slice(h * ATT_HEAD_DIM, (h + 1) * ATT_HEAD_DIM)
            kt_refs[g][:, tok] = rot(qkv_ref[n_att + hd], cos_ref[...], sin_ref[...])
            vt_refs[g][:, tok] = qkv_ref[2 * n_att + hd]
        for r in range(dil):
            rows = pl.ds(r, tm // dil, stride=dil) if dil > 1 else slice(None)
            cos = cos_ref[rows, :]
            sin = sin_ref[rows, :]
            for h in range(HEADS_PER_GROUP):
                hd = g * HEADS_PER_GROUP + h
                view = slice(r * gw + h * ATT_HEAD_DIM, r * gw + (h + 1) * ATT_HEAD_DIM)
                qv_refs[g][:, view] = rot(qkv_ref[hd, rows, :], cos, sin).astype(BF16)
                kv_refs[g][:, view] = rot(qkv_ref[n_att + hd, rows, :], cos, sin).astype(BF16)
                vv_refs[g][:, view] = qkv_ref[2 * n_att + hd, rows, :].astype(BF16)

    @pl.when(in_tail)
    def _():
        for g in range(n_grp):
            for h in range(HEADS_PER_GROUP):
                tok = slice(h * ATT_HEAD_DIM, (h + 1) * ATT_HEAD_DIM)
                qt_refs[g][:, tok] = rot(qkv_ref[g * HEADS_PER_GROUP + h], cos_ref[...], sin_ref[...])


def _rotary_call(qkv, cos_t, sin_t, n_head_rows, tm=256):
    n_slab, t, _ = qkv.shape
    gw = HEADS_PER_GROUP * ATT_HEAD_DIM
    n_grp = len(DIL_CONFIGS)
    n_head = n_head_rows // tm
    row = pl.BlockSpec((tm, gw), lambda i: (i, 0))
    tail = pl.BlockSpec((tm, gw), lambda i: (jnp.maximum(i - n_head, 0), 0))
    tab = pl.BlockSpec((tm, ATT_HEAD_DIM), lambda i: (i, 0))
    views = [pl.BlockSpec((tm // dil, dil * gw), lambda i: (i, 0)) for _, dil in DIL_CONFIGS]
    view_shapes = [jax.ShapeDtypeStruct((t // dil, dil * gw), BF16) for _, dil in DIL_CONFIGS]
    return pl.pallas_call(
        functools.partial(_rotary_kernel, n_head=n_head), grid=(t // tm,),
        in_specs=[pl.BlockSpec((n_slab, tm, ATT_HEAD_DIM), lambda i: (0, i, 0)), tab, tab],
        out_specs=[row] * (2 * n_grp) + [tail] * n_grp + views * 3,
        out_shape=([jax.ShapeDtypeStruct((t, gw), F32)] * (2 * n_grp)
                   + [jax.ShapeDtypeStruct((t - n_head_rows, gw), F32)] * n_grp + view_shapes * 3),
        compiler_params=_params("arbitrary"), name="rotary_split",
    )(qkv, cos_t, sin_t)


def _rotary_tables(pos):
    half = ROT_DIM // 2
    inv_freq = jnp.exp(-math.log(ROPE_THETA) * jnp.arange(half, dtype=F32) * (2.0 / ROT_DIM))
    ang = pos.astype(F32)[:, None] * inv_freq[None, :]
    cos, sin = jnp.cos(ang), jnp.sin(ang)
    rest = ATT_HEAD_DIM - ROT_DIM
    ones = jnp.ones((pos.shape[0], rest), F32)
    cos_t = jnp.concatenate([cos, cos, ones], axis=1)
    sin_t = jnp.concatenate([-sin, sin, jnp.zeros_like(ones)], axis=1)
    return cos_t, sin_t


def _band_attn_kernel(q_ref, kc_ref, kp_ref, vc_ref, vp_ref, o_ref, lse_ref, *, dil, span, has_prev):
    n = pl.program_id(1)
    scale = ATT_HEAD_DIM ** -0.5
    qi = lax.broadcasted_iota(jnp.int32, (ATT_BLOCK, ATT_BLOCK), 0)
    ki = lax.broadcasted_iota(jnp.int32, (ATT_BLOCK, ATT_BLOCK), 1)
    rel_c = qi - ki
    valid_c = (rel_c >= 0) & (rel_c <= span)
    rel_p = rel_c + ATT_BLOCK
    valid_p = (rel_p <= span) & (n > 0)
    nt = (((1,), (1,)), ((), ()))
    for r in range(dil):
        rows = pl.ds(r, ATT_BLOCK, stride=dil) if dil > 1 else slice(None)
        for h in range(HEADS_PER_GROUP):
            cs = slice((r * HEADS_PER_GROUP + h) * ATT_HEAD_DIM, (r * HEADS_PER_GROUP + h + 1) * ATT_HEAD_DIM)
            q = q_ref[:, cs]
            s_c = lax.dot_general(q, kc_ref[:, cs], nt, preferred_element_type=F32) * scale
            s_c = jnp.where(valid_c, s_c, -jnp.inf)
            m = jnp.max(s_c, axis=-1, keepdims=True)
            if has_prev:
                s_p = lax.dot_general(q, kp_ref[:, cs], nt, preferred_element_type=F32) * scale
                s_p = jnp.where(valid_p, s_p, -jnp.inf)
                m = jnp.maximum(m, jnp.max(s_p, axis=-1, keepdims=True))
            p_c = jnp.exp(s_c - m)
            den = jnp.sum(p_c, axis=-1, keepdims=True)
            o = jnp.dot(p_c.astype(BF16), vc_ref[:, cs], preferred_element_type=F32)
            if has_prev:
                p_p = jnp.exp(s_p - m)
                den = den + jnp.sum(p_p, axis=-1, keepdims=True)
                o = o + jnp.dot(p_p.astype(BF16), vp_ref[:, cs], preferred_element_type=F32)
            o_ref[h, rows, :] = o / den
            lse_ref[h, rows, :] = jnp.broadcast_to(m + jnp.log(den), (ATT_BLOCK, ATT_HEAD_DIM))


def _band_attn_call(q, k, v, bsz, seq, window, dil, g):
    gw = HEADS_PER_GROUP * ATT_HEAD_DIM
    t_pad = q.shape[0] * dil
    span = window // dil
    assert span <= ATT_BLOCK and seq % (dil * ATT_BLOCK) == 0
    nb = seq // (dil * ATT_BLOCK)
    cur = pl.BlockSpec((ATT_BLOCK, dil * gw), lambda b, n: (b * nb + n, 0))
    prev = pl.BlockSpec((ATT_BLOCK, dil * gw), lambda b, n: (b * nb + jnp.maximum(n - 1, 0), 0))
    out = pl.BlockSpec((HEADS_PER_GROUP, ATT_BLOCK * dil, ATT_HEAD_DIM), lambda b, n: (0, b * nb + n, 0))
    kern = functools.partial(_band_attn_kernel, dil=dil, span=span, has_prev=nb > 1)
    return pl.pallas_call(
        kern, grid=(bsz, nb),
        in_specs=[cur, cur, prev, cur, prev], out_specs=[out, out],
        out_shape=[jax.ShapeDtypeStruct((HEADS_PER_GROUP, t_pad, ATT_HEAD_DIM), F32)] * 2,
        compiler_params=_params("parallel", "parallel"), name=f"band_attn_g{g}",
    )(q, k, k, v, v)


def _sample_attn_kernel(q_ref, kc_ref, vc_ref, kn_ref, vn_ref, o_ref, lse_ref, *, dil, lq):
    scale = ATT_HEAD_DIM ** -0.5
    n_cache = kc_ref.shape[1]
    row = lax.broadcasted_iota(jnp.int32, (n_cache, HEADS_PER_GROUP, 1), 0)
    for i in range(lq):
        res = i if dil > 1 else 0
        new_rows = range(i + 1) if dil == 1 else (i,)
        q = q_ref[0, i]
        s_c = jnp.sum(kc_ref[0, :, res] * q[None], axis=-1, keepdims=True) * scale
        if dil == 1:
            s_c = jnp.where(row >= i, s_c, -jnp.inf)
        s_new = [jnp.sum(kn_ref[0, j] * q, axis=-1, keepdims=True) * scale for j in new_rows]
        m = jnp.max(s_c, axis=0)
        for s in s_new:
            m = jnp.maximum(m, s)
        p_c = jnp.exp(s_c - m[None])
        den = jnp.sum(p_c, axis=0)
        o = jnp.sum(p_c * vc_ref[0, :, res], axis=0)
        for j, s in zip(new_rows, s_new):
            p = jnp.exp(s - m)
            den = den + p
            o = o + p * vn_ref[0, j]
        o_ref[0, i] = o / den
        lse_ref[0, i] = jnp.broadcast_to(m + jnp.log(den), (HEADS_PER_GROUP, ATT_HEAD_DIM))


def _sample_attn_call(q, k_new, v_new, k_buf, v_buf, window, dil, g):
    b, lq, nh, e = q.shape
    assert k_buf.shape[1] == window and window % dil == 0 and (dil == 1 or dil >= lq) and lq <= window // dil
    n_res = min(dil, lq)
    n_cache = window // dil
    cache = lambda a: a.reshape(b, n_cache, dil, nh, e)
    new = pl.BlockSpec((1, lq, nh, e), lambda i: (i, 0, 0, 0))
    buf = pl.BlockSpec((1, n_cache, n_res, nh, e), lambda i: (i, 0, 0, 0, 0))
    kern = functools.partial(_sample_attn_kernel, dil=dil, lq=lq)
    return pl.pallas_call(
        kern, grid=(b,), in_specs=[new, buf, buf, new, new], out_specs=[new, new],
        out_shape=[jax.ShapeDtypeStruct((b, lq, nh, e), F32)] * 2,
        compiler_params=_params("parallel"), name=f"sample_attn_g{g}",
    )(q, cache(k_buf), cache(v_buf), k_new, v_new)


def _combine_kernel(*refs):
    n_grp = len(DIL_CONFIGS)
    o_refs, l_refs, out_ref = refs[:n_grp], refs[n_grp:2 * n_grp], refs[2 * n_grp]
    for h in range(HEADS_PER_GROUP):
        ls = [r[h] for r in l_refs]
        m = functools.reduce(jnp.maximum, ls)
        ws = [jnp.exp(l - m) for l in ls]
        tot = functools.reduce(lambda a, b: a + b, ws)
        acc = functools.reduce(lambda a, b: a + b, [w * r[h] for w, r in zip(ws, o_refs)])
        out_ref[:, h * ATT_HEAD_DIM:(h + 1) * ATT_HEAD_DIM] = (acc / tot).astype(out_ref.dtype)


def _combine_call(outs, lses, tm=512):
    nh, t, e = outs[0].shape
    row = pl.BlockSpec((nh, tm, e), lambda i: (0, i, 0))
    return pl.pallas_call(
        _combine_kernel, grid=(t // tm,), in_specs=[row] * (2 * len(outs)),
        out_specs=pl.BlockSpec((tm, nh * e), lambda i: (i, 0)),
        out_shape=jax.ShapeDtypeStruct((t, nh * e), BF16),
        compiler_params=_params("parallel"), name="combine_groups",
    )(*outs, *lses)


def _ssd_kernel(*refs, n_chunks, valid_len, has_state):
    if has_state:
        (z_ref, xbc_ref, dt_ref, h0_ref, cprev_ref, cw_ref, cb_ref, dtb_ref, alog_ref, dskip_ref, nw_ref,
         y_ref, hfin_ref, ctail_ref, ht_ref, xext_ref) = refs
    else:
        (z_ref, xbc_ref, dt_ref, cw_ref, cb_ref, dtb_ref, alog_ref, dskip_ref, nw_ref,
         y_ref, hfin_ref, ctail_ref, ht_ref, xext_ref) = refs
    c = pl.program_id(1)
    lc = SSM_CHUNK
    n_st = SSM_STATE
    gw = ht_ref.shape[1] // SSM_GROUPS
    d_inner = ht_ref.shape[1]
    heads_per_group = gw // SSM_HEAD_DIM
    n_tr = d_inner // LANES

    @pl.when(c == 0)
    def _():
        if has_state:
            xext_ref[0:SUBLANES, :] = cprev_ref[0]
            for k in range(n_tr):
                ht_ref[:, k * LANES:(k + 1) * LANES] = h0_ref[0, k * LANES:(k + 1) * LANES, :].T
        else:
            xext_ref[0:SUBLANES, :] = jnp.zeros((SUBLANES, xext_ref.shape[1]), F32)
            ht_ref[...] = jnp.zeros(ht_ref.shape, F32)

    xext_ref[SUBLANES:SUBLANES + lc, :] = xbc_ref[...]

    def conv_silu(c0, width):
        acc = cb_ref[:, c0:c0 + width]
        for s in range(CONV_W):
            acc = acc + cw_ref[CONV_W - 1 - s:CONV_W - s, c0:c0 + width] * xext_ref[pl.ds(SUBLANES - s, lc), c0:c0 + width]
        return acc * _sigmoid(acc)

    dtr = dt_ref[...] + dtb_ref[...]
    dt = jnp.maximum(dtr, 0.0) + jnp.log(1.0 + jnp.exp(-jnp.abs(dtr)))
    if valid_len < lc:
        trow = lax.broadcasted_iota(jnp.int32, dt.shape, 0)
        dt = jnp.where(trow < valid_len, dt, 0.0)
    a = dt * (-jnp.exp(alog_ref[...]))
    ti = lax.broadcasted_iota(jnp.int32, (lc, lc), 0)
    si = lax.broadcasted_iota(jnp.int32, (lc, lc), 1)
    tri = si <= ti
    acum = jnp.dot(tri.astype(F32), a, precision=HI, preferred_element_type=F32)
    acum_t = jnp.dot(a.T, (ti <= si).astype(F32), precision=HI, preferred_element_type=F32)
    lane = lax.broadcasted_iota(jnp.int32, (lc, LANES), 1)
    nt = (((1,), (1,)), ((), ()))

    for g in range(SSM_GROUPS):
        c0 = g * gw
        xs = conv_silu(c0, gw)
        bm = conv_silu(d_inner + g * n_st, n_st)
        cm = conv_silu(d_inner + SSM_GROUPS * n_st + g * n_st, n_st)
        eh = lax.broadcasted_iota(jnp.int32, (LANES, gw), 0)
        ej = lax.broadcasted_iota(jnp.int32, (LANES, gw), 1)
        expand = (eh == g * heads_per_group + (ej >> (SSM_HEAD_DIM.bit_length() - 1))).astype(F32)
        dt_x = jnp.dot(dt, expand, precision=HI, preferred_element_type=F32)
        ac_x = jnp.dot(acum, expand, precision=HI, preferred_element_type=F32)
        xdt = xs * dt_x
        xdt_b = xdt.astype(BF16)
        cm_b = cm.astype(BF16)
        cb = lax.dot_general(cm_b, bm.astype(BF16), nt, preferred_element_type=F32)
        h_old = ht_ref[:, c0:c0 + gw]
        y = jnp.dot(cm_b, h_old.astype(BF16), preferred_element_type=F32) * jnp.exp(ac_x)
        diag = []
        for k in range(gw // LANES):
            x_pair = xdt_b[:, k * LANES:(k + 1) * LANES]
            y_pair = None
            for hh in range(LANES // SSM_HEAD_DIM):
                h = g * heads_per_group + k * (LANES // SSM_HEAD_DIM) + hh
                seg = acum[:, h:h + 1] - acum_t[h:h + 1, :]
                lmat = jnp.exp(jnp.where(tri, seg, -jnp.inf))
                m_b = (cb * lmat).astype(BF16)
                in_head = (lane >= hh * SSM_HEAD_DIM) & (lane < (hh + 1) * SSM_HEAD_DIM)
                part = jnp.dot(m_b, jnp.where(in_head, x_pair, jnp.zeros_like(x_pair)), preferred_element_type=F32)
                y_pair = part if y_pair is None else y_pair + part
            diag.append(y_pair)
        y = y + jnp.concatenate(diag, axis=1) + dskip_ref[:, c0:c0 + gw] * xs
        zg = z_ref[:, c0:c0 + gw].astype(F32)
        y = y * (zg * _sigmoid(zg))
        y = y * lax.rsqrt(jnp.mean(y * y, axis=-1, keepdims=True) + RMS_EPS)
        y_ref[:, c0:c0 + gw] = (y * nw_ref[:, c0:c0 + gw]).astype(y_ref.dtype)
        a_last = ac_x[lc - 1:lc, :]
        xw = (xdt * jnp.exp(a_last - ac_x)).astype(BF16)
        ht_ref[:, c0:c0 + gw] = h_old * jnp.exp(a_last) + jnp.dot(bm.T.astype(BF16), xw, preferred_element_type=F32)

    xext_ref[0:SUBLANES, :] = xext_ref[lc:lc + SUBLANES, :]

    @pl.when(c == n_chunks - 1)
    def _():
        ctail_ref[0] = xext_ref[0:SUBLANES, :]
        for k in range(n_tr):
            hfin_ref[0, k * LANES:(k + 1) * LANES, :] = ht_ref[:, k * LANES:(k + 1) * LANES].T


def _ssd_call(z, xbc, dt_raw, n_seq, n_chunks, valid_len, state, conv_w, conv_b, dt_bias, a_log, d_skip, norm_w,
              out_rows, name):
    d_inner = z.shape[1]
    conv_dim = xbc.shape[1]
    n_heads = d_inner // SSM_HEAD_DIM
    lc = SSM_CHUNK
    pad_h = lambda v: jnp.pad(v.astype(F32), (0, LANES - n_heads)).reshape(1, LANES)
    rows = lambda w: pl.BlockSpec((lc, w), lambda b, c: (b * n_chunks + c, 0))
    const = lambda r, w: pl.BlockSpec((r, w), lambda b, c: (0, 0))
    per_seq = lambda r, w: pl.BlockSpec((1, r, w), lambda b, c: (b, 0, 0))
    args = [z, xbc, dt_raw]
    specs = [rows(d_inner), rows(conv_dim), rows(LANES)]
    if state is not None:
        args += list(state)
        specs += [per_seq(d_inner, SSM_STATE), per_seq(SUBLANES, conv_dim)]
    args += [conv_w, conv_b.reshape(1, conv_dim), pad_h(dt_bias), pad_h(a_log),
             jnp.repeat(d_skip.astype(F32), SSM_HEAD_DIM).reshape(1, d_inner), norm_w.reshape(1, d_inner)]
    specs += [const(CONV_W, conv_dim), const(1, conv_dim), const(1, LANES), const(1, LANES),
              const(1, d_inner), const(1, d_inner)]
    kern = functools.partial(_ssd_kernel, n_chunks=n_chunks, valid_len=valid_len, has_state=state is not None)
    return pl.pallas_call(
        kern, grid=(n_seq, n_chunks), in_specs=specs,
        out_specs=[rows(d_inner), per_seq(d_inner, SSM_STATE), per_seq(SUBLANES, conv_dim)],
        out_shape=[jax.ShapeDtypeStruct((out_rows, d_inner), BF16),
                   jax.ShapeDtypeStruct((n_seq, d_inner, SSM_STATE), F32),
                   jax.ShapeDtypeStruct((n_seq, SUBLANES, conv_dim), F32)],
        scratch_shapes=[pltpu.VMEM((SSM_STATE, d_inner), F32), pltpu.VMEM((lc + 2 * SUBLANES, conv_dim), F32)],
        compiler_params=_params("parallel", "arbitrary"), name=name,
    )(*args)


def _mem_attn_kernel(q_ref, k_ref, v_ref, o_ref):
    scale = q_ref.shape[1] ** -0.5
    s = lax.dot_general(q_ref[...], k_ref[...].astype(BF16), (((1,), (1,)), ((), ())),
                        preferred_element_type=F32) * scale
    m = jnp.max(s, axis=-1, keepdims=True)
    p = jnp.exp(s - m)
    den = jnp.sum(p, axis=-1, keepdims=True)
    o = jnp.dot(p.astype(BF16), v_ref[...].astype(BF16), preferred_element_type=F32)
    o_ref[...] = (o / den).astype(o_ref.dtype)


def _mem_attn_call(q, mem_k, mem_v, n_seq, lq, tq, out_rows, name):
    d = q.shape[1]
    hd = d // MEM_HEADS
    n_mem = mem_k.shape[0] // n_seq
    nq = lq // tq
    qs = pl.BlockSpec((tq, hd), lambda b, h, i: (b * nq + i, h))
    ks = pl.BlockSpec((n_mem, hd), lambda b, h, i: (b, h))
    return pl.pallas_call(
        _mem_attn_kernel, grid=(n_seq, MEM_HEADS, nq), in_specs=[qs, ks, ks], out_specs=qs,
        out_shape=jax.ShapeDtypeStruct((out_rows, d), BF16),
        compiler_params=_params("parallel", "parallel", "parallel"), name=name,
    )(q, mem_k, mem_v)


def _router_kernel(x_ref, w_ref, b_ref, idx_ref, gate_ref):
    logits = jnp.dot(x_ref[...], w_ref[...], precision=HI, preferred_element_type=F32) + b_ref[...]
    lane = lax.broadcasted_iota(jnp.int32, logits.shape, 1)
    idx_out = jnp.zeros(logits.shape, jnp.int32)
    vals = []
    for k in range(TOP_K):
        m = jnp.max(logits, axis=-1, keepdims=True)
        pick = jnp.min(jnp.where(logits == m, lane, LANES), axis=-1, keepdims=True)
        idx_out = jnp.where(lane == k, pick, idx_out)
        logits = jnp.where(lane == pick, -jnp.inf, logits)
        vals.append(m)
    exps = [jnp.exp(v - vals[0]) for v in vals]
    tot = functools.reduce(lambda a, b: a + b, exps)
    gate_out = jnp.zeros(logits.shape, F32)
    for k in range(TOP_K):
        gate_out = jnp.where(lane == k, exps[k] / tot, gate_out)
    idx_ref[...] = idx_out
    gate_ref[...] = gate_out


def _router_call(x, w_router, b_router, tm=256):
    t, d = x.shape
    n_exp = w_router.shape[1]
    w = jnp.pad(w_router.astype(F32), ((0, 0), (0, LANES - n_exp)))
    b = jnp.pad(b_router.astype(F32), (0, LANES - n_exp), constant_values=-jnp.inf).reshape(1, LANES)
    out = pl.BlockSpec((tm, LANES), lambda i: (i, 0))
    return pl.pallas_call(
        _router_kernel, grid=(t // tm,),
        in_specs=[pl.BlockSpec((tm, d), lambda i: (i, 0)), pl.BlockSpec((d, LANES), lambda i: (0, 0)),
                  pl.BlockSpec((1, LANES), lambda i: (0, 0))],
        out_specs=[out, out],
        out_shape=[jax.ShapeDtypeStruct((t, LANES), jnp.int32), jax.ShapeDtypeStruct((t, LANES), F32)],
        compiler_params=_params("parallel"), name="router",
    )(x, w, b)


def _moe_kernel(nused_ref, bexp_ref, tokc_ref, tokn_ref, dprev_ref, dcur_ref, x_hbm, wg_ref, wu_ref, wd_ref,
                bg_ref, bu_ref, bd_ref, y_hbm, xbuf, xb, acc, obuf, gsem, ssem, *, nj):
    i = pl.program_id(0)
    j = pl.program_id(1)
    n_used = nused_ref[0]
    slot = i % 2
    tm = xb.shape[0]
    rows_per_step = tm // nj

    def gather_copy(tok, r, s):
        return pltpu.make_async_copy(x_hbm.at[pl.ds(tok, 1)], xbuf.at[s, pl.ds(r, 1)], gsem.at[s])

    def scatter_copy(dst, r, s):
        return pltpu.make_async_copy(obuf.at[s, pl.ds(r, 1)], y_hbm.at[pl.ds(dst, 1)], ssem.at[s])

    def start_gather(tok_ref, s):
        def body(r, carry):
            gather_copy(tok_ref[0, 0, r], r, s).start()
            return carry
        lax.fori_loop(0, tm, body, 0)

    def wait_gather(s):
        pltpu.make_async_copy(x_hbm.at[pl.ds(0, tm)], xbuf.at[s], gsem.at[s]).wait()

    def wait_scatter(s):
        pltpu.make_async_copy(obuf.at[s], y_hbm.at[pl.ds(0, tm)], ssem.at[s]).wait()

    @pl.when(i < n_used)
    def _():
        @pl.when(j == 0)
        def _():
            @pl.when(i == 0)
            def _():
                start_gather(tokc_ref, 0)
                obuf[1] = jnp.zeros(obuf.shape[1:], F32)

            wait_gather(slot)
            xb[...] = xbuf[slot].astype(BF16)
            acc[...] = jnp.zeros(acc.shape, F32)

        for r in range(rows_per_step):
            row = j * rows_per_step + r
            gather_copy(tokn_ref[0, 0, row], row, 1 - slot).start()
            scatter_copy(dprev_ref[0, 0, row], row, 1 - slot).start()

        x = xb[...]
        gate = jnp.minimum(jnp.dot(x, wg_ref[0], preferred_element_type=F32) + bg_ref[0], SWIGLU_LIMIT)
        up = jnp.clip(jnp.dot(x, wu_ref[0], preferred_element_type=F32) + bu_ref[0], -SWIGLU_LIMIT, SWIGLU_LIMIT)
        act = (up + 1.0) * gate * _sigmoid(gate * SWIGLU_ALPHA)
        acc[...] += jnp.dot(act.astype(BF16), wd_ref[0], preferred_element_type=F32)

        @pl.when(j == nj - 1)
        def _():
            @pl.when(i >= 1)
            def _():
                wait_scatter(slot)

            obuf[slot] = acc[...] + bd_ref[0]

            @pl.when(i == n_used - 1)
            def _():
                def body(r, carry):
                    scatter_copy(dcur_ref[0, 0, r], r, slot).start()
                    return carry
                lax.fori_loop(0, tm, body, 0)
                wait_gather(1 - slot)
                wait_scatter(1 - slot)
                wait_scatter(slot)


def _moe_call(x, top_idx, w_gate_up, b_gate_up, w_down, b_down):
    t, d = x.shape
    n_exp, _, two_h = w_gate_up.shape
    d_exp = two_h // 2
    tm, tn = MOE_TM, MOE_TN
    nj = d_exp // tn
    n_assign = t * TOP_K
    n_blocks = n_assign // tm + n_exp
    n_slots = n_blocks * tm

    flat_e = top_idx.reshape(n_assign)
    onehot = (flat_e[:, None] == jnp.arange(n_exp, dtype=jnp.int32)[None, :]).astype(jnp.int32)
    counts = jnp.sum(onehot, axis=0)
    rank = jnp.take_along_axis(jnp.cumsum(onehot, axis=0), flat_e[:, None], axis=1)[:, 0] - 1
    padded = (counts + tm - 1) // tm * tm
    pad_end = jnp.cumsum(padded)
    slot = (pad_end - padded)[flat_e] + rank
    assign = jnp.arange(n_assign, dtype=jnp.int32)
    slot_assign = jnp.full((n_slots,), -1, jnp.int32).at[slot].set(assign, unique_indices=True,
                                                                   mode='promise_in_bounds')
    slot_token = jnp.maximum(slot_assign, 0) // TOP_K
    slot_id = jnp.arange(n_slots + tm, dtype=jnp.int32) - tm
    dump = n_assign + ((slot_id // tm) % 2) * tm + slot_id % tm
    shifted = jnp.concatenate([jnp.full((tm,), -1, jnp.int32), slot_assign])
    slot_dest = jnp.where(shifted >= 0, (shifted % TOP_K) * t + shifted // TOP_K, dump)
    n_used = (pad_end[-1] // tm).astype(jnp.int32)
    blk = jnp.minimum(jnp.arange(n_blocks, dtype=jnp.int32), n_used - 1) * tm
    block_expert = jnp.minimum(jnp.searchsorted(pad_end, blk, side='right'), n_exp - 1).astype(jnp.int32)

    live = lambda i, nu: i < nu[0]
    tok3 = slot_token.reshape(n_blocks, 1, tm)
    dest3 = slot_dest.reshape(n_blocks + 1, 1, tm)
    smem = lambda imap: pl.BlockSpec((1, 1, tm), imap, memory_space=pltpu.SMEM)
    grid_spec = pltpu.PrefetchScalarGridSpec(
        num_scalar_prefetch=2, grid=(n_blocks, nj),
        in_specs=[
            smem(lambda i, j, nu, be: (i, 0, 0)),
            smem(lambda i, j, nu, be: (jnp.minimum(i + 1, n_blocks - 1), 0, 0)),
            smem(lambda i, j, nu, be: (i, 0, 0)),
            smem(lambda i, j, nu, be: (i + 1, 0, 0)),
            pl.BlockSpec(memory_space=pl.ANY),
            pl.BlockSpec((1, d, tn), lambda i, j, nu, be: (be[i], 0, jnp.where(live(i, nu), j, nj - 1))),
            pl.BlockSpec((1, d, tn), lambda i, j, nu, be: (be[i], 0, nj + jnp.where(live(i, nu), j, nj - 1))),
            pl.BlockSpec((1, tn, d), lambda i, j, nu, be: (be[i], jnp.where(live(i, nu), j, nj - 1), 0)),
            pl.BlockSpec((1, 1, tn), lambda i, j, nu, be: (be[i], 0, jnp.where(live(i, nu), j, nj - 1))),
            pl.BlockSpec((1, 1, tn), lambda i, j, nu, be: (be[i], 0, nj + jnp.where(live(i, nu), j, nj - 1))),
            pl.BlockSpec((1, 1, d), lambda i, j, nu, be: (be[i], 0, 0)),
        ],
        out_specs=pl.BlockSpec(memory_space=pl.ANY),
        scratch_shapes=[pltpu.VMEM((2, tm, d), F32), pltpu.VMEM((tm, d), BF16), pltpu.VMEM((tm, d), F32),
                        pltpu.VMEM((2, tm, d), F32), pltpu.SemaphoreType.DMA((2,)), pltpu.SemaphoreType.DMA((2,))],
    )
    bgu = b_gate_up.astype(F32).reshape(n_exp, 1, two_h)
    return pl.pallas_call(
        functools.partial(_moe_kernel, nj=nj), grid_spec=grid_spec,
        out_shape=jax.ShapeDtypeStruct((n_assign + 2 * tm, d), F32),
        compiler_params=_params("arbitrary", "arbitrary"), name="moe_experts",
    )(n_used.reshape(1), block_expert, tok3, tok3, dest3, dest3, x,
      w_gate_up, w_gate_up, w_down, bgu, bgu, b_down.astype(F32).reshape(n_exp, 1, d))


def _moe_out_kernel(*refs, n_head):
    y_refs, (gate_ref, h_ref, g_ref, b_ref, head_ref, tail_ref) = refs[:TOP_K], refs[TOP_K:]
    gates = gate_ref[...]
    y = gates[:, 0:1] * y_refs[0][...]
    for k in range(1, TOP_K):
        y = y + gates[:, k:k + 1] * y_refs[k][...]
    out = _layer_norm_rows(DN_ALPHA * h_ref[...] + y, g_ref[...], b_ref[...])

    @pl.when(pl.program_id(0) < n_head)
    def _():
        head_ref[...] = out

    @pl.when(pl.program_id(0) >= n_head)
    def _():
        tail_ref[...] = out


def _moe_out_call(y, gates, h, g, b, n_head_rows, tm=256):
    t, d = h.shape
    nb = t // tm
    n_head = n_head_rows // tm
    row = pl.BlockSpec((tm, d), lambda i: (i, 0))
    vec = pl.BlockSpec((1, d), lambda i: (0, 0))
    planes = [pl.BlockSpec((tm, d), functools.partial(lambda k, i: (k * nb + i, 0), k)) for k in range(TOP_K)]
    return pl.pallas_call(
        functools.partial(_moe_out_kernel, n_head=n_head), grid=(nb,),
        in_specs=planes + [pl.BlockSpec((tm, LANES), lambda i: (i, 0)), row, vec, vec],
        out_specs=[pl.BlockSpec((tm, d), lambda i: (jnp.minimum(i, n_head - 1), 0)),
                   pl.BlockSpec((tm, d), lambda i: (jnp.maximum(i - n_head, 0), 0))],
        out_shape=[jax.ShapeDtypeStruct((n_head_rows, d), F32), jax.ShapeDtypeStruct((t - n_head_rows, d), F32)],
        compiler_params=_params("arbitrary"), name="moe_combine_ln",
    )(*([y] * TOP_K), gates, h, g.reshape(1, d), b.reshape(1, d))


def kernel(x_prompt, x_sample, state_conv, state_ssm, cache_k_w128, cache_v_w128, cache_k_w512, cache_v_w512, cache_k_w2048, cache_v_w2048, cache_mem_k, cache_mem_v, mem_prompt, ln_in_g, ln_in_b, w_in, conv_w, conv_b, dt_bias, a_log, d_skip, ssm_norm_w, w_branch_ssm, w_branch_att, w_mix_out, ln1_g, ln1_b, w_mem_q, w_mem_k, w_mem_v, w_mem_o, ln2_g, ln2_b, w_router, b_router, w_gate_up, b_gate_up, w_down, b_down, ln3_g, ln3_b):
    assert w_in.shape[0] == DEPTH
    bp, lp, d = x_prompt.shape
    bs, ls, _ = x_sample.shape
    n_p, n_s = bp * lp, bs * ls
    t_real = n_p + n_s
    t_pad = -(-t_real // ROW_TILE) * ROW_TILE
    d_inner = ssm_norm_w.shape[1]
    conv_dim = conv_w.shape[2]
    n_heads = d_inner // SSM_HEAD_DIM
    gw = HEADS_PER_GROUP * ATT_HEAD_DIM
    att_w = len(DIL_CONFIGS) * gw
    n_mem = mem_prompt.shape[1]
    lc = SSM_CHUNK
    assert lp % lc == 0 and ls <= lc and n_s % SUBLANES == 0

    def sample_rows(a):
        return a[n_p:n_p + n_s]

    def with_sample_rows(a, rows):
        tail = jnp.concatenate([rows.astype(a.dtype), jnp.zeros((t_pad - t_real, a.shape[1]), a.dtype)], axis=0)
        return lax.dynamic_update_slice(a, tail, (n_p, 0))

    def with_sample_heads(a, rows):
        nh, _, e = a.shape
        rows = jnp.transpose(rows.reshape(n_s, nh, e), (1, 0, 2)).astype(a.dtype)
        tail = jnp.concatenate([rows, jnp.zeros((nh, t_pad - t_real, e), a.dtype)], axis=1)
        return lax.dynamic_update_slice(a, tail, (0, n_p, 0))

    x_tail = jnp.concatenate([x_sample.reshape(n_s, d), jnp.zeros((t_pad - t_real, d), x_sample.dtype)], axis=0)
    h0, h0_b = _ln_call(x_prompt.reshape(n_p, d), x_tail, ln_in_g, ln_in_b)

    o_z, o_xbc, o_dt, o_qkv, o_g = 0, d_inner, d_inner + conv_dim, d_inner + conv_dim + n_heads, \
        d_inner + conv_dim + n_heads + 3 * att_w
    w_cols = lambda a, b: w_in[0, :, a:b].astype(BF16)
    z = _mm_call(h0_b, w_cols(o_z, o_xbc), BF16, 512, 1024, "in_z")
    xbc = _mm_call(h0_b, w_cols(o_xbc, o_dt), F32, 512, 1024, "in_xbc")
    w_dt = jnp.pad(w_cols(o_dt, o_qkv), ((0, 0), (0, LANES - n_heads)))
    dt_raw = _mm_call(h0_b, w_dt, F32, 512, LANES, "in_dt")
    qkv = _mm_heads_call(h0_b, w_cols(o_qkv, o_g), F32, 512, att_w, "in_qkv")
    gates = _mm_call(h0_b, w_cols(o_g, w_in.shape[2]), BF16, 512, 1024, "in_gates")

    ssd_w = (conv_w[0], conv_b[0], dt_bias[0], a_log[0], d_skip[0], ssm_norm_w[0])
    y_ssm, ssm_p, conv_tail = _ssd_call(z, xbc, dt_raw, bp, lp // lc, lc, None, *ssd_w, out_rows=t_pad,
                                        name="ssd_prompt")

    def pad_seq(a):
        return jnp.pad(sample_rows(a).reshape(bs, ls, a.shape[1]), ((0, 0), (0, lc - ls), (0, 0))).reshape(bs * lc, a.shape[1])

    conv_prev = jnp.pad(state_conv[0], ((0, 0), (SUBLANES - (CONV_W - 1), 0), (0, 0)))
    y_s, ssm_s, _ = _ssd_call(pad_seq(z), pad_seq(xbc), pad_seq(dt_raw), bs, 1, ls,
                              (state_ssm[0].reshape(bs, d_inner, SSM_STATE), conv_prev), *ssd_w,
                              out_rows=bs * lc, name="ssd_sample")
    y_ssm = with_sample_rows(y_ssm, y_s.reshape(bs, lc, d_inner)[:, :ls].reshape(n_s, d_inner))

    pos = jnp.concatenate([jnp.tile(jnp.arange(lp, dtype=jnp.int32), bp),
                           jnp.tile(PAST_LEN + jnp.arange(ls, dtype=jnp.int32), bs),
                           jnp.zeros((t_pad - t_real,), jnp.int32)])
    cos_t, sin_t = _rotary_tables(pos)
    split = _rotary_call(qkv, cos_t, sin_t, n_p)
    n_grp = len(DIL_CONFIGS)
    k_tok, v_tok, q_tail, q_view, k_view, v_view = [split[a * n_grp:(a + 1) * n_grp] for a in range(6)]
    caches = ((cache_k_w128, cache_v_w128), (cache_k_w512, cache_v_w512), (cache_k_w2048, cache_v_w2048))
    outs, lses, kv_p, kv_s = [], [], [], []
    heads = lambda a: a.reshape(bs, ls, HEADS_PER_GROUP, ATT_HEAD_DIM)
    for g, (window, dil) in enumerate(DIL_CONFIGS):
        o_p, l_p = _band_attn_call(q_view[g], k_view[g], v_view[g], bp, lp, window, dil, g)
        k_new, v_new = heads(sample_rows(k_tok[g])), heads(sample_rows(v_tok[g]))
        o_s, l_s = _sample_attn_call(heads(q_tail[g][:n_s]), k_new, v_new, caches[g][0][0], caches[g][1][0],
                                     window, dil, g)
        outs.append(with_sample_heads(o_p, o_s))
        lses.append(with_sample_heads(l_p, l_s))
        keep = min(window, lp)
        for a in (k_tok[g], v_tok[g]):
            kv_p.append(a[:n_p].reshape(bp, lp, HEADS_PER_GROUP, ATT_HEAD_DIM)[:, lp - keep:][None])
        kv_s += [k_new[None], v_new[None]]
    att = _combine_call(outs, lses)

    merged = _branch_call(y_ssm, att, gates, w_branch_ssm[0].astype(BF16), w_branch_att[0].astype(BF16))
    h1, h1_b = _mm_res_ln_call(merged, w_mix_out[0].astype(BF16), h0, ln1_g[0], ln1_b[0], "mix_out_ln1")

    mem_b = mem_prompt.reshape(bp * n_mem, d).astype(BF16)
    mem_k_p = _mm_call(mem_b, w_mem_k[0].astype(BF16), F32, 512, 1024, "mem_k")
    mem_v_p = _mm_call(mem_b, w_mem_v[0].astype(BF16), F32, 512, 1024, "mem_v")
    q_mem = _mm_call(h1_b, w_mem_q[0].astype(BF16), BF16, 512, 1024, "mem_q")
    o_mem = _mem_attn_call(q_mem, mem_k_p, mem_v_p, bp, lp, 512, t_pad, "mem_attn_prompt")
    ls8 = -(-ls // SUBLANES) * SUBLANES
    q_s = jnp.pad(sample_rows(q_mem).reshape(bs, ls, d), ((0, 0), (0, ls8 - ls), (0, 0))).reshape(bs * ls8, d)
    o_mem_s = _mem_attn_call(q_s, cache_mem_k[0].reshape(bs * n_mem, d), cache_mem_v[0].reshape(bs * n_mem, d),
                             bs, ls8, ls8, bs * ls8, "mem_attn_sample")
    o_mem = with_sample_rows(o_mem, o_mem_s.reshape(bs, ls8, d)[:, :ls].reshape(n_s, d))
    h2, _ = _mm_res_ln_call(o_mem, w_mem_o[0].astype(BF16), h1, ln2_g[0], ln2_b[0], "mem_o_ln2")

    idx_t, gate_t = _router_call(h2, w_router[0], b_router[0])
    y_moe = _moe_call(h2, idx_t[:, :TOP_K], w_gate_up[0].astype(BF16), b_gate_up[0], w_down[0].astype(BF16), b_down[0])
    h3_head, h3_tail = _moe_out_call(y_moe, gate_t, h2, ln3_g[0], ln3_b[0], n_p)

    y_prompt = h3_head.reshape(bp, lp, d)
    y_sample = h3_tail[:n_s].reshape(bs, ls, d)
    conv_p = conv_tail[:, SUBLANES - (CONV_W - 1):][None]
    xp_s = jnp.concatenate([state_conv[0].astype(xbc.dtype), sample_rows(xbc).reshape(bs, ls, conv_dim)], axis=1)
    conv_s = xp_s[:, -(CONV_W - 1):][None]
    state_shape = (n_heads, SSM_HEAD_DIM, SSM_STATE)
    ssm_p = ssm_p.reshape(1, bp, *state_shape)
    ssm_s = ssm_s.reshape(1, bs, *state_shape)
    mem_shape = (1, bp, n_mem, MEM_HEADS, d // MEM_HEADS)
    return (y_prompt, y_sample, conv_p, ssm_p, *kv_p, mem_k_p.reshape(mem_shape), mem_v_p.reshape(mem_shape),
            conv_s, ssm_s, *kv_s)
```

```python
import functools
import math

import jax
import jax.numpy as jnp
from jax import lax
from jax.experimental import pallas as pl
from jax.experimental.pallas import tpu as pltpu

F32 = jnp.float32
BF16 = jnp.bfloat16

PAST_LEN = 16384
SSM_HEAD_DIM = 64
SSM_GROUPS = 8
SSM_STATE = 128
CONV_W = 4
SSM_CHUNK = 128
RMS_EPS = 1e-5
DIL_CONFIGS = ((128, 1), (512, 4), (2048, 16))
HEADS_PER_GROUP = 4
ATT_HEAD_DIM = 128
ATT_BLOCK = 128
ROT_DIM = ATT_HEAD_DIM // 4
ROPE_THETA = 500000.0
MEM_HEADS = 4
TOP_K = 4
SWIGLU_LIMIT = 7.0
SWIGLU_ALPHA = 1.702
LN_EPS = 1e-5
DEPTH = 1
DN_ALPHA = (2.0 * DEPTH) ** 0.25

LANES = 128
SUBLANES = 8
ROW_TILE = 512
MOE_TM = 512
MOE_TN = 512
VMEM_LIMIT = 56 * 1024 * 1024
HI = lax.Precision.HIGHEST


def _params(*sem):
    return pltpu.CompilerParams(dimension_semantics=sem, vmem_limit_bytes=VMEM_LIMIT)


def _sigmoid(x):
    return 1.0 / (1.0 + jnp.exp(-x))


def _layer_norm_rows(x, g, b):
    mu = jnp.mean(x, axis=-1, keepdims=True)
    xc = x - mu
    var = jnp.mean(xc * xc, axis=-1, keepdims=True)
    return xc * lax.rsqrt(var + LN_EPS) * g + b


def _ln_kernel(xa_ref, xb_ref, g_ref, b_ref, of_ref, ob_ref, *, n_head):
    def emit(x_ref):
        y = _layer_norm_rows(x_ref[...], g_ref[...], b_ref[...])
        of_ref[...] = y
        ob_ref[...] = y.astype(BF16)

    pl.when(pl.program_id(0) < n_head)(lambda: emit(xa_ref))
    pl.when(pl.program_id(0) >= n_head)(lambda: emit(xb_ref))


def _ln_call(x_head, x_tail, g, b, tm=256):
    d = x_head.shape[1]
    n_head, n_tail = x_head.shape[0] // tm, x_tail.shape[0] // tm
    t = (n_head + n_tail) * tm
    row = pl.BlockSpec((tm, d), lambda i: (i, 0))
    vec = pl.BlockSpec((1, d), lambda i: (0, 0))
    return pl.pallas_call(
        functools.partial(_ln_kernel, n_head=n_head), grid=(n_head + n_tail,),
        in_specs=[pl.BlockSpec((tm, d), lambda i: (jnp.minimum(i, n_head - 1), 0)),
                  pl.BlockSpec((tm, d), lambda i: (jnp.maximum(i - n_head, 0), 0)), vec, vec],
        out_specs=[row, row],
        out_shape=[jax.ShapeDtypeStruct((t, d), F32), jax.ShapeDtypeStruct((t, d), BF16)],
        compiler_params=_params("arbitrary"), name="ln_in",
    )(x_head, x_tail, g.reshape(1, d), b.reshape(1, d))


def _mm_kernel(a_ref, w_ref, *rest, n_cast):
    o_ref = rest[-2] if n_cast else rest[-1]
    if n_cast:
        src_ref, dst_ref = rest[0], rest[-1]
        step = pl.program_id(0) * pl.num_programs(1) + pl.program_id(1)

        @pl.when(step < n_cast)
        def _():
            dst_ref[...] = src_ref[...].astype(dst_ref.dtype)

    o_ref[...] = jnp.dot(a_ref[...], w_ref[...], preferred_element_type=F32).astype(o_ref.dtype)


def _mm_call(a, w, out_dtype, tm, tn, name, cast=None):
    m, k = a.shape
    n = w.shape[1]
    tm = min(tm, m)
    ni = m // tm
    args = [a, w]
    in_specs = [pl.BlockSpec((tm, k), lambda j, i: (i, 0)), pl.BlockSpec((k, tn), lambda j, i: (0, j))]
    out_specs = pl.BlockSpec((tm, tn), lambda j, i: (i, j))
    out_shape = jax.ShapeDtypeStruct((m, n), out_dtype)
    aliases = {}
    n_cast = 0
    if cast is not None:
        src, dst, chunk0, n_cast, rows = cast
        assert 0 < n_cast <= (n // tn) * ni
        chunk = pl.BlockSpec((rows, src.shape[1]), lambda j, i: (chunk0 + jnp.minimum(j * ni + i, n_cast - 1), 0))
        args.append(src)
        in_specs.append(chunk)
        if dst is not None:
            aliases = {len(args): 1}
            args.append(dst)
            in_specs.append(pl.BlockSpec(memory_space=pl.ANY))
        out_specs = [out_specs, chunk]
        out_shape = [out_shape, jax.ShapeDtypeStruct(src.shape, BF16)]
    return pl.pallas_call(
        functools.partial(_mm_kernel, n_cast=n_cast), grid=(n // tn, ni),
        in_specs=in_specs, out_specs=out_specs, out_shape=out_shape, input_output_aliases=aliases,
        compiler_params=_params("arbitrary", "arbitrary") if n_cast else _params("parallel", "parallel"), name=name,
    )(*args)


def _cast_kernel(src_ref, _, dst_ref):
    dst_ref[...] = src_ref[...].astype(dst_ref.dtype)


def _cast_call(src, dst, chunk0, n_chunks, rows):
    chunk = pl.BlockSpec((rows, src.shape[1]), lambda i: (chunk0 + i, 0))
    return pl.pallas_call(
        _cast_kernel, grid=(n_chunks,), in_specs=[chunk, pl.BlockSpec(memory_space=pl.ANY)], out_specs=chunk,
        out_shape=jax.ShapeDtypeStruct(src.shape, BF16), input_output_aliases={1: 0},
        compiler_params=_params("parallel"), name="cast_rest",
    )(src, dst)


def _mm_heads_kernel(a_ref, w_ref, o_ref):
    res = jnp.dot(a_ref[...], w_ref[...], preferred_element_type=F32)
    for h in range(o_ref.shape[0]):
        o_ref[h] = res[:, h * LANES:(h + 1) * LANES].astype(o_ref.dtype)


def _mm_heads_call(a, w, out_dtype, tm, tn, name):
    m, k = a.shape
    n = w.shape[1]
    return pl.pallas_call(
        _mm_heads_kernel, grid=(n // tn, m // tm),
        in_specs=[pl.BlockSpec((tm, k), lambda j, i: (i, 0)), pl.BlockSpec((k, tn), lambda j, i: (0, j))],
        out_specs=pl.BlockSpec((tn // LANES, tm, LANES), lambda j, i: (j, i, 0)),
        out_shape=jax.ShapeDtypeStruct((n // LANES, m, LANES), out_dtype),
        compiler_params=_params("parallel", "parallel"), name=name,
    )(a, w)


def _mm_res_ln_kernel(a_ref, w_ref, h_ref, g_ref, b_ref, of_ref, ob_ref):
    mix = jnp.dot(a_ref[...], w_ref[...], preferred_element_type=F32)
    y = _layer_norm_rows(DN_ALPHA * h_ref[...] + mix, g_ref[...], b_ref[...])
    of_ref[...] = y
    ob_ref[...] = y.astype(BF16)


def _mm_res_ln_call(a, w, h, g, b, name, tm=256):
    m, k = a.shape
    d = w.shape[1]
    row = pl.BlockSpec((tm, d), lambda i: (i, 0))
    vec = pl.BlockSpec((1, d), lambda i: (0, 0))
    return pl.pallas_call(
        _mm_res_ln_kernel, grid=(m // tm,),
        in_specs=[pl.BlockSpec((tm, k), lambda i: (i, 0)), pl.BlockSpec((k, d), lambda i: (0, 0)), row, vec, vec],
        out_specs=[row, row],
        out_shape=[jax.ShapeDtypeStruct((m, d), F32), jax.ShapeDtypeStruct((m, d), BF16)],
        compiler_params=_params("parallel"), name=name,
    )(a, w, h, g.reshape(1, d), b.reshape(1, d))


def _branch_kernel(y_ref, att_ref, gs_ref, ga_ref, ws_ref, wa_ref, o_ref):
    bs = jnp.dot(y_ref[...], ws_ref[...], preferred_element_type=F32)
    ba = jnp.dot(att_ref[...], wa_ref[...], preferred_element_type=F32)
    merged = _sigmoid(gs_ref[...].astype(F32)) * bs + _sigmoid(ga_ref[...].astype(F32)) * ba
    o_ref[...] = merged.astype(o_ref.dtype)


def _branch_call(y_ssm, att, gates, ws, wa, tm=512, tn=1024):
    m, ks = y_ssm.shape
    ka = att.shape[1]
    d = ws.shape[1]
    nj = d // tn
    return pl.pallas_call(
        _branch_kernel, grid=(nj, m // tm),
        in_specs=[pl.BlockSpec((tm, ks), lambda j, i: (i, 0)),
                  pl.BlockSpec((tm, ka), lambda j, i: (i, 0)),
                  pl.BlockSpec((tm, tn), lambda j, i: (i, j)),
                  pl.BlockSpec((tm, tn), lambda j, i: (i, nj + j)),
                  pl.BlockSpec((ks, tn), lambda j, i: (0, j)),
                  pl.BlockSpec((ka, tn), lambda j, i: (0, j))],
        out_specs=pl.BlockSpec((tm, tn), lambda j, i: (i, j)),
        out_shape=jax.ShapeDtypeStruct((m, d), BF16),
        compiler_params=_params("parallel", "parallel"), name="branch_merge",
    )(y_ssm, att, gates, gates, ws, wa)


def _rotary_kernel(qkv_ref, cos_ref, sin_ref, *out_refs, n_head):
    n_grp = len(DIL_CONFIGS)
    kt_refs, vt_refs, qt_refs, qv_refs, kv_refs, vv_refs = [out_refs[a * n_grp:(a + 1) * n_grp] for a in range(6)]
    in_tail = pl.program_id(0) >= n_head
    tm = qkv_ref.shape[1]
    half = ROT_DIM // 2
    gw = HEADS_PER_GROUP * ATT_HEAD_DIM
    n_att = n_grp * HEADS_PER_GROUP

    def rot(x, cos, sin):
        lane = lax.broadcasted_iota(jnp.int32, x.shape, 1)
        partner = jnp.where(lane < half, pltpu.roll(x, ATT_HEAD_DIM - half, 1), pltpu.roll(x, half, 1))
        return x * cos + partner * sin

    for g, (_, dil) in enumerate(DIL_CONFIGS):
        for h in range(HEADS_PER_GROUP):
            hd = g * HEADS_PER_GROUP + h
            tok = slice(h * ATT_HEAD_DIM, (h + 1) * ATT_HEAD_DIM)
            kt_refs[g][:, tok] = rot(qkv_ref[n_att + hd], cos_ref[...], sin_ref[...])
            vt_refs[g][:, tok] = qkv_ref[2 * n_att + hd]
        for r in range(dil):
            rows = pl.ds(r, tm // dil, stride=dil) if dil > 1 else slice(None)
            cos = cos_ref[rows, :]
            sin = sin_ref[rows, :]
            for h in range(HEADS_PER_GROUP):
                hd = g * HEADS_PER_GROUP + h
                view = slice(r * gw + h * ATT_HEAD_DIM, r * gw + (h + 1) * ATT_HEAD_DIM)
                qv_refs[g][:, view] = rot(qkv_ref[hd, rows, :], cos, sin).astype(BF16)
                kv_refs[g][:, view] = rot(qkv_ref[n_att + hd, rows, :], cos, sin).astype(BF16)
                vv_refs[g][:, view] = qkv_ref[2 * n_att + hd, rows, :].astype(BF16)

    @pl.when(in_tail)
    def _():
        for g in range(n_grp):
            for h in range(HEADS_PER_GROUP):
                tok = slice(h * ATT_HEAD_DIM, (h + 1) * ATT_HEAD_DIM)
                qt_refs[g][:, tok] = rot(qkv_ref[g * HEADS_PER_GROUP + h], cos_ref[...], sin_ref[...])


def _rotary_call(qkv, cos_t, sin_t, n_head_rows, tm=256):
    n_slab, t, _ = qkv.shape
    gw = HEADS_PER_GROUP * ATT_HEAD_DIM
    n_grp = len(DIL_CONFIGS)
    n_head = n_head_rows // tm
    row = pl.BlockSpec((tm, gw), lambda i: (i, 0))
    tail = pl.BlockSpec((tm, gw), lambda i: (jnp.maximum(i - n_head, 0), 0))
    tab = pl.BlockSpec((tm, ATT_HEAD_DIM), lambda i: (i, 0))
    views = [pl.BlockSpec((tm // dil, dil * gw), lambda i: (i, 0)) for _, dil in DIL_CONFIGS]
    view_shapes = [jax.ShapeDtypeStruct((t // dil, dil * gw), BF16) for _, dil in DIL_CONFIGS]
    return pl.pallas_call(
        functools.partial(_rotary_kernel, n_head=n_head), grid=(t // tm,),
        in_specs=[pl.BlockSpec((n_slab, tm, ATT_HEAD_DIM), lambda i: (0, i, 0)), tab, tab],
        out_specs=[row] * (2 * n_grp) + [tail] * n_grp + views * 3,
        out_shape=([jax.ShapeDtypeStruct((t, gw), F32)] * (2 * n_grp)
                   + [jax.ShapeDtypeStruct((t - n_head_rows, gw), F32)] * n_grp + view_shapes * 3),
        compiler_params=_params("arbitrary"), name="rotary_split",
    )(qkv, cos_t, sin_t)


def _rotary_tables(pos):
    half = ROT_DIM // 2
    inv_freq = jnp.exp(-math.log(ROPE_THETA) * jnp.arange(half, dtype=F32) * (2.0 / ROT_DIM))
    ang = pos.astype(F32)[:, None] * inv_freq[None, :]
    cos, sin = jnp.cos(ang), jnp.sin(ang)
    rest = ATT_HEAD_DIM - ROT_DIM
    ones = jnp.ones((pos.shape[0], rest), F32)
    cos_t = jnp.concatenate([cos, cos, ones], axis=1)
    sin_t = jnp.concatenate([-sin, sin, jnp.zeros_like(ones)], axis=1)
    return cos_t, sin_t


def _band_attn_kernel(q_ref, kc_ref, kp_ref, vc_ref, vp_ref, o_ref, lse_ref, *, dil, span, has_prev):
    n = pl.program_id(1)
    scale = ATT_HEAD_DIM ** -0.5
    qi = lax.broadcasted_iota(jnp.int32, (ATT_BLOCK, ATT_BLOCK), 0)
    ki = lax.broadcasted_iota(jnp.int32, (ATT_BLOCK, ATT_BLOCK), 1)
    rel_c = qi - ki
    valid_c = (rel_c >= 0) & (rel_c <= span)
    rel_p = rel_c + ATT_BLOCK
    valid_p = (rel_p <= span) & (n > 0)
    nt = (((1,), (1,)), ((), ()))
    batch = 2 * HEADS_PER_GROUP
    problems = [(r, h) for r in range(dil) for h in range(HEADS_PER_GROUP)]
    for b0 in range(0, len(problems), batch):
        todo = problems[b0:b0 + batch]
        cols = [slice((r * HEADS_PER_GROUP + h) * ATT_HEAD_DIM, (r * HEADS_PER_GROUP + h + 1) * ATT_HEAD_DIM)
                for r, h in todo]
        qs = [q_ref[:, cs] for cs in cols]
        s_c = [lax.dot_general(q, kc_ref[:, cs], nt, preferred_element_type=F32) for q, cs in zip(qs, cols)]
        s_c = [jnp.where(valid_c, s * scale, -jnp.inf) for s in s_c]
        m = [jnp.max(s, axis=-1, keepdims=True) for s in s_c]
        if has_prev:
            s_p = [lax.dot_general(q, kp_ref[:, cs], nt, preferred_element_type=F32) for q, cs in zip(qs, cols)]
            s_p = [jnp.where(valid_p, s * scale, -jnp.inf) for s in s_p]
            m = [jnp.maximum(mm, jnp.max(s, axis=-1, keepdims=True)) for mm, s in zip(m, s_p)]
        p_c = [jnp.exp(s - mm) for s, mm in zip(s_c, m)]
        den = [jnp.sum(p, axis=-1, keepdims=True) for p in p_c]
        o = [jnp.dot(p.astype(BF16), vc_ref[:, cs], preferred_element_type=F32) for p, cs in zip(p_c, cols)]
        if has_prev:
            p_p = [jnp.exp(s - mm) for s, mm in zip(s_p, m)]
            den = [d + jnp.sum(p, axis=-1, keepdims=True) for d, p in zip(den, p_p)]
            o = [oo + jnp.dot(p.astype(BF16), vp_ref[:, cs], preferred_element_type=F32)
                 for oo, p, cs in zip(o, p_p, cols)]
        for (r, h), oo, d, mm in zip(todo, o, den, m):
            rows = pl.ds(r, ATT_BLOCK, stride=dil) if dil > 1 else slice(None)
            o_ref[h, rows, :] = oo / d
            lse_ref[h, rows, :] = jnp.broadcast_to(mm + jnp.log(d), (ATT_BLOCK, ATT_HEAD_DIM))


def _band_attn_call(q, k, v, bsz, seq, window, dil, g):
    gw = HEADS_PER_GROUP * ATT_HEAD_DIM
    t_pad = q.shape[0] * dil
    span = window // dil
    assert span <= ATT_BLOCK and seq % (dil * ATT_BLOCK) == 0
    nb = seq // (dil * ATT_BLOCK)
    cur = pl.BlockSpec((ATT_BLOCK, dil * gw), lambda b, n: (b * nb + n, 0))
    prev = pl.BlockSpec((ATT_BLOCK, dil * gw), lambda b, n: (b * nb + jnp.maximum(n - 1, 0), 0))
    out = pl.BlockSpec((HEADS_PER_GROUP, ATT_BLOCK * dil, ATT_HEAD_DIM), lambda b, n: (0, b * nb + n, 0))
    kern = functools.partial(_band_attn_kernel, dil=dil, span=span, has_prev=nb > 1)
    return pl.pallas_call(
        kern, grid=(bsz, nb),
        in_specs=[cur, cur, prev, cur, prev], out_specs=[out, out],
        out_shape=[jax.ShapeDtypeStruct((HEADS_PER_GROUP, t_pad, ATT_HEAD_DIM), F32)] * 2,
        compiler_params=_params("parallel", "parallel"), name=f"band_attn_g{g}",
    )(q, k, k, v, v)


def _sample_attn_kernel(q_ref, kc_ref, vc_ref, kn_ref, vn_ref, o_ref, lse_ref, *, dil, lq):
    scale = ATT_HEAD_DIM ** -0.5
    n_cache = kc_ref.shape[1]
    row = lax.broadcasted_iota(jnp.int32, (n_cache, HEADS_PER_GROUP, 1), 0)
    for i in range(lq):
        res = i if dil > 1 else 0
        new_rows = range(i + 1) if dil == 1 else (i,)
        q = q_ref[0, i]
        s_c = jnp.sum(kc_ref[0, :, res] * q[None], axis=-1, keepdims=True) * scale
        if dil == 1:
            s_c = jnp.where(row >= i, s_c, -jnp.inf)
        s_new = [jnp.sum(kn_ref[0, j] * q, axis=-1, keepdims=True) * scale for j in new_rows]
        m = jnp.max(s_c, axis=0)
        for s in s_new:
            m = jnp.maximum(m, s)
        p_c = jnp.exp(s_c - m[None])
        den = jnp.sum(p_c, axis=0)
        o = jnp.sum(p_c * vc_ref[0, :, res], axis=0)
        for j, s in zip(new_rows, s_new):
            p = jnp.exp(s - m)
            den = den + p
            o = o + p * vn_ref[0, j]
        o_ref[0, i] = o / den
        lse_ref[0, i] = jnp.broadcast_to(m + jnp.log(den), (HEADS_PER_GROUP, ATT_HEAD_DIM))


def _decode_mem_attn_kernel(q_ref, k_ref, v_ref, o_ref):
    scale = q_ref.shape[-1] ** -0.5
    for i in range(q_ref.shape[1]):
        q = q_ref[0, i]
        s = jnp.sum(k_ref[0] * q[None], axis=-1, keepdims=True) * scale
        m = jnp.max(s, axis=0)
        p = jnp.exp(s - m[None])
        den = jnp.sum(p, axis=0)
        o_ref[0, i] = (jnp.sum(p * v_ref[0], axis=0) / den).astype(o_ref.dtype)


def _decode_mem_attn_call(q, mem_k, mem_v):
    b, lq, nh, e = q.shape
    qs = pl.BlockSpec((1, lq, nh, e), lambda i: (i, 0, 0, 0))
    ms = pl.BlockSpec((1, mem_k.shape[1], nh, e), lambda i: (i, 0, 0, 0))
    return pl.pallas_call(
        _decode_mem_attn_kernel, grid=(b,), in_specs=[qs, ms, ms], out_specs=qs,
        out_shape=jax.ShapeDtypeStruct((b, lq, nh, e), F32),
        compiler_params=_params("parallel"), name="mem_attn_sample",
    )(q, mem_k, mem_v)


def _sample_attn_call(q, k_new, v_new, k_buf, v_buf, window, dil, g):
    b, lq, nh, e = q.shape
    assert k_buf.shape[1] == window and window % dil == 0 and (dil == 1 or dil >= lq) and lq <= window // dil
    n_res = min(dil, lq)
    n_cache = window // dil
    cache = lambda a: a.reshape(b, n_cache, dil, nh, e)
    new = pl.BlockSpec((1, lq, nh, e), lambda i: (i, 0, 0, 0))
    buf = pl.BlockSpec((1, n_cache, n_res, nh, e), lambda i: (i, 0, 0, 0, 0))
    kern = functools.partial(_sample_attn_kernel, dil=dil, lq=lq)
    return pl.pallas_call(
        kern, grid=(b,), in_specs=[new, buf, buf, new, new], out_specs=[new, new],
        out_shape=[jax.ShapeDtypeStruct((b, lq, nh, e), F32)] * 2,
        compiler_params=_params("parallel"), name=f"sample_attn_g{g}",
    )(q, cache(k_buf), cache(v_buf), k_new, v_new)


def _combine_kernel(*refs):
    n_grp = len(DIL_CONFIGS)
    o_refs, l_refs, out_ref = refs[:n_grp], refs[n_grp:2 * n_grp], refs[2 * n_grp]
    for h in range(HEADS_PER_GROUP):
        ls = [r[h] for r in l_refs]
        m = functools.reduce(jnp.maximum, ls)
        ws = [jnp.exp(l - m) for l in ls]
        tot = functools.reduce(lambda a, b: a + b, ws)
        acc = functools.reduce(lambda a, b: a + b, [w * r[h] for w, r in zip(ws, o_refs)])
        out_ref[:, h * ATT_HEAD_DIM:(h + 1) * ATT_HEAD_DIM] = (acc / tot).astype(out_ref.dtype)


def _combine_call(outs, lses, tm=512):
    nh, t, e = outs[0].shape
    row = pl.BlockSpec((nh, tm, e), lambda i: (0, i, 0))
    return pl.pallas_call(
        _combine_kernel, grid=(t // tm,), in_specs=[row] * (2 * len(outs)),
        out_specs=pl.BlockSpec((tm, nh * e), lambda i: (i, 0)),
        out_shape=jax.ShapeDtypeStruct((t, nh * e), BF16),
        compiler_params=_params("parallel"), name="combine_groups",
    )(*outs, *lses)


def _ssd_kernel(*refs, n_chunks, valid_len, has_state):
    if has_state:
        (z_ref, xbc_ref, dt_ref, h0_ref, cprev_ref, cw_ref, cb_ref, dtb_ref, alog_ref, dskip_ref, nw_ref,
         y_ref, hfin_ref, ctail_ref, ht_ref, xext_ref) = refs
    else:
        (z_ref, xbc_ref, dt_ref, cw_ref, cb_ref, dtb_ref, alog_ref, dskip_ref, nw_ref,
         y_ref, hfin_ref, ctail_ref, ht_ref, xext_ref) = refs
    c = pl.program_id(1)
    lc = SSM_CHUNK
    n_st = SSM_STATE
    gw = ht_ref.shape[1] // SSM_GROUPS
    d_inner = ht_ref.shape[1]
    heads_per_group = gw // SSM_HEAD_DIM
    n_tr = d_inner // LANES

    @pl.when(c == 0)
    def _():
        if has_state:
            xext_ref[0:SUBLANES, :] = cprev_ref[0]
            for k in range(n_tr):
                ht_ref[:, k * LANES:(k + 1) * LANES] = h0_ref[0, k * LANES:(k + 1) * LANES, :].T
        else:
            xext_ref[0:SUBLANES, :] = jnp.zeros((SUBLANES, xext_ref.shape[1]), F32)
            ht_ref[...] = jnp.zeros(ht_ref.shape, F32)

    xext_ref[SUBLANES:SUBLANES + lc, :] = xbc_ref[...]

    def conv_silu(c0, width):
        acc = cb_ref[:, c0:c0 + width]
        for s in range(CONV_W):
            acc = acc + cw_ref[CONV_W - 1 - s:CONV_W - s, c0:c0 + width] * xext_ref[pl.ds(SUBLANES - s, lc), c0:c0 + width]
        return acc * _sigmoid(acc)

    dtr = dt_ref[...] + dtb_ref[...]
    dt = jnp.maximum(dtr, 0.0) + jnp.log(1.0 + jnp.exp(-jnp.abs(dtr)))
    if valid_len < lc:
        trow = lax.broadcasted_iota(jnp.int32, dt.shape, 0)
        dt = jnp.where(trow < valid_len, dt, 0.0)
    a = dt * (-jnp.exp(alog_ref[...]))
    ti = lax.broadcasted_iota(jnp.int32, (lc, lc), 0)
    si = lax.broadcasted_iota(jnp.int32, (lc, lc), 1)
    tri = si <= ti
    acum = jnp.dot(tri.astype(F32), a, precision=HI, preferred_element_type=F32)
    acum_t = jnp.dot(a.T, (ti <= si).astype(F32), precision=HI, preferred_element_type=F32)
    lane = lax.broadcasted_iota(jnp.int32, (lc, LANES), 1)
    nt = (((1,), (1,)), ((), ()))

    for g in range(SSM_GROUPS):
        c0 = g * gw
        xs = conv_silu(c0, gw)
        bm = conv_silu(d_inner + g * n_st, n_st)
        cm = conv_silu(d_inner + SSM_GROUPS * n_st + g * n_st, n_st)
        eh = lax.broadcasted_iota(jnp.int32, (LANES, gw), 0)
        ej = lax.broadcasted_iota(jnp.int32, (LANES, gw), 1)
        expand = (eh == g * heads_per_group + (ej >> (SSM_HEAD_DIM.bit_length() - 1))).astype(F32)
        dt_x = jnp.dot(dt, expand, precision=HI, preferred_element_type=F32)
        ac_x = jnp.dot(acum, expand, precision=HI, preferred_element_type=F32)
        xdt = xs * dt_x
        xdt_b = xdt.astype(BF16)
        cm_b = cm.astype(BF16)
        cb = lax.dot_general(cm_b, bm.astype(BF16), nt, preferred_element_type=F32)
        h_old = ht_ref[:, c0:c0 + gw]
        y = jnp.dot(cm_b, h_old.astype(BF16), preferred_element_type=F32) * jnp.exp(ac_x)
        diag = []
        for k in range(gw // LANES):
            x_pair = xdt_b[:, k * LANES:(k + 1) * LANES]
            y_pair = None
            for hh in range(LANES // SSM_HEAD_DIM):
                h = g * heads_per_group + k * (LANES // SSM_HEAD_DIM) + hh
                seg = acum[:, h:h + 1] - acum_t[h:h + 1, :]
                lmat = jnp.exp(jnp.where(tri, seg, -jnp.inf))
                m_b = (cb * lmat).astype(BF16)
                in_head = (lane >= hh * SSM_HEAD_DIM) & (lane < (hh + 1) * SSM_HEAD_DIM)
                part = jnp.dot(m_b, jnp.where(in_head, x_pair, jnp.zeros_like(x_pair)), preferred_element_type=F32)
                y_pair = part if y_pair is None else y_pair + part
            diag.append(y_pair)
        y = y + jnp.concatenate(diag, axis=1) + dskip_ref[:, c0:c0 + gw] * xs
        zg = z_ref[:, c0:c0 + gw].astype(F32)
        y = y * (zg * _sigmoid(zg))
        y = y * lax.rsqrt(jnp.mean(y * y, axis=-1, keepdims=True) + RMS_EPS)
        y_ref[:, c0:c0 + gw] = (y * nw_ref[:, c0:c0 + gw]).astype(y_ref.dtype)
        a_last = ac_x[lc - 1:lc, :]
        xw = (xdt * jnp.exp(a_last - ac_x)).astype(BF16)
        ht_ref[:, c0:c0 + gw] = h_old * jnp.exp(a_last) + jnp.dot(bm.T.astype(BF16), xw, preferred_element_type=F32)

    xext_ref[0:SUBLANES, :] = xext_ref[lc:lc + SUBLANES, :]

    @pl.when(c == n_chunks - 1)
    def _():
        ctail_ref[0] = xext_ref[0:SUBLANES, :]
        for k in range(n_tr):
            hfin_ref[0, k * LANES:(k + 1) * LANES, :] = ht_ref[:, k * LANES:(k + 1) * LANES].T


def _ssd_call(z, xbc, dt_raw, n_seq, n_chunks, valid_len, state, conv_w, conv_b, dt_bias, a_log, d_skip, norm_w,
              out_rows, name):
    d_inner = z.shape[1]
    conv_dim = xbc.shape[1]
    n_heads = d_inner // SSM_HEAD_DIM
    lc = SSM_CHUNK
    pad_h = lambda v: jnp.pad(v.astype(F32), (0, LANES - n_heads)).reshape(1, LANES)
    rows = lambda w: pl.BlockSpec((lc, w), lambda b, c: (b * n_chunks + c, 0))
    const = lambda r, w: pl.BlockSpec((r, w), lambda b, c: (0, 0))
    per_seq = lambda r, w: pl.BlockSpec((1, r, w), lambda b, c: (b, 0, 0))
    args = [z, xbc, dt_raw]
    specs = [rows(d_inner), rows(conv_dim), rows(LANES)]
    if state is not None:
        args += list(state)
        specs += [per_seq(d_inner, SSM_STATE), per_seq(SUBLANES, conv_dim)]
    args += [conv_w, conv_b.reshape(1, conv_dim), pad_h(dt_bias), pad_h(a_log),
             jnp.repeat(d_skip.astype(F32), SSM_HEAD_DIM).reshape(1, d_inner), norm_w.reshape(1, d_inner)]
    specs += [const(CONV_W, conv_dim), const(1, conv_dim), const(1, LANES), const(1, LANES),
              const(1, d_inner), const(1, d_inner)]
    kern = functools.partial(_ssd_kernel, n_chunks=n_chunks, valid_len=valid_len, has_state=state is not None)
    return pl.pallas_call(
        kern, grid=(n_seq, n_chunks), in_specs=specs,
        out_specs=[rows(d_inner), per_seq(d_inner, SSM_STATE), per_seq(SUBLANES, conv_dim)],
        out_shape=[jax.ShapeDtypeStruct((out_rows, d_inner), BF16),
                   jax.ShapeDtypeStruct((n_seq, d_inner, SSM_STATE), F32),
                   jax.ShapeDtypeStruct((n_seq, SUBLANES, conv_dim), F32)],
        scratch_shapes=[pltpu.VMEM((SSM_STATE, d_inner), F32), pltpu.VMEM((lc + 2 * SUBLANES, conv_dim), F32)],
        compiler_params=_params("parallel", "arbitrary"), name=name,
    )(*args)


def _mem_attn_kernel(q_ref, k_ref, v_ref, o_ref):
    scale = q_ref.shape[1] ** -0.5
    s = lax.dot_general(q_ref[...], k_ref[...].astype(BF16), (((1,), (1,)), ((), ())),
                        preferred_element_type=F32) * scale
    m = jnp.max(s, axis=-1, keepdims=True)
    p = jnp.exp(s - m)
    den = jnp.sum(p, axis=-1, keepdims=True)
    o = jnp.dot(p.astype(BF16), v_ref[...].astype(BF16), preferred_element_type=F32)
    o_ref[...] = (o / den).astype(o_ref.dtype)


def _mem_attn_call(q, mem_k, mem_v, n_seq, lq, tq, out_rows, name):
    d = q.shape[1]
    hd = d // MEM_HEADS
    n_mem = mem_k.shape[0] // n_seq
    nq = lq // tq
    qs = pl.BlockSpec((tq, hd), lambda b, h, i: (b * nq + i, h))
    ks = pl.BlockSpec((n_mem, hd), lambda b, h, i: (b, h))
    return pl.pallas_call(
        _mem_attn_kernel, grid=(n_seq, MEM_HEADS, nq), in_specs=[qs, ks, ks], out_specs=qs,
        out_shape=jax.ShapeDtypeStruct((out_rows, d), BF16),
        compiler_params=_params("parallel", "parallel", "parallel"), name=name,
    )(q, mem_k, mem_v)


def _router_kernel(x_ref, w_ref, b_ref, idx_ref, gate_ref):
    logits = jnp.dot(x_ref[...], w_ref[...], precision=HI, preferred_element_type=F32) + b_ref[...]
    lane = lax.broadcasted_iota(jnp.int32, logits.shape, 1)
    idx_out = jnp.zeros(logits.shape, jnp.int32)
    vals = []
    for k in range(TOP_K):
        m = jnp.max(logits, axis=-1, keepdims=True)
        pick = jnp.min(jnp.where(logits == m, lane, LANES), axis=-1, keepdims=True)
        idx_out = jnp.where(lane == k, pick, idx_out)
        logits = jnp.where(lane == pick, -jnp.inf, logits)
        vals.append(m)
    exps = [jnp.exp(v - vals[0]) for v in vals]
    tot = functools.reduce(lambda a, b: a + b, exps)
    gate_out = jnp.zeros(logits.shape, F32)
    for k in range(TOP_K):
        gate_out = jnp.where(lane == k, exps[k] / tot, gate_out)
    idx_ref[...] = idx_out
    gate_ref[...] = gate_out


def _router_call(x, w_router, b_router, tm=256):
    t, d = x.shape
    n_exp = w_router.shape[1]
    w = jnp.pad(w_router.astype(F32), ((0, 0), (0, LANES - n_exp)))
    b = jnp.pad(b_router.astype(F32), (0, LANES - n_exp), constant_values=-jnp.inf).reshape(1, LANES)
    out = pl.BlockSpec((tm, LANES), lambda i: (i, 0))
    return pl.pallas_call(
        _router_kernel, grid=(t // tm,),
        in_specs=[pl.BlockSpec((tm, d), lambda i: (i, 0)), pl.BlockSpec((d, LANES), lambda i: (0, 0)),
                  pl.BlockSpec((1, LANES), lambda i: (0, 0))],
        out_specs=[out, out],
        out_shape=[jax.ShapeDtypeStruct((t, LANES), jnp.int32), jax.ShapeDtypeStruct((t, LANES), F32)],
        compiler_params=_params("parallel"), name="router",
    )(x, w, b)


def _moe_kernel(nused_ref, bexp_ref, tokc_ref, tokn_ref, dprev_ref, dcur_ref, x_hbm, wg_ref, wu_ref, wd_ref,
                bg_ref, bu_ref, bd_ref, y_hbm, xbuf, xb, acc, obuf, gsem, ssem, *, nj):
    i = pl.program_id(0)
    j = pl.program_id(1)
    n_used = nused_ref[0]
    slot = i % 2
    tm = xb.shape[0]
    rows_per_step = tm // nj

    def gather_copy(tok, r, s):
        return pltpu.make_async_copy(x_hbm.at[pl.ds(tok, 1)], xbuf.at[s, pl.ds(r, 1)], gsem.at[s])

    def scatter_copy(dst, r, s):
        return pltpu.make_async_copy(obuf.at[s, pl.ds(r, 1)], y_hbm.at[pl.ds(dst, 1)], ssem.at[s])

    def start_gather(tok_ref, s):
        def body(r, carry):
            gather_copy(tok_ref[0, 0, r], r, s).start()
            return carry
        lax.fori_loop(0, tm, body, 0)

    def wait_gather(s):
        pltpu.make_async_copy(x_hbm.at[pl.ds(0, tm)], xbuf.at[s], gsem.at[s]).wait()

    def wait_scatter(s):
        pltpu.make_async_copy(obuf.at[s], y_hbm.at[pl.ds(0, tm)], ssem.at[s]).wait()

    @pl.when(i < n_used)
    def _():
        @pl.when(j == 0)
        def _():
            @pl.when(i == 0)
            def _():
                start_gather(tokc_ref, 0)
                obuf[1] = jnp.zeros(obuf.shape[1:], F32)

            wait_gather(slot)
            xb[...] = xbuf[slot].astype(BF16)
            acc[...] = jnp.zeros(acc.shape, F32)

        for r in range(rows_per_step):
            row = j * rows_per_step + r
            gather_copy(tokn_ref[0, 0, row], row, 1 - slot).start()
            scatter_copy(dprev_ref[0, 0, row], row, 1 - slot).start()

        x = xb[...]
        gate = jnp.minimum(jnp.dot(x, wg_ref[0], preferred_element_type=F32) + bg_ref[0], SWIGLU_LIMIT)
        up = jnp.clip(jnp.dot(x, wu_ref[0], preferred_element_type=F32) + bu_ref[0], -SWIGLU_LIMIT, SWIGLU_LIMIT)
        act = (up + 1.0) * gate * _sigmoid(gate * SWIGLU_ALPHA)
        acc[...] += jnp.dot(act.astype(BF16), wd_ref[0], preferred_element_type=F32)

        @pl.when(j == nj - 1)
        def _():
            @pl.when(i >= 1)
            def _():
                wait_scatter(slot)

            obuf[slot] = acc[...] + bd_ref[0]

            @pl.when(i == n_used - 1)
            def _():
                def body(r, carry):
                    scatter_copy(dcur_ref[0, 0, r], r, slot).start()
                    return carry
                lax.fori_loop(0, tm, body, 0)
                wait_gather(1 - slot)
                wait_scatter(1 - slot)
                wait_scatter(slot)


def _moe_call(x, top_idx, w_gate_up, b_gate_up, w_down, b_down):
    t, d = x.shape
    n_exp, _, two_h = w_gate_up.shape
    d_exp = two_h // 2
    tm, tn = MOE_TM, MOE_TN
    nj = d_exp // tn
    n_assign = t * TOP_K
    n_blocks = n_assign // tm + n_exp
    n_slots = n_blocks * tm

    flat_e = top_idx.reshape(n_assign)
    onehot = (flat_e[:, None] == jnp.arange(n_exp, dtype=jnp.int32)[None, :]).astype(jnp.int32)
    counts = jnp.sum(onehot, axis=0)
    rank = jnp.take_along_axis(jnp.cumsum(onehot, axis=0), flat_e[:, None], axis=1)[:, 0] - 1
    padded = (counts + tm - 1) // tm * tm
    pad_end = jnp.cumsum(padded)
    slot = (pad_end - padded)[flat_e] + rank
    assign = jnp.arange(n_assign, dtype=jnp.int32)
    slot_assign = jnp.full((n_slots,), -1, jnp.int32).at[slot].set(assign, unique_indices=True,
                                                                   mode='promise_in_bounds')
    slot_token = jnp.maximum(slot_assign, 0) // TOP_K
    slot_id = jnp.arange(n_slots + tm, dtype=jnp.int32) - tm
    dump = n_assign + ((slot_id // tm) % 2) * tm + slot_id % tm
    shifted = jnp.concatenate([jnp.full((tm,), -1, jnp.int32), slot_assign])
    slot_dest = jnp.where(shifted >= 0, (shifted % TOP_K) * t + shifted // TOP_K, dump)
    n_used = (pad_end[-1] // tm).astype(jnp.int32)
    blk = jnp.minimum(jnp.arange(n_blocks, dtype=jnp.int32), n_used - 1) * tm
    block_expert = jnp.minimum(jnp.sum((pad_end[None, :] <= blk[:, None]).astype(jnp.int32), axis=1), n_exp - 1)

    live = lambda i, nu: i < nu[0]
    tok3 = slot_token.reshape(n_blocks, 1, tm)
    dest3 = slot_dest.reshape(n_blocks + 1, 1, tm)
    smem = lambda imap: pl.BlockSpec((1, 1, tm), imap, memory_space=pltpu.SMEM)
    grid_spec = pltpu.PrefetchScalarGridSpec(
        num_scalar_prefetch=2, grid=(n_blocks, nj),
        in_specs=[
            smem(lambda i, j, nu, be: (i, 0, 0)),
            smem(lambda i, j, nu, be: (jnp.minimum(i + 1, n_blocks - 1), 0, 0)),
            smem(lambda i, j, nu, be: (i, 0, 0)),
            smem(lambda i, j, nu, be: (i + 1, 0, 0)),
            pl.BlockSpec(memory_space=pl.ANY),
            pl.BlockSpec((1, d, tn), lambda i, j, nu, be: (be[i], 0, jnp.where(live(i, nu), j, nj - 1))),
            pl.BlockSpec((1, d, tn), lambda i, j, nu, be: (be[i], 0, nj + jnp.where(live(i, nu), j, nj - 1))),
            pl.BlockSpec((1, tn, d), lambda i, j, nu, be: (be[i], jnp.where(live(i, nu), j, nj - 1), 0)),
            pl.BlockSpec((1, 1, tn), lambda i, j, nu, be: (be[i], 0, jnp.where(live(i, nu), j, nj - 1))),
            pl.BlockSpec((1, 1, tn), lambda i, j, nu, be: (be[i], 0, nj + jnp.where(live(i, nu), j, nj - 1))),
            pl.BlockSpec((1, 1, d), lambda i, j, nu, be: (be[i], 0, 0)),
        ],
        out_specs=pl.BlockSpec(memory_space=pl.ANY),
        scratch_shapes=[pltpu.VMEM((2, tm, d), F32), pltpu.VMEM((tm, d), BF16), pltpu.VMEM((tm, d), F32),
                        pltpu.VMEM((2, tm, d), F32), pltpu.SemaphoreType.DMA((2,)), pltpu.SemaphoreType.DMA((2,))],
    )
    bgu = b_gate_up.astype(F32).reshape(n_exp, 1, two_h)
    return pl.pallas_call(
        functools.partial(_moe_kernel, nj=nj), grid_spec=grid_spec,
        out_shape=jax.ShapeDtypeStruct((n_assign + 2 * tm, d), F32),
        compiler_params=_params("arbitrary", "arbitrary"), name="moe_experts",
    )(n_used.reshape(1), block_expert, tok3, tok3, dest3, dest3, x,
      w_gate_up, w_gate_up, w_down, bgu, bgu, b_down.astype(F32).reshape(n_exp, 1, d))


def _moe_out_kernel(*refs, n_head):
    y_refs, (gate_ref, h_ref, g_ref, b_ref, head_ref, tail_ref) = refs[:TOP_K], refs[TOP_K:]
    gates = gate_ref[...]
    y = gates[:, 0:1] * y_refs[0][...]
    for k in range(1, TOP_K):
        y = y + gates[:, k:k + 1] * y_refs[k][...]
    out = _layer_norm_rows(DN_ALPHA * h_ref[...] + y, g_ref[...], b_ref[...])

    @pl.when(pl.program_id(0) < n_head)
    def _():
        head_ref[...] = out

    @pl.when(pl.program_id(0) >= n_head)
    def _():
        tail_ref[...] = out


def _moe_out_call(y, gates, h, g, b, n_head_rows, tm=256):
    t, d = h.shape
    nb = t // tm
    n_head = n_head_rows // tm
    row = pl.BlockSpec((tm, d), lambda i: (i, 0))
    vec = pl.BlockSpec((1, d), lambda i: (0, 0))
    planes = [pl.BlockSpec((tm, d), functools.partial(lambda k, i: (k * nb + i, 0), k)) for k in range(TOP_K)]
    return pl.pallas_call(
        functools.partial(_moe_out_kernel, n_head=n_head), grid=(nb,),
        in_specs=planes + [pl.BlockSpec((tm, LANES), lambda i: (i, 0)), row, vec, vec],
        out_specs=[pl.BlockSpec((tm, d), lambda i: (jnp.minimum(i, n_head - 1), 0)),
                   pl.BlockSpec((tm, d), lambda i: (jnp.maximum(i - n_head, 0), 0))],
        out_shape=[jax.ShapeDtypeStruct((n_head_rows, d), F32), jax.ShapeDtypeStruct((t - n_head_rows, d), F32)],
        compiler_params=_params("arbitrary"), name="moe_combine_ln",
    )(*([y] * TOP_K), gates, h, g.reshape(1, d), b.reshape(1, d))


def kernel(x_prompt, x_sample, state_conv, state_ssm, cache_k_w128, cache_v_w128, cache_k_w512, cache_v_w512, cache_k_w2048, cache_v_w2048, cache_mem_k, cache_mem_v, mem_prompt, ln_in_g, ln_in_b, w_in, conv_w, conv_b, dt_bias, a_log, d_skip, ssm_norm_w, w_branch_ssm, w_branch_att, w_mix_out, ln1_g, ln1_b, w_mem_q, w_mem_k, w_mem_v, w_mem_o, ln2_g, ln2_b, w_router, b_router, w_gate_up, b_gate_up, w_down, b_down, ln3_g, ln3_b):
    assert w_in.shape[0] == DEPTH
    bp, lp, d = x_prompt.shape
    bs, ls, _ = x_sample.shape
    n_p, n_s = bp * lp, bs * ls
    t_real = n_p + n_s
    t_pad = -(-t_real // ROW_TILE) * ROW_TILE
    d_inner = ssm_norm_w.shape[1]
    conv_dim = conv_w.shape[2]
    n_heads = d_inner // SSM_HEAD_DIM
    gw = HEADS_PER_GROUP * ATT_HEAD_DIM
    att_w = len(DIL_CONFIGS) * gw
    n_mem = mem_prompt.shape[1]
    lc = SSM_CHUNK
    assert lp % lc == 0 and ls <= lc and n_s % SUBLANES == 0

    def sample_rows(a):
        return a[n_p:n_p + n_s]

    def with_sample_rows(a, rows):
        tail = jnp.concatenate([rows.astype(a.dtype), jnp.zeros((t_pad - t_real, a.shape[1]), a.dtype)], axis=0)
        return lax.dynamic_update_slice(a, tail, (n_p, 0))

    def with_sample_heads(a, rows):
        nh, _, e = a.shape
        rows = jnp.transpose(rows.reshape(n_s, nh, e), (1, 0, 2)).astype(a.dtype)
        tail = jnp.concatenate([rows, jnp.zeros((nh, t_pad - t_real, e), a.dtype)], axis=1)
        return lax.dynamic_update_slice(a, tail, (0, n_p, 0))

    x_tail = jnp.concatenate([x_sample.reshape(n_s, d), jnp.zeros((t_pad - t_real, d), x_sample.dtype)], axis=0)
    h0, h0_b = _ln_call(x_prompt.reshape(n_p, d), x_tail, ln_in_g, ln_in_b)

    o_z, o_xbc, o_dt, o_qkv, o_g = 0, d_inner, d_inner + conv_dim, d_inner + conv_dim + n_heads, \
        d_inner + conv_dim + n_heads + 3 * att_w
    w_cols = lambda a, b: w_in[0, :, a:b].astype(BF16)
    n_exp, _, two_h = w_gate_up.shape[1:]
    wgu_src = w_gate_up[0].reshape(n_exp * d, two_h)
    wd_src = w_down[0].reshape(n_exp * (two_h // 2), d)
    steps = lambda n_cols, tn: (n_cols // tn) * (t_pad // 512)
    gu_rows, wd_rows = 128, 256
    gu_chunks, wd_chunks = wgu_src.shape[0] // gu_rows, wd_src.shape[0] // wd_rows
    n_a = min(steps(conv_dim, 1024), gu_chunks)
    n_b = min(steps(d_inner, 1024), gu_chunks - n_a)
    n_c = min(steps(2 * d, 1024), wd_chunks)
    xbc, wgu_b = _mm_call(h0_b, w_cols(o_xbc, o_dt), F32, 512, 1024, "in_xbc", (wgu_src, None, 0, n_a, gu_rows))
    if n_b:
        z, wgu_b = _mm_call(h0_b, w_cols(o_z, o_xbc), BF16, 512, 1024, "in_z", (wgu_src, wgu_b, n_a, n_b, gu_rows))
    else:
        z = _mm_call(h0_b, w_cols(o_z, o_xbc), BF16, 512, 1024, "in_z")
    if gu_chunks - n_a - n_b:
        wgu_b = _cast_call(wgu_src, wgu_b, n_a + n_b, gu_chunks - n_a - n_b, gu_rows)
    w_dt = jnp.pad(w_cols(o_dt, o_qkv), ((0, 0), (0, LANES - n_heads)))
    dt_raw = _mm_call(h0_b, w_dt, F32, 512, LANES, "in_dt")
    qkv = _mm_heads_call(h0_b, w_cols(o_qkv, o_g), F32, 512, att_w, "in_qkv")
    gates, wd_b = _mm_call(h0_b, w_cols(o_g, w_in.shape[2]), BF16, 512, 1024, "in_gates",
                           (wd_src, None, 0, n_c, wd_rows))
    if wd_chunks - n_c:
        wd_b = _cast_call(wd_src, wd_b, n_c, wd_chunks - n_c, wd_rows)
    wgu_b = wgu_b.reshape(n_exp, d, two_h)
    wd_b = wd_b.reshape(n_exp, two_h // 2, d)

    ssd_w = (conv_w[0], conv_b[0], dt_bias[0], a_log[0], d_skip[0], ssm_norm_w[0])
    y_ssm, ssm_p, conv_tail = _ssd_call(z, xbc, dt_raw, bp, lp // lc, lc, None, *ssd_w, out_rows=t_pad,
                                        name="ssd_prompt")

    def pad_seq(a):
        return jnp.pad(sample_rows(a).reshape(bs, ls, a.shape[1]), ((0, 0), (0, lc - ls), (0, 0))).reshape(bs * lc, a.shape[1])

    conv_prev = jnp.pad(state_conv[0], ((0, 0), (SUBLANES - (CONV_W - 1), 0), (0, 0)))
    y_s, ssm_s, _ = _ssd_call(pad_seq(z), pad_seq(xbc), pad_seq(dt_raw), bs, 1, ls,
                              (state_ssm[0].reshape(bs, d_inner, SSM_STATE), conv_prev), *ssd_w,
                              out_rows=bs * lc, name="ssd_sample")
    y_ssm = with_sample_rows(y_ssm, y_s.reshape(bs, lc, d_inner)[:, :ls].reshape(n_s, d_inner))

    pos = jnp.concatenate([jnp.tile(jnp.arange(lp, dtype=jnp.int32), bp),
                           jnp.tile(PAST_LEN + jnp.arange(ls, dtype=jnp.int32), bs),
                           jnp.zeros((t_pad - t_real,), jnp.int32)])
    cos_t, sin_t = _rotary_tables(pos)
    split = _rotary_call(qkv, cos_t, sin_t, n_p)
    n_grp = len(DIL_CONFIGS)
    k_tok, v_tok, q_tail, q_view, k_view, v_view = [split[a * n_grp:(a + 1) * n_grp] for a in range(6)]
    caches = ((cache_k_w128, cache_v_w128), (cache_k_w512, cache_v_w512), (cache_k_w2048, cache_v_w2048))
    outs, lses, kv_p, kv_s = [], [], [], []
    heads = lambda a: a.reshape(bs, ls, HEADS_PER_GROUP, ATT_HEAD_DIM)
    for g, (window, dil) in enumerate(DIL_CONFIGS):
        o_p, l_p = _band_attn_call(q_view[g], k_view[g], v_view[g], bp, lp, window, dil, g)
        k_new, v_new = heads(sample_rows(k_tok[g])), heads(sample_rows(v_tok[g]))
        o_s, l_s = _sample_attn_call(heads(q_tail[g][:n_s]), k_new, v_new, caches[g][0][0], caches[g][1][0],
                                     window, dil, g)
        outs.append(with_sample_heads(o_p, o_s))
        lses.append(with_sample_heads(l_p, l_s))
        keep = min(window, lp)
        for a in (k_tok[g], v_tok[g]):
            kv_p.append(a[:n_p].reshape(bp, lp, HEADS_PER_GROUP, ATT_HEAD_DIM)[:, lp - keep:][None])
        kv_s += [k_new[None], v_new[None]]
    att = _combine_call(outs, lses)

    merged = _branch_call(y_ssm, att, gates, w_branch_ssm[0].astype(BF16), w_branch_att[0].astype(BF16))
    h1, h1_b = _mm_res_ln_call(merged, w_mix_out[0].astype(BF16), h0, ln1_g[0], ln1_b[0], "mix_out_ln1")

    mem_b = mem_prompt.reshape(bp * n_mem, d).astype(BF16)
    mem_k_p = _mm_call(mem_b, w_mem_k[0].astype(BF16), F32, 512, 1024, "mem_k")
    mem_v_p = _mm_call(mem_b, w_mem_v[0].astype(BF16), F32, 512, 1024, "mem_v")
    q_mem = _mm_call(h1_b, w_mem_q[0].astype(BF16), BF16, 512, 1024, "mem_q")
    o_mem = _mem_attn_call(q_mem, mem_k_p, mem_v_p, bp, lp, 512, t_pad, "mem_attn_prompt")
    q_s = sample_rows(q_mem).reshape(bs, ls, MEM_HEADS, d // MEM_HEADS).astype(F32)
    o_mem_s = _decode_mem_attn_call(q_s, cache_mem_k[0], cache_mem_v[0])
    o_mem = with_sample_rows(o_mem, o_mem_s.reshape(n_s, d))
    h2, _ = _mm_res_ln_call(o_mem, w_mem_o[0].astype(BF16), h1, ln2_g[0], ln2_b[0], "mem_o_ln2")

    idx_t, gate_t = _router_call(h2, w_router[0], b_router[0])
    y_moe = _moe_call(h2, idx_t[:, :TOP_K], wgu_b, b_gate_up[0], wd_b, b_down[0])
    h3_head, h3_tail = _moe_out_call(y_moe, gate_t, h2, ln3_g[0], ln3_b[0], n_p)

    y_prompt = h3_head.reshape(bp, lp, d)
    y_sample = h3_tail[:n_s].reshape(bs, ls, d)
    conv_p = conv_tail[:, SUBLANES - (CONV_W - 1):][None]
    xp_s = jnp.concatenate([state_conv[0].astype(xbc.dtype), sample_rows(xbc).reshape(bs, ls, conv_dim)], axis=1)
    conv_s = xp_s[:, -(CONV_W - 1):][None]
    state_shape = (n_heads, SSM_HEAD_DIM, SSM_STATE)
    ssm_p = ssm_p.reshape(1, bp, *state_shape)
    ssm_s = ssm_s.reshape(1, bs, *state_shape)
    mem_shape = (1, bp, n_mem, MEM_HEADS, d // MEM_HEADS)
    return (y_prompt, y_sample, conv_p, ssm_p, *kv_p, mem_k_p.reshape(mem_shape), mem_v_p.reshape(mem_shape),
            conv_s, ssm_s, *kv_s)
```

```python
import functools
import math

import jax
import jax.numpy as jnp
from jax import lax
from jax.experimental import pallas as pl
from jax.experimental.pallas import tpu as pltpu

F32 = jnp.float32
BF16 = jnp.bfloat16

PAST_LEN = 16384
SSM_HEAD_DIM = 64
SSM_GROUPS = 8
SSM_STATE = 128
CONV_W = 4
SSM_CHUNK = 128
RMS_EPS = 1e-5
DIL_CONFIGS = ((128, 1), (512, 4), (2048, 16))
HEADS_PER_GROUP = 4
ATT_HEAD_DIM = 128
ATT_BLOCK = 128
ROT_DIM = ATT_HEAD_DIM // 4
ROPE_THETA = 500000.0
MEM_HEADS = 4
TOP_K = 4
SWIGLU_LIMIT = 7.0
SWIGLU_ALPHA = 1.702
LN_EPS = 1e-5
DEPTH = 1
DN_ALPHA = (2.0 * DEPTH) ** 0.25

LANES = 128
SUBLANES = 8
ROW_TILE = 512
MOE_TM = 512
MOE_TN = 512
VMEM_LIMIT = 56 * 1024 * 1024
HI = lax.Precision.HIGHEST


def _params(*sem):
    return pltpu.CompilerParams(dimension_semantics=sem, vmem_limit_bytes=VMEM_LIMIT)


def _sigmoid(x):
    return 0.5 * jnp.tanh(0.5 * x) + 0.5


def _split3(x):
    hi = x.astype(BF16)
    r1 = x - hi.astype(F32)
    mid = r1.astype(BF16)
    lo = (r1 - mid.astype(F32)).astype(BF16)
    return hi, mid, lo


def _layer_norm_rows(x, g, b):
    mu = jnp.mean(x, axis=-1, keepdims=True)
    xc = x - mu
    var = jnp.mean(xc * xc, axis=-1, keepdims=True)
    return xc * lax.rsqrt(var + LN_EPS) * g + b


def _ln_kernel(xa_ref, xb_ref, g_ref, b_ref, of_ref, ob_ref, *, n_head):
    def emit(x_ref):
        y = _layer_norm_rows(x_ref[...], g_ref[...], b_ref[...])
        of_ref[...] = y
        ob_ref[...] = y.astype(BF16)

    pl.when(pl.program_id(0) < n_head)(lambda: emit(xa_ref))
    pl.when(pl.program_id(0) >= n_head)(lambda: emit(xb_ref))


def _ln_call(x_head, x_tail, g, b, tm=256):
    d = x_head.shape[1]
    n_head, n_tail = x_head.shape[0] // tm, x_tail.shape[0] // tm
    t = (n_head + n_tail) * tm
    row = pl.BlockSpec((tm, d), lambda i: (i, 0))
    vec = pl.BlockSpec((1, d), lambda i: (0, 0))
    return pl.pallas_call(
        functools.partial(_ln_kernel, n_head=n_head), grid=(n_head + n_tail,),
        in_specs=[pl.BlockSpec((tm, d), lambda i: (jnp.minimum(i, n_head - 1), 0)),
                  pl.BlockSpec((tm, d), lambda i: (jnp.maximum(i - n_head, 0), 0)), vec, vec],
        out_specs=[row, row],
        out_shape=[jax.ShapeDtypeStruct((t, d), F32), jax.ShapeDtypeStruct((t, d), BF16)],
        compiler_params=_params("arbitrary"), name="ln_in",
    )(x_head, x_tail, g.reshape(1, d), b.reshape(1, d))


def _mm_kernel(a_ref, w_ref, *rest, n_cast):
    o_ref = rest[-2] if n_cast else rest[-1]
    if n_cast:
        src_ref, dst_ref = rest[0], rest[-1]
        step = pl.program_id(0) * pl.num_programs(1) + pl.program_id(1)

        @pl.when(step < n_cast)
        def _():
            dst_ref[...] = src_ref[...].astype(dst_ref.dtype)

    o_ref[...] = jnp.dot(a_ref[...], w_ref[...], preferred_element_type=F32).astype(o_ref.dtype)


def _mm_call(a, w, out_dtype, tm, tn, name, cast=None):
    m, k = a.shape
    n = w.shape[1]
    tm = min(tm, m)
    ni = m // tm
    args = [a, w]
    in_specs = [pl.BlockSpec((tm, k), lambda j, i: (i, 0)), pl.BlockSpec((k, tn), lambda j, i: (0, j))]
    out_specs = pl.BlockSpec((tm, tn), lambda j, i: (i, j))
    out_shape = jax.ShapeDtypeStruct((m, n), out_dtype)
    aliases = {}
    n_cast = 0
    if cast is not None:
        src, dst, chunk0, n_cast, rows = cast
        assert 0 < n_cast <= (n // tn) * ni
        chunk = pl.BlockSpec((rows, src.shape[1]), lambda j, i: (chunk0 + jnp.minimum(j * ni + i, n_cast - 1), 0))
        args.append(src)
        in_specs.append(chunk)
        if dst is not None:
            aliases = {len(args): 1}
            args.append(dst)
            in_specs.append(pl.BlockSpec(memory_space=pl.ANY))
        out_specs = [out_specs, chunk]
        out_shape = [out_shape, jax.ShapeDtypeStruct(src.shape, BF16)]
    return pl.pallas_call(
        functools.partial(_mm_kernel, n_cast=n_cast), grid=(n // tn, ni),
        in_specs=in_specs, out_specs=out_specs, out_shape=out_shape, input_output_aliases=aliases,
        compiler_params=_params("arbitrary", "arbitrary") if n_cast else _params("parallel", "parallel"), name=name,
    )(*args)


def _cast_kernel(src_ref, _, dst_ref):
    dst_ref[...] = src_ref[...].astype(dst_ref.dtype)


def _cast_call(src, dst, chunk0, n_chunks, rows):
    chunk = pl.BlockSpec((rows, src.shape[1]), lambda i: (chunk0 + i, 0))
    return pl.pallas_call(
        _cast_kernel, grid=(n_chunks,), in_specs=[chunk, pl.BlockSpec(memory_space=pl.ANY)], out_specs=chunk,
        out_shape=jax.ShapeDtypeStruct(src.shape, BF16), input_output_aliases={1: 0},
        compiler_params=_params("parallel"), name="cast_rest",
    )(src, dst)


def _mm_heads_kernel(a_ref, w_ref, o_ref):
    res = jnp.dot(a_ref[...], w_ref[...], preferred_element_type=F32)
    for h in range(o_ref.shape[0]):
        o_ref[h] = res[:, h * LANES:(h + 1) * LANES].astype(o_ref.dtype)


def _mm_heads_call(a, w, out_dtype, tm, tn, name):
    m, k = a.shape
    n = w.shape[1]
    return pl.pallas_call(
        _mm_heads_kernel, grid=(n // tn, m // tm),
        in_specs=[pl.BlockSpec((tm, k), lambda j, i: (i, 0)), pl.BlockSpec((k, tn), lambda j, i: (0, j))],
        out_specs=pl.BlockSpec((tn // LANES, tm, LANES), lambda j, i: (j, i, 0)),
        out_shape=jax.ShapeDtypeStruct((n // LANES, m, LANES), out_dtype),
        compiler_params=_params("parallel", "parallel"), name=name,
    )(a, w)


def _mm_res_ln_kernel(a_ref, w_ref, h_ref, g_ref, b_ref, of_ref, ob_ref):
    mix = jnp.dot(a_ref[...], w_ref[...], preferred_element_type=F32)
    y = _layer_norm_rows(DN_ALPHA * h_ref[...] + mix, g_ref[...], b_ref[...])
    of_ref[...] = y
    ob_ref[...] = y.astype(BF16)


def _mm_res_ln_call(a, w, h, g, b, name, tm=256):
    m, k = a.shape
    d = w.shape[1]
    row = pl.BlockSpec((tm, d), lambda i: (i, 0))
    vec = pl.BlockSpec((1, d), lambda i: (0, 0))
    return pl.pallas_call(
        _mm_res_ln_kernel, grid=(m // tm,),
        in_specs=[pl.BlockSpec((tm, k), lambda i: (i, 0)), pl.BlockSpec((k, d), lambda i: (0, 0)), row, vec, vec],
        out_specs=[row, row],
        out_shape=[jax.ShapeDtypeStruct((m, d), F32), jax.ShapeDtypeStruct((m, d), BF16)],
        compiler_params=_params("parallel"), name=name,
    )(a, w, h, g.reshape(1, d), b.reshape(1, d))


def _branch_kernel(y_ref, att_ref, gs_ref, ga_ref, ws_ref, wa_ref, o_ref):
    bs = jnp.dot(y_ref[...], ws_ref[...], preferred_element_type=F32)
    ba = jnp.dot(att_ref[...], wa_ref[...], preferred_element_type=F32)
    merged = _sigmoid(gs_ref[...].astype(F32)) * bs + _sigmoid(ga_ref[...].astype(F32)) * ba
    o_ref[...] = merged.astype(o_ref.dtype)


def _branch_call(y_ssm, att, gates, ws, wa, tm=512, tn=1024):
    m, ks = y_ssm.shape
    ka = att.shape[1]
    d = ws.shape[1]
    nj = d // tn
    return pl.pallas_call(
        _branch_kernel, grid=(nj, m // tm),
        in_specs=[pl.BlockSpec((tm, ks), lambda j, i: (i, 0)),
                  pl.BlockSpec((tm, ka), lambda j, i: (i, 0)),
                  pl.BlockSpec((tm, tn), lambda j, i: (i, j)),
                  pl.BlockSpec((tm, tn), lambda j, i: (i, nj + j)),
                  pl.BlockSpec((ks, tn), lambda j, i: (0, j)),
                  pl.BlockSpec((ka, tn), lambda j, i: (0, j))],
        out_specs=pl.BlockSpec((tm, tn), lambda j, i: (i, j)),
        out_shape=jax.ShapeDtypeStruct((m, d), BF16),
        compiler_params=_params("parallel", "parallel"), name="branch_merge",
    )(y_ssm, att, gates, gates, ws, wa)


def _rotary_kernel(qkv_ref, cos_ref, sin_ref, *out_refs, n_head):
    n_grp = len(DIL_CONFIGS)
    kt_refs, vt_refs, qt_refs, qv_refs, kv_refs, vv_refs = [out_refs[a * n_grp:(a + 1) * n_grp] for a in range(6)]
    in_tail = pl.program_id(0) >= n_head
    tm = qkv_ref.shape[1]
    half = ROT_DIM // 2
    gw = HEADS_PER_GROUP * ATT_HEAD_DIM
    n_att = n_grp * HEADS_PER_GROUP

    def rot(x, cos, sin):
        lane = lax.broadcasted_iota(jnp.int32, x.shape, 1)
        partner = jnp.where(lane < half, pltpu.roll(x, ATT_HEAD_DIM - half, 1), pltpu.roll(x, half, 1))
        return x * cos + partner * sin

    for g, (_, dil) in enumerate(DIL_CONFIGS):
        for h in range(HEADS_PER_GROUP):
            hd = g * HEADS_PER_GROUP + h
            tok = slice(h * ATT_HEAD_DIM, (h + 1) * ATT_HEAD_DIM)
            kt_refs[g][:, tok] = rot(qkv_ref[n_att + hd], cos_ref[...], sin_ref[...])
            vt_refs[g][:, tok] = qkv_ref[2 * n_att + hd]
        for r in range(dil):
            rows = pl.ds(r, tm // dil, stride=dil) if dil > 1 else slice(None)
            cos = cos_ref[rows, :]
            sin = sin_ref[rows, :]
            for h in range(HEADS_PER_GROUP):
                hd = g * HEADS_PER_GROUP + h
                view = slice(r * gw + h * ATT_HEAD_DIM, r * gw + (h + 1) * ATT_HEAD_DIM)
                qv_refs[g][:, view] = rot(qkv_ref[hd, rows, :], cos, sin).astype(BF16)
                kv_refs[g][:, view] = rot(qkv_ref[n_att + hd, rows, :], cos, sin).astype(BF16)
                vv_refs[g][:, view] = qkv_ref[2 * n_att + hd, rows, :].astype(BF16)

    @pl.when(in_tail)
    def _():
        for g in range(n_grp):
            for h in range(HEADS_PER_GROUP):
                tok = slice(h * ATT_HEAD_DIM, (h + 1) * ATT_HEAD_DIM)
                qt_refs[g][:, tok] = rot(qkv_ref[g * HEADS_PER_GROUP + h], cos_ref[...], sin_ref[...])


def _rotary_call(qkv, cos_t, sin_t, n_head_rows, tm=256):
    n_slab, t, _ = qkv.shape
    gw = HEADS_PER_GROUP * ATT_HEAD_DIM
    n_grp = len(DIL_CONFIGS)
    n_head = n_head_rows // tm
    row = pl.BlockSpec((tm, gw), lambda i: (i, 0))
    tail = pl.BlockSpec((tm, gw), lambda i: (jnp.maximum(i - n_head, 0), 0))
    tab = pl.BlockSpec((tm, ATT_HEAD_DIM), lambda i: (i, 0))
    views = [pl.BlockSpec((tm // dil, dil * gw), lambda i: (i, 0)) for _, dil in DIL_CONFIGS]
    view_shapes = [jax.ShapeDtypeStruct((t // dil, dil * gw), BF16) for _, dil in DIL_CONFIGS]
    return pl.pallas_call(
        functools.partial(_rotary_kernel, n_head=n_head), grid=(t // tm,),
        in_specs=[pl.BlockSpec((n_slab, tm, ATT_HEAD_DIM), lambda i: (0, i, 0)), tab, tab],
        out_specs=[row] * (2 * n_grp) + [tail] * n_grp + views * 3,
        out_shape=([jax.ShapeDtypeStruct((t, gw), F32)] * (2 * n_grp)
                   + [jax.ShapeDtypeStruct((t - n_head_rows, gw), F32)] * n_grp + view_shapes * 3),
        compiler_params=_params("arbitrary"), name="rotary_split",
    )(qkv, cos_t, sin_t)


def _rotary_tables(pos):
    half = ROT_DIM // 2
    inv_freq = jnp.exp(-math.log(ROPE_THETA) * jnp.arange(half, dtype=F32) * (2.0 / ROT_DIM))
    ang = pos.astype(F32)[:, None] * inv_freq[None, :]
    cos, sin = jnp.cos(ang), jnp.sin(ang)
    rest = ATT_HEAD_DIM - ROT_DIM
    ones = jnp.ones((pos.shape[0], rest), F32)
    cos_t = jnp.concatenate([cos, cos, ones], axis=1)
    sin_t = jnp.concatenate([-sin, sin, jnp.zeros_like(ones)], axis=1)
    return cos_t, sin_t


def _band_attn_kernel(q_ref, kc_ref, kp_ref, vc_ref, vp_ref, o_ref, lse_ref, *, dil, span, has_prev):
    n = pl.program_id(1)
    scale = ATT_HEAD_DIM ** -0.5
    qi = lax.broadcasted_iota(jnp.int32, (ATT_BLOCK, ATT_BLOCK), 0)
    ki = lax.broadcasted_iota(jnp.int32, (ATT_BLOCK, ATT_BLOCK), 1)
    rel_c = qi - ki
    valid_c = (rel_c >= 0) & (rel_c <= span)
    rel_p = rel_c + ATT_BLOCK
    valid_p = (rel_p <= span) & (n > 0)
    nt = (((1,), (1,)), ((), ()))
    batch = 2 * HEADS_PER_GROUP
    problems = [(r, h) for r in range(dil) for h in range(HEADS_PER_GROUP)]
    for b0 in range(0, len(problems), batch):
        todo = problems[b0:b0 + batch]
        cols = [slice((r * HEADS_PER_GROUP + h) * ATT_HEAD_DIM, (r * HEADS_PER_GROUP + h + 1) * ATT_HEAD_DIM)
                for r, h in todo]
        qs = [q_ref[:, cs] for cs in cols]
        s_c = [lax.dot_general(q, kc_ref[:, cs], nt, preferred_element_type=F32) for q, cs in zip(qs, cols)]
        s_c = [jnp.where(valid_c, s * scale, -jnp.inf) for s in s_c]
        m = [jnp.max(s, axis=-1, keepdims=True) for s in s_c]
        if has_prev:
            s_p = [lax.dot_general(q, kp_ref[:, cs], nt, preferred_element_type=F32) for q, cs in zip(qs, cols)]
            s_p = [jnp.where(valid_p, s * scale, -jnp.inf) for s in s_p]
            m = [jnp.maximum(mm, jnp.max(s, axis=-1, keepdims=True)) for mm, s in zip(m, s_p)]
        p_c = [jnp.exp(s - mm) for s, mm in zip(s_c, m)]
        den = [jnp.sum(p, axis=-1, keepdims=True) for p in p_c]
        o = [jnp.dot(p.astype(BF16), vc_ref[:, cs], preferred_element_type=F32) for p, cs in zip(p_c, cols)]
        if has_prev:
            p_p = [jnp.exp(s - mm) for s, mm in zip(s_p, m)]
            den = [d + jnp.sum(p, axis=-1, keepdims=True) for d, p in zip(den, p_p)]
            o = [oo + jnp.dot(p.astype(BF16), vp_ref[:, cs], preferred_element_type=F32)
                 for oo, p, cs in zip(o, p_p, cols)]
        for (r, h), oo, d, mm in zip(todo, o, den, m):
            rows = pl.ds(r, ATT_BLOCK, stride=dil) if dil > 1 else slice(None)
            o_ref[h, rows, :] = oo / d
            lse_ref[h, rows, :] = jnp.broadcast_to(mm + jnp.log(d), (ATT_BLOCK, ATT_HEAD_DIM))


def _band_attn_call(q, k, v, bsz, seq, window, dil, g):
    gw = HEADS_PER_GROUP * ATT_HEAD_DIM
    t_pad = q.shape[0] * dil
    span = window // dil
    assert span <= ATT_BLOCK and seq % (dil * ATT_BLOCK) == 0
    nb = seq // (dil * ATT_BLOCK)
    cur = pl.BlockSpec((ATT_BLOCK, dil * gw), lambda b, n: (b * nb + n, 0))
    prev = pl.BlockSpec((ATT_BLOCK, dil * gw), lambda b, n: (b * nb + jnp.maximum(n - 1, 0), 0))
    out = pl.BlockSpec((HEADS_PER_GROUP, ATT_BLOCK * dil, ATT_HEAD_DIM), lambda b, n: (0, b * nb + n, 0))
    kern = functools.partial(_band_attn_kernel, dil=dil, span=span, has_prev=nb > 1)
    return pl.pallas_call(
        kern, grid=(bsz, nb),
        in_specs=[cur, cur, prev, cur, prev], out_specs=[out, out],
        out_shape=[jax.ShapeDtypeStruct((HEADS_PER_GROUP, t_pad, ATT_HEAD_DIM), F32)] * 2,
        compiler_params=_params("parallel", "parallel"), name=f"band_attn_g{g}",
    )(q, k, k, v, v)


def _sample_attn_kernel(q_ref, kc_ref, vc_ref, kn_ref, vn_ref, o_ref, lse_ref, *, dil, lq):
    scale = ATT_HEAD_DIM ** -0.5
    n_cache = kc_ref.shape[1]
    row = lax.broadcasted_iota(jnp.int32, (n_cache, HEADS_PER_GROUP, 1), 0)
    for i in range(lq):
        res = i if dil > 1 else 0
        new_rows = range(i + 1) if dil == 1 else (i,)
        q = q_ref[0, i]
        s_c = jnp.sum(kc_ref[0, :, res] * q[None], axis=-1, keepdims=True) * scale
        if dil == 1:
            s_c = jnp.where(row >= i, s_c, -jnp.inf)
        s_new = [jnp.sum(kn_ref[0, j] * q, axis=-1, keepdims=True) * scale for j in new_rows]
        m = jnp.max(s_c, axis=0)
        for s in s_new:
            m = jnp.maximum(m, s)
        p_c = jnp.exp(s_c - m[None])
        den = jnp.sum(p_c, axis=0)
        o = jnp.sum(p_c * vc_ref[0, :, res], axis=0)
        for j, s in zip(new_rows, s_new):
            p = jnp.exp(s - m)
            den = den + p
            o = o + p * vn_ref[0, j]
        o_ref[0, i] = o / den
        lse_ref[0, i] = jnp.broadcast_to(m + jnp.log(den), (HEADS_PER_GROUP, ATT_HEAD_DIM))


def _decode_mem_attn_kernel(q_ref, k_ref, v_ref, o_ref):
    scale = q_ref.shape[-1] ** -0.5
    for i in range(q_ref.shape[1]):
        q = q_ref[0, i]
        s = jnp.sum(k_ref[0] * q[None], axis=-1, keepdims=True) * scale
        m = jnp.max(s, axis=0)
        p = jnp.exp(s - m[None])
        den = jnp.sum(p, axis=0)
        o_ref[0, i] = (jnp.sum(p * v_ref[0], axis=0) / den).astype(o_ref.dtype)


def _decode_mem_attn_call(q, mem_k, mem_v):
    b, lq, nh, e = q.shape
    qs = pl.BlockSpec((1, lq, nh, e), lambda i: (i, 0, 0, 0))
    ms = pl.BlockSpec((1, mem_k.shape[1], nh, e), lambda i: (i, 0, 0, 0))
    return pl.pallas_call(
        _decode_mem_attn_kernel, grid=(b,), in_specs=[qs, ms, ms], out_specs=qs,
        out_shape=jax.ShapeDtypeStruct((b, lq, nh, e), F32),
        compiler_params=_params("parallel"), name="mem_attn_sample",
    )(q, mem_k, mem_v)


def _sample_attn_call(q, k_new, v_new, k_buf, v_buf, window, dil, g):
    b, lq, nh, e = q.shape
    assert k_buf.shape[1] == window and window % dil == 0 and (dil == 1 or dil >= lq) and lq <= window // dil
    n_res = min(dil, lq)
    n_cache = window // dil
    cache = lambda a: a.reshape(b, n_cache, dil, nh, e)
    new = pl.BlockSpec((1, lq, nh, e), lambda i: (i, 0, 0, 0))
    buf = pl.BlockSpec((1, n_cache, n_res, nh, e), lambda i: (i, 0, 0, 0, 0))
    kern = functools.partial(_sample_attn_kernel, dil=dil, lq=lq)
    return pl.pallas_call(
        kern, grid=(b,), in_specs=[new, buf, buf, new, new], out_specs=[new, new],
        out_shape=[jax.ShapeDtypeStruct((b, lq, nh, e), F32)] * 2,
        compiler_params=_params("parallel"), name=f"sample_attn_g{g}",
    )(q, cache(k_buf), cache(v_buf), k_new, v_new)


def _combine_kernel(*refs):
    n_grp = len(DIL_CONFIGS)
    o_refs, l_refs, out_ref = refs[:n_grp], refs[n_grp:2 * n_grp], refs[2 * n_grp]
    for h in range(HEADS_PER_GROUP):
        ls = [r[h] for r in l_refs]
        m = functools.reduce(jnp.maximum, ls)
        ws = [jnp.exp(l - m) for l in ls]
        tot = functools.reduce(lambda a, b: a + b, ws)
        acc = functools.reduce(lambda a, b: a + b, [w * r[h] for w, r in zip(ws, o_refs)])
        out_ref[:, h * ATT_HEAD_DIM:(h + 1) * ATT_HEAD_DIM] = (acc / tot).astype(out_ref.dtype)


def _combine_call(outs, lses, tm=512):
    nh, t, e = outs[0].shape
    row = pl.BlockSpec((nh, tm, e), lambda i: (0, i, 0))
    return pl.pallas_call(
        _combine_kernel, grid=(t // tm,), in_specs=[row] * (2 * len(outs)),
        out_specs=pl.BlockSpec((tm, nh * e), lambda i: (i, 0)),
        out_shape=jax.ShapeDtypeStruct((t, nh * e), BF16),
        compiler_params=_params("parallel"), name="combine_groups",
    )(*outs, *lses)


def _ssd_kernel(*refs, n_chunks, valid_len, has_state):
    if has_state:
        (z_ref, xbc_ref, dt_ref, h0_ref, cprev_ref, cw_ref, cb_ref, dtb_ref, alog_ref, dskip_ref, nw_ref, expand_ref,
         y_ref, hfin_ref, ctail_ref, ht_ref, xext_ref) = refs
    else:
        (z_ref, xbc_ref, dt_ref, cw_ref, cb_ref, dtb_ref, alog_ref, dskip_ref, nw_ref, expand_ref,
         y_ref, hfin_ref, ctail_ref, ht_ref, xext_ref) = refs
    c = pl.program_id(1)
    lc = SSM_CHUNK
    n_st = SSM_STATE
    gw = ht_ref.shape[1] // SSM_GROUPS
    d_inner = ht_ref.shape[1]
    heads_per_group = gw // SSM_HEAD_DIM
    n_tr = d_inner // LANES

    @pl.when(c == 0)
    def _():
        if has_state:
            xext_ref[0:SUBLANES, :] = cprev_ref[0]
            for k in range(n_tr):
                ht_ref[:, k * LANES:(k + 1) * LANES] = h0_ref[0, k * LANES:(k + 1) * LANES, :].T
        else:
            xext_ref[0:SUBLANES, :] = jnp.zeros((SUBLANES, xext_ref.shape[1]), F32)
            ht_ref[...] = jnp.zeros(ht_ref.shape, F32)

    xext_ref[SUBLANES:SUBLANES + lc, :] = xbc_ref[...]

    def conv_silu(c0, width):
        acc = cb_ref[:, c0:c0 + width]
        for s in range(CONV_W):
            acc = acc + cw_ref[CONV_W - 1 - s:CONV_W - s, c0:c0 + width] * xext_ref[pl.ds(SUBLANES - s, lc), c0:c0 + width]
        return acc * _sigmoid(acc)

    dtr = dt_ref[...] + dtb_ref[...]
    dt = jnp.maximum(dtr, 0.0) + jnp.log(1.0 + jnp.exp(-jnp.abs(dtr)))
    if valid_len < lc:
        trow = lax.broadcasted_iota(jnp.int32, dt.shape, 0)
        dt = jnp.where(trow < valid_len, dt, 0.0)
    a = dt * (-jnp.exp(alog_ref[...]))
    ti = lax.broadcasted_iota(jnp.int32, (lc, lc), 0)
    si = lax.broadcasted_iota(jnp.int32, (lc, lc), 1)
    tri = si <= ti
    f32_dot = lambda lhs, rhs: jnp.dot(lhs, rhs, preferred_element_type=F32)
    tri_b = tri.astype(BF16)
    acum = functools.reduce(lambda u, v: u + v, [f32_dot(tri_b, p) for p in _split3(a)])
    tri_t = (ti <= si).astype(BF16)
    acum_t = functools.reduce(lambda u, v: u + v, [f32_dot(p, tri_t) for p in _split3(a.T)])
    dt3, ac3 = _split3(dt), _split3(acum)
    lane = lax.broadcasted_iota(jnp.int32, (lc, LANES), 1)
    nt = (((1,), (1,)), ((), ()))

    for g in range(SSM_GROUPS):
        c0 = g * gw
        xs = conv_silu(c0, gw)
        bm = conv_silu(d_inner + g * n_st, n_st)
        cm = conv_silu(d_inner + SSM_GROUPS * n_st + g * n_st, n_st)
        expand = expand_ref[:, c0:c0 + gw]
        dt_x = functools.reduce(lambda u, v: u + v, [f32_dot(p, expand) for p in dt3])
        ac_x = functools.reduce(lambda u, v: u + v, [f32_dot(p, expand) for p in ac3])
        xdt = xs * dt_x
        xdt_b = xdt.astype(BF16)
        cm_b = cm.astype(BF16)
        cb = lax.dot_general(cm_b, bm.astype(BF16), nt, preferred_element_type=F32)
        h_old = ht_ref[:, c0:c0 + gw]
        y = jnp.dot(cm_b, h_old.astype(BF16), preferred_element_type=F32) * jnp.exp(ac_x)
        diag = []
        for k in range(gw // LANES):
            x_pair = xdt_b[:, k * LANES:(k + 1) * LANES]
            y_pair = None
            for hh in range(LANES // SSM_HEAD_DIM):
                h = g * heads_per_group + k * (LANES // SSM_HEAD_DIM) + hh
                seg = acum[:, h:h + 1] - acum_t[h:h + 1, :]
                lmat = jnp.exp(jnp.where(tri, seg, -jnp.inf))
                m_b = (cb * lmat).astype(BF16)
                in_head = (lane >= hh * SSM_HEAD_DIM) & (lane < (hh + 1) * SSM_HEAD_DIM)
                part = jnp.dot(m_b, jnp.where(in_head, x_pair, jnp.zeros_like(x_pair)), preferred_element_type=F32)
                y_pair = part if y_pair is None else y_pair + part
            diag.append(y_pair)
        y = y + jnp.concatenate(diag, axis=1) + dskip_ref[:, c0:c0 + gw] * xs
        zg = z_ref[:, c0:c0 + gw].astype(F32)
        y = y * (zg * _sigmoid(zg))
        y = y * lax.rsqrt(jnp.mean(y * y, axis=-1, keepdims=True) + RMS_EPS)
        y_ref[:, c0:c0 + gw] = (y * nw_ref[:, c0:c0 + gw]).astype(y_ref.dtype)
        a_last = ac_x[lc - 1:lc, :]
        xw = (xdt * jnp.exp(a_last - ac_x)).astype(BF16)
        ht_ref[:, c0:c0 + gw] = h_old * jnp.exp(a_last) + jnp.dot(bm.T.astype(BF16), xw, preferred_element_type=F32)

    xext_ref[0:SUBLANES, :] = xext_ref[lc:lc + SUBLANES, :]

    @pl.when(c == n_chunks - 1)
    def _():
        ctail_ref[0] = xext_ref[0:SUBLANES, :]
        for k in range(n_tr):
            hfin_ref[0, k * LANES:(k + 1) * LANES, :] = ht_ref[:, k * LANES:(k + 1) * LANES].T


def _ssd_call(z, xbc, dt_raw, n_seq, n_chunks, valid_len, state, conv_w, conv_b, dt_bias, a_log, d_skip, norm_w,
              out_rows, name):
    d_inner = z.shape[1]
    conv_dim = xbc.shape[1]
    n_heads = d_inner // SSM_HEAD_DIM
    lc = SSM_CHUNK
    pad_h = lambda v: jnp.pad(v.astype(F32), (0, LANES - n_heads)).reshape(1, LANES)
    rows = lambda w: pl.BlockSpec((lc, w), lambda b, c: (b * n_chunks + c, 0))
    const = lambda r, w: pl.BlockSpec((r, w), lambda b, c: (0, 0))
    per_seq = lambda r, w: pl.BlockSpec((1, r, w), lambda b, c: (b, 0, 0))
    args = [z, xbc, dt_raw]
    specs = [rows(d_inner), rows(conv_dim), rows(LANES)]
    if state is not None:
        args += list(state)
        specs += [per_seq(d_inner, SSM_STATE), per_seq(SUBLANES, conv_dim)]
    args += [conv_w, conv_b.reshape(1, conv_dim), pad_h(dt_bias), pad_h(a_log),
             jnp.repeat(d_skip.astype(F32), SSM_HEAD_DIM).reshape(1, d_inner), norm_w.reshape(1, d_inner),
             (jnp.arange(LANES)[:, None] == jnp.arange(d_inner)[None, :] // SSM_HEAD_DIM).astype(BF16)]
    specs += [const(CONV_W, conv_dim), const(1, conv_dim), const(1, LANES), const(1, LANES),
              const(1, d_inner), const(1, d_inner), const(LANES, d_inner)]
    kern = functools.partial(_ssd_kernel, n_chunks=n_chunks, valid_len=valid_len, has_state=state is not None)
    return pl.pallas_call(
        kern, grid=(n_seq, n_chunks), in_specs=specs,
        out_specs=[rows(d_inner), per_seq(d_inner, SSM_STATE), per_seq(SUBLANES, conv_dim)],
        out_shape=[jax.ShapeDtypeStruct((out_rows, d_inner), BF16),
                   jax.ShapeDtypeStruct((n_seq, d_inner, SSM_STATE), F32),
                   jax.ShapeDtypeStruct((n_seq, SUBLANES, conv_dim), F32)],
        scratch_shapes=[pltpu.VMEM((SSM_STATE, d_inner), F32), pltpu.VMEM((lc + 2 * SUBLANES, conv_dim), F32)],
        compiler_params=_params("parallel", "arbitrary"), name=name,
    )(*args)


def _mem_attn_kernel(q_ref, k_ref, v_ref, o_ref):
    scale = q_ref.shape[1] ** -0.5
    s = lax.dot_general(q_ref[...], k_ref[...].astype(BF16), (((1,), (1,)), ((), ())),
                        preferred_element_type=F32) * scale
    m = jnp.max(s, axis=-1, keepdims=True)
    p = jnp.exp(s - m)
    den = jnp.sum(p, axis=-1, keepdims=True)
    o = jnp.dot(p.astype(BF16), v_ref[...].astype(BF16), preferred_element_type=F32)
    o_ref[...] = (o / den).astype(o_ref.dtype)


def _mem_attn_call(q, mem_k, mem_v, n_seq, lq, tq, out_rows, name):
    d = q.shape[1]
    hd = d // MEM_HEADS
    n_mem = mem_k.shape[0] // n_seq
    nq = lq // tq
    qs = pl.BlockSpec((tq, hd), lambda b, h, i: (b * nq + i, h))
    ks = pl.BlockSpec((n_mem, hd), lambda b, h, i: (b, h))
    return pl.pallas_call(
        _mem_attn_kernel, grid=(n_seq, MEM_HEADS, nq), in_specs=[qs, ks, ks], out_specs=qs,
        out_shape=jax.ShapeDtypeStruct((out_rows, d), BF16),
        compiler_params=_params("parallel", "parallel", "parallel"), name=name,
    )(q, mem_k, mem_v)


def _router_kernel(x_ref, w_ref, b_ref, idx_ref, gate_ref):
    logits = jnp.dot(x_ref[...], w_ref[...], precision=HI, preferred_element_type=F32) + b_ref[...]
    lane = lax.broadcasted_iota(jnp.int32, logits.shape, 1)
    idx_out = jnp.zeros(logits.shape, jnp.int32)
    vals = []
    for k in range(TOP_K):
        m = jnp.max(logits, axis=-1, keepdims=True)
        pick = jnp.min(jnp.where(logits == m, lane, LANES), axis=-1, keepdims=True)
        idx_out = jnp.where(lane == k, pick, idx_out)
        logits = jnp.where(lane == pick, -jnp.inf, logits)
        vals.append(m)
    exps = [jnp.exp(v - vals[0]) for v in vals]
    tot = functools.reduce(lambda a, b: a + b, exps)
    gate_out = jnp.zeros(logits.shape, F32)
    for k in range(TOP_K):
        gate_out = jnp.where(lane == k, exps[k] / tot, gate_out)
    idx_ref[...] = idx_out
    gate_ref[...] = gate_out


def _router_call(x, w_router, b_router, tm=256):
    t, d = x.shape
    n_exp = w_router.shape[1]
    w = jnp.pad(w_router.astype(F32), ((0, 0), (0, LANES - n_exp)))
    b = jnp.pad(b_router.astype(F32), (0, LANES - n_exp), constant_values=-jnp.inf).reshape(1, LANES)
    out = pl.BlockSpec((tm, LANES), lambda i: (i, 0))
    return pl.pallas_call(
        _router_kernel, grid=(t // tm,),
        in_specs=[pl.BlockSpec((tm, d), lambda i: (i, 0)), pl.BlockSpec((d, LANES), lambda i: (0, 0)),
                  pl.BlockSpec((1, LANES), lambda i: (0, 0))],
        out_specs=[out, out],
        out_shape=[jax.ShapeDtypeStruct((t, LANES), jnp.int32), jax.ShapeDtypeStruct((t, LANES), F32)],
        compiler_params=_params("parallel"), name="router",
    )(x, w, b)


def _moe_kernel(nused_ref, bexp_ref, tokc_ref, tokn_ref, dprev_ref, dcur_ref, x_hbm, wg_ref, wu_ref, wd_ref,
                bg_ref, bu_ref, bd_ref, y_hbm, xbuf, xb, acc, obuf, gsem, ssem, *, nj):
    i = pl.program_id(0)
    j = pl.program_id(1)
    n_used = nused_ref[0]
    slot = i % 2
    tm = xb.shape[0]
    rows_per_step = tm // nj

    def gather_copy(tok, r, s):
        return pltpu.make_async_copy(x_hbm.at[pl.ds(tok, 1)], xbuf.at[s, pl.ds(r, 1)], gsem.at[s])

    def scatter_copy(dst, r, s):
        return pltpu.make_async_copy(obuf.at[s, pl.ds(r, 1)], y_hbm.at[pl.ds(dst, 1)], ssem.at[s])

    def start_gather(tok_ref, s):
        def body(r, carry):
            gather_copy(tok_ref[0, 0, r], r, s).start()
            return carry
        lax.fori_loop(0, tm, body, 0)

    def wait_gather(s):
        pltpu.make_async_copy(x_hbm.at[pl.ds(0, tm)], xbuf.at[s], gsem.at[s]).wait()

    def wait_scatter(s):
        pltpu.make_async_copy(obuf.at[s], y_hbm.at[pl.ds(0, tm)], ssem.at[s]).wait()

    @pl.when(i < n_used)
    def _():
        @pl.when(j == 0)
        def _():
            @pl.when(i == 0)
            def _():
                start_gather(tokc_ref, 0)
                obuf[1] = jnp.zeros(obuf.shape[1:], F32)

            wait_gather(slot)
            xb[...] = xbuf[slot].astype(BF16)
            acc[...] = jnp.zeros(acc.shape, F32)

        fetch_steps = max(nj - 1, 1)
        fetch_rows = -(-tm // (fetch_steps * SUBLANES)) * SUBLANES
        for jj in range(nj):
            @pl.when(j == jj)
            def _(jj=jj):
                for r in range(min(jj * fetch_rows, tm), min((jj + 1) * fetch_rows, tm)):
                    gather_copy(tokn_ref[0, 0, r], r, 1 - slot).start()
                for r in range(jj * rows_per_step, (jj + 1) * rows_per_step):
                    scatter_copy(dprev_ref[0, 0, r], r, 1 - slot).start()

        x = xb[...]
        gate = jnp.minimum(jnp.dot(x, wg_ref[0], preferred_element_type=F32) + bg_ref[0], SWIGLU_LIMIT)
        up = jnp.clip(jnp.dot(x, wu_ref[0], preferred_element_type=F32) + bu_ref[0], -SWIGLU_LIMIT, SWIGLU_LIMIT)
        act = (up + 1.0) * gate * _sigmoid(gate * SWIGLU_ALPHA)
        acc[...] += jnp.dot(act.astype(BF16), wd_ref[0], preferred_element_type=F32)

        @pl.when(j == nj - 1)
        def _():
            @pl.when(i >= 1)
            def _():
                wait_scatter(slot)

            obuf[slot] = acc[...] + bd_ref[0]

            @pl.when(i == n_used - 1)
            def _():
                def body(r, carry):
                    scatter_copy(dcur_ref[0, 0, r], r, slot).start()
                    return carry
                lax.fori_loop(0, tm, body, 0)
                wait_gather(1 - slot)
                wait_scatter(1 - slot)
                wait_scatter(slot)


def _moe_call(x, top_idx, w_gate_up, b_gate_up, w_down, b_down):
    t, d = x.shape
    n_exp, _, two_h = w_gate_up.shape
    d_exp = two_h // 2
    tm, tn = MOE_TM, MOE_TN
    nj = d_exp // tn
    n_assign = t * TOP_K
    n_blocks = n_assign // tm + n_exp
    n_slots = n_blocks * tm

    flat_e = top_idx.reshape(n_assign)
    onehot = (flat_e[:, None] == jnp.arange(n_exp, dtype=jnp.int32)[None, :]).astype(jnp.int32)
    counts = jnp.sum(onehot, axis=0)
    rank = jnp.take_along_axis(jnp.cumsum(onehot, axis=0), flat_e[:, None], axis=1)[:, 0] - 1
    padded = (counts + tm - 1) // tm * tm
    pad_end = jnp.cumsum(padded)
    slot = (pad_end - padded)[flat_e] + rank
    assign = jnp.arange(n_assign, dtype=jnp.int32)
    slot_assign = jnp.full((n_slots,), -1, jnp.int32).at[slot].set(assign, unique_indices=True,
                                                                   mode='promise_in_bounds')
    slot_token = jnp.maximum(slot_assign, 0) // TOP_K
    slot_id = jnp.arange(n_slots + tm, dtype=jnp.int32) - tm
    dump = n_assign + ((slot_id // tm) % 2) * tm + slot_id % tm
    shifted = jnp.concatenate([jnp.full((tm,), -1, jnp.int32), slot_assign])
    slot_dest = jnp.where(shifted >= 0, (shifted % TOP_K) * t + shifted // TOP_K, dump)
    n_used = (pad_end[-1] // tm).astype(jnp.int32)
    blk = jnp.minimum(jnp.arange(n_blocks, dtype=jnp.int32), n_used - 1) * tm
    block_expert = jnp.minimum(jnp.sum((pad_end[None, :] <= blk[:, None]).astype(jnp.int32), axis=1), n_exp - 1)

    live = lambda i, nu: i < nu[0]
    tok3 = slot_token.reshape(n_blocks, 1, tm)
    dest3 = slot_dest.reshape(n_blocks + 1, 1, tm)
    smem = lambda imap: pl.BlockSpec((1, 1, tm), imap, memory_space=pltpu.SMEM)
    grid_spec = pltpu.PrefetchScalarGridSpec(
        num_scalar_prefetch=2, grid=(n_blocks, nj),
        in_specs=[
            smem(lambda i, j, nu, be: (i, 0, 0)),
            smem(lambda i, j, nu, be: (jnp.minimum(i + 1, n_blocks - 1), 0, 0)),
            smem(lambda i, j, nu, be: (i, 0, 0)),
            smem(lambda i, j, nu, be: (i + 1, 0, 0)),
            pl.BlockSpec(memory_space=pl.ANY),
            pl.BlockSpec((1, d, tn), lambda i, j, nu, be: (be[i], 0, jnp.where(live(i, nu), j, nj - 1))),
            pl.BlockSpec((1, d, tn), lambda i, j, nu, be: (be[i], 0, nj + jnp.where(live(i, nu), j, nj - 1))),
            pl.BlockSpec((1, tn, d), lambda i, j, nu, be: (be[i], jnp.where(live(i, nu), j, nj - 1), 0)),
            pl.BlockSpec((1, 1, tn), lambda i, j, nu, be: (be[i], 0, jnp.where(live(i, nu), j, nj - 1))),
            pl.BlockSpec((1, 1, tn), lambda i, j, nu, be: (be[i], 0, nj + jnp.where(live(i, nu), j, nj - 1))),
            pl.BlockSpec((1, 1, d), lambda i, j, nu, be: (be[i], 0, 0)),
        ],
        out_specs=pl.BlockSpec(memory_space=pl.ANY),
        scratch_shapes=[pltpu.VMEM((2, tm, d), F32), pltpu.VMEM((tm, d), BF16), pltpu.VMEM((tm, d), F32),
                        pltpu.VMEM((2, tm, d), F32), pltpu.SemaphoreType.DMA((2,)), pltpu.SemaphoreType.DMA((2,))],
    )
    bgu = b_gate_up.astype(F32).reshape(n_exp, 1, two_h)
    return pl.pallas_call(
        functools.partial(_moe_kernel, nj=nj), grid_spec=grid_spec,
        out_shape=jax.ShapeDtypeStruct((n_assign + 2 * tm, d), F32),
        compiler_params=_params("arbitrary", "arbitrary"), name="moe_experts",
    )(n_used.reshape(1), block_expert, tok3, tok3, dest3, dest3, x,
      w_gate_up, w_gate_up, w_down, bgu, bgu, b_down.astype(F32).reshape(n_exp, 1, d))


def _moe_out_kernel(*refs, n_head):
    y_refs, (gate_ref, h_ref, g_ref, b_ref, head_ref, tail_ref) = refs[:TOP_K], refs[TOP_K:]
    gates = gate_ref[...]
    y = gates[:, 0:1] * y_refs[0][...]
    for k in range(1, TOP_K):
        y = y + gates[:, k:k + 1] * y_refs[k][...]
    out = _layer_norm_rows(DN_ALPHA * h_ref[...] + y, g_ref[...], b_ref[...])

    @pl.when(pl.program_id(0) < n_head)
    def _():
        head_ref[...] = out

    @pl.when(pl.program_id(0) >= n_head)
    def _():
        tail_ref[...] = out


def _moe_out_call(y, gates, h, g, b, n_head_rows, tm=256):
    t, d = h.shape
    nb = t // tm
    n_head = n_head_rows // tm
    row = pl.BlockSpec((tm, d), lambda i: (i, 0))
    vec = pl.BlockSpec((1, d), lambda i: (0, 0))
    planes = [pl.BlockSpec((tm, d), functools.partial(lambda k, i: (k * nb + i, 0), k)) for k in range(TOP_K)]
    return pl.pallas_call(
        functools.partial(_moe_out_kernel, n_head=n_head), grid=(nb,),
        in_specs=planes + [pl.BlockSpec((tm, LANES), lambda i: (i, 0)), row, vec, vec],
        out_specs=[pl.BlockSpec((tm, d), lambda i: (jnp.minimum(i, n_head - 1), 0)),
                   pl.BlockSpec((tm, d), lambda i: (jnp.maximum(i - n_head, 0), 0))],
        out_shape=[jax.ShapeDtypeStruct((n_head_rows, d), F32), jax.ShapeDtypeStruct((t - n_head_rows, d), F32)],
        compiler_params=_params("arbitrary"), name="moe_combine_ln",
    )(*([y] * TOP_K), gates, h, g.reshape(1, d), b.reshape(1, d))


def kernel(x_prompt, x_sample, state_conv, state_ssm, cache_k_w128, cache_v_w128, cache_k_w512, cache_v_w512, cache_k_w2048, cache_v_w2048, cache_mem_k, cache_mem_v, mem_prompt, ln_in_g, ln_in_b, w_in, conv_w, conv_b, dt_bias, a_log, d_skip, ssm_norm_w, w_branch_ssm, w_branch_att, w_mix_out, ln1_g, ln1_b, w_mem_q, w_mem_k, w_mem_v, w_mem_o, ln2_g, ln2_b, w_router, b_router, w_gate_up, b_gate_up, w_down, b_down, ln3_g, ln3_b):
    assert w_in.shape[0] == DEPTH
    bp, lp, d = x_prompt.shape
    bs, ls, _ = x_sample.shape
    n_p, n_s = bp * lp, bs * ls
    t_real = n_p + n_s
    t_pad = -(-t_real // ROW_TILE) * ROW_TILE
    d_inner = ssm_norm_w.shape[1]
    conv_dim = conv_w.shape[2]
    n_heads = d_inner // SSM_HEAD_DIM
    gw = HEADS_PER_GROUP * ATT_HEAD_DIM
    att_w = len(DIL_CONFIGS) * gw
    n_mem = mem_prompt.shape[1]
    lc = SSM_CHUNK
    assert lp % lc == 0 and ls <= lc and n_s % SUBLANES == 0

    def sample_rows(a):
        return a[n_p:n_p + n_s]

    def with_sample_rows(a, rows):
        tail = jnp.concatenate([rows.astype(a.dtype), jnp.zeros((t_pad - t_real, a.shape[1]), a.dtype)], axis=0)
        return lax.dynamic_update_slice(a, tail, (n_p, 0))

    def with_sample_heads(a, rows):
        nh, _, e = a.shape
        rows = jnp.transpose(rows.reshape(n_s, nh, e), (1, 0, 2)).astype(a.dtype)
        tail = jnp.concatenate([rows, jnp.zeros((nh, t_pad - t_real, e), a.dtype)], axis=1)
        return lax.dynamic_update_slice(a, tail, (0, n_p, 0))

    x_tail = jnp.concatenate([x_sample.reshape(n_s, d), jnp.zeros((t_pad - t_real, d), x_sample.dtype)], axis=0)
    h0, h0_b = _ln_call(x_prompt.reshape(n_p, d), x_tail, ln_in_g, ln_in_b)

    o_z, o_xbc, o_dt, o_qkv, o_g = 0, d_inner, d_inner + conv_dim, d_inner + conv_dim + n_heads, \
        d_inner + conv_dim + n_heads + 3 * att_w
    w_cols = lambda a, b: w_in[0, :, a:b].astype(BF16)
    n_exp, _, two_h = w_gate_up.shape[1:]
    wgu_src = w_gate_up[0].reshape(n_exp * d, two_h)
    wd_src = w_down[0].reshape(n_exp * (two_h // 2), d)
    steps = lambda n_cols, tn: (n_cols // tn) * (t_pad // 512)
    gu_rows, wd_rows = 128, 256
    gu_chunks, wd_chunks = wgu_src.shape[0] // gu_rows, wd_src.shape[0] // wd_rows
    n_a = min(steps(conv_dim, 1024), gu_chunks)
    n_b = min(steps(d_inner, 1024), gu_chunks - n_a)
    n_c = min(steps(2 * d, 1024), wd_chunks)
    xbc, wgu_b = _mm_call(h0_b, w_cols(o_xbc, o_dt), F32, 512, 1024, "in_xbc", (wgu_src, None, 0, n_a, gu_rows))
    if n_b:
        z, wgu_b = _mm_call(h0_b, w_cols(o_z, o_xbc), BF16, 512, 1024, "in_z", (wgu_src, wgu_b, n_a, n_b, gu_rows))
    else:
        z = _mm_call(h0_b, w_cols(o_z, o_xbc), BF16, 512, 1024, "in_z")
    if gu_chunks - n_a - n_b:
        wgu_b = _cast_call(wgu_src, wgu_b, n_a + n_b, gu_chunks - n_a - n_b, gu_rows)
    w_dt = jnp.pad(w_cols(o_dt, o_qkv), ((0, 0), (0, LANES - n_heads)))
    dt_raw = _mm_call(h0_b, w_dt, F32, 512, LANES, "in_dt")
    qkv = _mm_heads_call(h0_b, w_cols(o_qkv, o_g), F32, 512, att_w, "in_qkv")
    gates, wd_b = _mm_call(h0_b, w_cols(o_g, w_in.shape[2]), BF16, 512, 1024, "in_gates",
                           (wd_src, None, 0, n_c, wd_rows))
    if wd_chunks - n_c:
        wd_b = _cast_call(wd_src, wd_b, n_c, wd_chunks - n_c, wd_rows)
    wgu_b = wgu_b.reshape(n_exp, d, two_h)
    wd_b = wd_b.reshape(n_exp, two_h // 2, d)

    ssd_w = (conv_w[0], conv_b[0], dt_bias[0], a_log[0], d_skip[0], ssm_norm_w[0])
    y_ssm, ssm_p, conv_tail = _ssd_call(z, xbc, dt_raw, bp, lp // lc, lc, None, *ssd_w, out_rows=t_pad,
                                        name="ssd_prompt")

    def pad_seq(a):
        return jnp.pad(sample_rows(a).reshape(bs, ls, a.shape[1]), ((0, 0), (0, lc - ls), (0, 0))).reshape(bs * lc, a.shape[1])

    conv_prev = jnp.pad(state_conv[0], ((0, 0), (SUBLANES - (CONV_W - 1), 0), (0, 0)))
    y_s, ssm_s, _ = _ssd_call(pad_seq(z), pad_seq(xbc), pad_seq(dt_raw), bs, 1, ls,
                              (state_ssm[0].reshape(bs, d_inner, SSM_STATE), conv_prev), *ssd_w,
                              out_rows=bs * lc, name="ssd_sample")
    y_ssm = with_sample_rows(y_ssm, y_s.reshape(bs, lc, d_inner)[:, :ls].reshape(n_s, d_inner))

    pos = jnp.concatenate([jnp.tile(jnp.arange(lp, dtype=jnp.int32), bp),
                           jnp.tile(PAST_LEN + jnp.arange(ls, dtype=jnp.int32), bs),
                           jnp.zeros((t_pad - t_real,), jnp.int32)])
    cos_t, sin_t = _rotary_tables(pos)
    split = _rotary_call(qkv, cos_t, sin_t, n_p)
    n_grp = len(DIL_CONFIGS)
    k_tok, v_tok, q_tail, q_view, k_view, v_view = [split[a * n_grp:(a + 1) * n_grp] for a in range(6)]
    caches = ((cache_k_w128, cache_v_w128), (cache_k_w512, cache_v_w512), (cache_k_w2048, cache_v_w2048))
    outs, lses, kv_p, kv_s = [], [], [], []
    heads = lambda a: a.reshape(bs, ls, HEADS_PER_GROUP, ATT_HEAD_DIM)
    for g, (window, dil) in enumerate(DIL_CONFIGS):
        o_p, l_p = _band_attn_call(q_view[g], k_view[g], v_view[g], bp, lp, window, dil, g)
        k_new, v_new = heads(sample_rows(k_tok[g])), heads(sample_rows(v_tok[g]))
        o_s, l_s = _sample_attn_call(heads(q_tail[g][:n_s]), k_new, v_new, caches[g][0][0], caches[g][1][0],
                                     window, dil, g)
        outs.append(with_sample_heads(o_p, o_s))
        lses.append(with_sample_heads(l_p, l_s))
        keep = min(window, lp)
        for a in (k_tok[g], v_tok[g]):
            kv_p.append(a[:n_p].reshape(bp, lp, HEADS_PER_GROUP, ATT_HEAD_DIM)[:, lp - keep:][None])
        kv_s += [k_new[None], v_new[None]]
    att = _combine_call(outs, lses)

    merged = _branch_call(y_ssm, att, gates, w_branch_ssm[0].astype(BF16), w_branch_att[0].astype(BF16))
    h1, h1_b = _mm_res_ln_call(merged, w_mix_out[0].astype(BF16), h0, ln1_g[0], ln1_b[0], "mix_out_ln1")

    mem_b = mem_prompt.reshape(bp * n_mem, d).astype(BF16)
    mem_k_p = _mm_call(mem_b, w_mem_k[0].astype(BF16), F32, 512, 1024, "mem_k")
    mem_v_p = _mm_call(mem_b, w_mem_v[0].astype(BF16), F32, 512, 1024, "mem_v")
    q_mem = _mm_call(h1_b, w_mem_q[0].astype(BF16), BF16, 512, 1024, "mem_q")
    o_mem = _mem_attn_call(q_mem, mem_k_p, mem_v_p, bp, lp, 512, t_pad, "mem_attn_prompt")
    q_s = sample_rows(q_mem).reshape(bs, ls, MEM_HEADS, d // MEM_HEADS).astype(F32)
    o_mem_s = _decode_mem_attn_call(q_s, cache_mem_k[0], cache_mem_v[0])
    o_mem = with_sample_rows(o_mem, o_mem_s.reshape(n_s, d))
    h2, _ = _mm_res_ln_call(o_mem, w_mem_o[0].astype(BF16), h1, ln2_g[0], ln2_b[0], "mem_o_ln2")

    idx_t, gate_t = _router_call(h2, w_router[0], b_router[0])
    y_moe = _moe_call(h2, idx_t[:, :TOP_K], wgu_b, b_gate_up[0], wd_b, b_down[0])
    h3_head, h3_tail = _moe_out_call(y_moe, gate_t, h2, ln3_g[0], ln3_b[0], n_p)

    y_prompt = h3_head.reshape(bp, lp, d)
    y_sample = h3_tail[:n_s].reshape(bs, ls, d)
    conv_p = conv_tail[:, SUBLANES - (CONV_W - 1):][None]
    xp_s = jnp.concatenate([state_conv[0].astype(xbc.dtype), sample_rows(xbc).reshape(bs, ls, conv_dim)], axis=1)
    conv_s = xp_s[:, -(CONV_W - 1):][None]
    state_shape = (n_heads, SSM_HEAD_DIM, SSM_STATE)
    ssm_p = ssm_p.reshape(1, bp, *state_shape)
    ssm_s = ssm_s.reshape(1, bs, *state_shape)
    mem_shape = (1, bp, n_mem, MEM_HEADS, d // MEM_HEADS)
    return (y_prompt, y_sample, conv_p, ssm_p, *kv_p, mem_k_p.reshape(mem_shape), mem_v_p.reshape(mem_shape),
            conv_s, ssm_s, *kv_s)
```

```python
import functools
import math

import jax
import jax.numpy as jnp
from jax import lax
from jax.experimental import pallas as pl
from jax.experimental.pallas import tpu as pltpu

F32 = jnp.float32
BF16 = jnp.bfloat16

PAST_LEN = 16384
SSM_HEAD_DIM = 64
SSM_GROUPS = 8
SSM_STATE = 128
CONV_W = 4
SSM_CHUNK = 128
RMS_EPS = 1e-5
DIL_CONFIGS = ((128, 1), (512, 4), (2048, 16))
HEADS_PER_GROUP = 4
ATT_HEAD_DIM = 128
ATT_BLOCK = 128
ROT_DIM = ATT_HEAD_DIM // 4
ROPE_THETA = 500000.0
MEM_HEADS = 4
TOP_K = 4
SWIGLU_LIMIT = 7.0
SWIGLU_ALPHA = 1.702
LN_EPS = 1e-5
DEPTH = 1
DN_ALPHA = (2.0 * DEPTH) ** 0.25

LANES = 128
SUBLANES = 8
ROW_TILE = 512
MOE_TM = 512
MOE_TN = 1024
VMEM_LIMIT = 56 * 1024 * 1024
HI = lax.Precision.HIGHEST


def _params(*sem):
    return pltpu.CompilerParams(dimension_semantics=sem, vmem_limit_bytes=VMEM_LIMIT)


def _sigmoid(x):
    return 0.5 * jnp.tanh(0.5 * x) + 0.5


def _split3(x):
    hi = x.astype(BF16)
    r1 = x - hi.astype(F32)
    mid = r1.astype(BF16)
    lo = (r1 - mid.astype(F32)).astype(BF16)
    return hi, mid, lo


def _layer_norm_rows(x, g, b):
    mu = jnp.mean(x, axis=-1, keepdims=True)
    xc = x - mu
    var = jnp.mean(xc * xc, axis=-1, keepdims=True)
    return xc * lax.rsqrt(var + LN_EPS) * g + b


def _ln_kernel(xa_ref, xb_ref, g_ref, b_ref, of_ref, ob_ref, *, n_head):
    def emit(x_ref):
        y = _layer_norm_rows(x_ref[...], g_ref[...], b_ref[...])
        of_ref[...] = y
        ob_ref[...] = y.astype(BF16)

    pl.when(pl.program_id(0) < n_head)(lambda: emit(xa_ref))
    pl.when(pl.program_id(0) >= n_head)(lambda: emit(xb_ref))


def _ln_call(x_head, x_tail, g, b, tm=256):
    d = x_head.shape[1]
    n_head, n_tail = x_head.shape[0] // tm, x_tail.shape[0] // tm
    t = (n_head + n_tail) * tm
    row = pl.BlockSpec((tm, d), lambda i: (i, 0))
    vec = pl.BlockSpec((1, d), lambda i: (0, 0))
    return pl.pallas_call(
        functools.partial(_ln_kernel, n_head=n_head), grid=(n_head + n_tail,),
        in_specs=[pl.BlockSpec((tm, d), lambda i: (jnp.minimum(i, n_head - 1), 0)),
                  pl.BlockSpec((tm, d), lambda i: (jnp.maximum(i - n_head, 0), 0)), vec, vec],
        out_specs=[row, row],
        out_shape=[jax.ShapeDtypeStruct((t, d), F32), jax.ShapeDtypeStruct((t, d), BF16)],
        compiler_params=_params("arbitrary"), name="ln_in",
    )(x_head, x_tail, g.reshape(1, d), b.reshape(1, d))


def _mm_kernel(a_ref, w_ref, *rest, n_cast):
    o_ref = rest[-2] if n_cast else rest[-1]
    if n_cast:
        src_ref, dst_ref = rest[0], rest[-1]
        step = pl.program_id(0) * pl.num_programs(1) + pl.program_id(1)

        @pl.when(step < n_cast)
        def _():
            dst_ref[...] = src_ref[...].astype(dst_ref.dtype)

    o_ref[...] = jnp.dot(a_ref[...], w_ref[...], preferred_element_type=F32).astype(o_ref.dtype)


def _mm_call(a, w, out_dtype, tm, tn, name, cast=None):
    m, k = a.shape
    n = w.shape[1]
    tm = min(tm, m)
    ni = m // tm
    args = [a, w]
    in_specs = [pl.BlockSpec((tm, k), lambda j, i: (i, 0)), pl.BlockSpec((k, tn), lambda j, i: (0, j))]
    out_specs = pl.BlockSpec((tm, tn), lambda j, i: (i, j))
    out_shape = jax.ShapeDtypeStruct((m, n), out_dtype)
    aliases = {}
    n_cast = 0
    if cast is not None:
        src, dst, chunk0, n_cast, rows = cast
        assert 0 < n_cast <= (n // tn) * ni
        chunk = pl.BlockSpec((rows, src.shape[1]), lambda j, i: (chunk0 + jnp.minimum(j * ni + i, n_cast - 1), 0))
        args.append(src)
        in_specs.append(chunk)
        if dst is not None:
            aliases = {len(args): 1}
            args.append(dst)
            in_specs.append(pl.BlockSpec(memory_space=pl.ANY))
        out_specs = [out_specs, chunk]
        out_shape = [out_shape, jax.ShapeDtypeStruct(src.shape, BF16)]
    return pl.pallas_call(
        functools.partial(_mm_kernel, n_cast=n_cast), grid=(n // tn, ni),
        in_specs=in_specs, out_specs=out_specs, out_shape=out_shape, input_output_aliases=aliases,
        compiler_params=_params("arbitrary", "arbitrary") if n_cast else _params("parallel", "parallel"), name=name,
    )(*args)


def _cast_kernel(src_ref, _, dst_ref):
    dst_ref[...] = src_ref[...].astype(dst_ref.dtype)


def _cast_call(src, dst, chunk0, n_chunks, rows):
    chunk = pl.BlockSpec((rows, src.shape[1]), lambda i: (chunk0 + i, 0))
    return pl.pallas_call(
        _cast_kernel, grid=(n_chunks,), in_specs=[chunk, pl.BlockSpec(memory_space=pl.ANY)], out_specs=chunk,
        out_shape=jax.ShapeDtypeStruct(src.shape, BF16), input_output_aliases={1: 0},
        compiler_params=_params("parallel"), name="cast_rest",
    )(src, dst)


def _mm_heads_kernel(a_ref, w_ref, o_ref):
    res = jnp.dot(a_ref[...], w_ref[...], preferred_element_type=F32)
    for h in range(o_ref.shape[0]):
        o_ref[h] = res[:, h * LANES:(h + 1) * LANES].astype(o_ref.dtype)


def _mm_heads_call(a, w, out_dtype, tm, tn, name):
    m, k = a.shape
    n = w.shape[1]
    return pl.pallas_call(
        _mm_heads_kernel, grid=(n // tn, m // tm),
        in_specs=[pl.BlockSpec((tm, k), lambda j, i: (i, 0)), pl.BlockSpec((k, tn), lambda j, i: (0, j))],
        out_specs=pl.BlockSpec((tn // LANES, tm, LANES), lambda j, i: (j, i, 0)),
        out_shape=jax.ShapeDtypeStruct((n // LANES, m, LANES), out_dtype),
        compiler_params=_params("parallel", "parallel"), name=name,
    )(a, w)


def _mm_res_ln_kernel(a_ref, w_ref, h_ref, g_ref, b_ref, of_ref, ob_ref):
    mix = jnp.dot(a_ref[...], w_ref[...], preferred_element_type=F32)
    y = _layer_norm_rows(DN_ALPHA * h_ref[...] + mix, g_ref[...], b_ref[...])
    of_ref[...] = y
    ob_ref[...] = y.astype(BF16)


def _mm_res_ln_call(a, w, h, g, b, name, tm=256):
    m, k = a.shape
    d = w.shape[1]
    row = pl.BlockSpec((tm, d), lambda i: (i, 0))
    vec = pl.BlockSpec((1, d), lambda i: (0, 0))
    return pl.pallas_call(
        _mm_res_ln_kernel, grid=(m // tm,),
        in_specs=[pl.BlockSpec((tm, k), lambda i: (i, 0)), pl.BlockSpec((k, d), lambda i: (0, 0)), row, vec, vec],
        out_specs=[row, row],
        out_shape=[jax.ShapeDtypeStruct((m, d), F32), jax.ShapeDtypeStruct((m, d), BF16)],
        compiler_params=_params("parallel"), name=name,
    )(a, w, h, g.reshape(1, d), b.reshape(1, d))


def _branch_kernel(y_ref, att_ref, gs_ref, ga_ref, ws_ref, wa_ref, o_ref):
    bs = jnp.dot(y_ref[...], ws_ref[...], preferred_element_type=F32)
    ba = jnp.dot(att_ref[...], wa_ref[...], preferred_element_type=F32)
    merged = _sigmoid(gs_ref[...].astype(F32)) * bs + _sigmoid(ga_ref[...].astype(F32)) * ba
    o_ref[...] = merged.astype(o_ref.dtype)


def _branch_call(y_ssm, att, gates, ws, wa, tm=512, tn=1024):
    m, ks = y_ssm.shape
    ka = att.shape[1]
    d = ws.shape[1]
    nj = d // tn
    return pl.pallas_call(
        _branch_kernel, grid=(nj, m // tm),
        in_specs=[pl.BlockSpec((tm, ks), lambda j, i: (i, 0)),
                  pl.BlockSpec((tm, ka), lambda j, i: (i, 0)),
                  pl.BlockSpec((tm, tn), lambda j, i: (i, j)),
                  pl.BlockSpec((tm, tn), lambda j, i: (i, nj + j)),
                  pl.BlockSpec((ks, tn), lambda j, i: (0, j)),
                  pl.BlockSpec((ka, tn), lambda j, i: (0, j))],
        out_specs=pl.BlockSpec((tm, tn), lambda j, i: (i, j)),
        out_shape=jax.ShapeDtypeStruct((m, d), BF16),
        compiler_params=_params("parallel", "parallel"), name="branch_merge",
    )(y_ssm, att, gates, gates, ws, wa)


def _rotary_kernel(qkv_ref, cos_ref, sin_ref, *out_refs, n_head):
    n_grp = len(DIL_CONFIGS)
    kt_refs, vt_refs, qt_refs, qv_refs, kv_refs, vv_refs = [out_refs[a * n_grp:(a + 1) * n_grp] for a in range(6)]
    in_tail = pl.program_id(0) >= n_head
    tm = qkv_ref.shape[1]
    half = ROT_DIM // 2
    gw = HEADS_PER_GROUP * ATT_HEAD_DIM
    n_att = n_grp * HEADS_PER_GROUP

    def rot(x, cos, sin):
        lane = lax.broadcasted_iota(jnp.int32, x.shape, 1)
        partner = jnp.where(lane < half, pltpu.roll(x, ATT_HEAD_DIM - half, 1), pltpu.roll(x, half, 1))
        return x * cos + partner * sin

    for g, (_, dil) in enumerate(DIL_CONFIGS):
        for h in range(HEADS_PER_GROUP):
            hd = g * HEADS_PER_GROUP + h
            tok = slice(h * ATT_HEAD_DIM, (h + 1) * ATT_HEAD_DIM)
            kt_refs[g][:, h, :] = rot(qkv_ref[n_att + hd], cos_ref[...], sin_ref[...])
            vt_refs[g][:, h, :] = qkv_ref[2 * n_att + hd]
        for r in range(dil):
            rows = pl.ds(r, tm // dil, stride=dil) if dil > 1 else slice(None)
            cos = cos_ref[rows, :]
            sin = sin_ref[rows, :]
            for h in range(HEADS_PER_GROUP):
                hd = g * HEADS_PER_GROUP + h
                view = slice(r * gw + h * ATT_HEAD_DIM, r * gw + (h + 1) * ATT_HEAD_DIM)
                qv_refs[g][:, view] = rot(qkv_ref[hd, rows, :], cos, sin).astype(BF16)
                kv_refs[g][:, view] = rot(qkv_ref[n_att + hd, rows, :], cos, sin).astype(BF16)
                vv_refs[g][:, view] = qkv_ref[2 * n_att + hd, rows, :].astype(BF16)

    @pl.when(in_tail)
    def _():
        for g in range(n_grp):
            for h in range(HEADS_PER_GROUP):
                tok = slice(h * ATT_HEAD_DIM, (h + 1) * ATT_HEAD_DIM)
                qt_refs[g][:, tok] = rot(qkv_ref[g * HEADS_PER_GROUP + h], cos_ref[...], sin_ref[...])


def _rotary_call(qkv, cos_t, sin_t, n_head_rows, tm=256):
    n_slab, t, _ = qkv.shape
    gw = HEADS_PER_GROUP * ATT_HEAD_DIM
    n_grp = len(DIL_CONFIGS)
    n_head = n_head_rows // tm
    row = pl.BlockSpec((tm, HEADS_PER_GROUP, ATT_HEAD_DIM), lambda i: (i, 0, 0))
    tail = pl.BlockSpec((tm, gw), lambda i: (jnp.maximum(i - n_head, 0), 0))
    tab = pl.BlockSpec((tm, ATT_HEAD_DIM), lambda i: (i, 0))
    views = [pl.BlockSpec((tm // dil, dil * gw), lambda i: (i, 0)) for _, dil in DIL_CONFIGS]
    view_shapes = [jax.ShapeDtypeStruct((t // dil, dil * gw), BF16) for _, dil in DIL_CONFIGS]
    return pl.pallas_call(
        functools.partial(_rotary_kernel, n_head=n_head), grid=(t // tm,),
        in_specs=[pl.BlockSpec((n_slab, tm, ATT_HEAD_DIM), lambda i: (0, i, 0)), tab, tab],
        out_specs=[row] * (2 * n_grp) + [tail] * n_grp + views * 3,
        out_shape=([jax.ShapeDtypeStruct((t, HEADS_PER_GROUP, ATT_HEAD_DIM), F32)] * (2 * n_grp)
                   + [jax.ShapeDtypeStruct((t - n_head_rows, gw), F32)] * n_grp + view_shapes * 3),
        compiler_params=_params("arbitrary"), name="rotary_split",
    )(qkv, cos_t, sin_t)


def _rotary_tables(pos):
    half = ROT_DIM // 2
    inv_freq = jnp.exp(-math.log(ROPE_THETA) * jnp.arange(half, dtype=F32) * (2.0 / ROT_DIM))
    ang = pos.astype(F32)[:, None] * inv_freq[None, :]
    cos, sin = jnp.cos(ang), jnp.sin(ang)
    rest = ATT_HEAD_DIM - ROT_DIM
    ones = jnp.ones((pos.shape[0], rest), F32)
    cos_t = jnp.concatenate([cos, cos, ones], axis=1)
    sin_t = jnp.concatenate([-sin, sin, jnp.zeros_like(ones)], axis=1)
    return cos_t, sin_t


def _band_attn_kernel(q_ref, kc_ref, kp_ref, vc_ref, vp_ref, o_ref, lse_ref, *, dil, span, has_prev):
    n = pl.program_id(1)
    scale = ATT_HEAD_DIM ** -0.5
    qi = lax.broadcasted_iota(jnp.int32, (ATT_BLOCK, ATT_BLOCK), 0)
    ki = lax.broadcasted_iota(jnp.int32, (ATT_BLOCK, ATT_BLOCK), 1)
    rel_c = qi - ki
    valid_c = (rel_c >= 0) & (rel_c <= span)
    rel_p = rel_c + ATT_BLOCK
    valid_p = (rel_p <= span) & (n > 0)
    nt = (((1,), (1,)), ((), ()))
    batch = 2 * HEADS_PER_GROUP
    problems = [(r, h) for r in range(dil) for h in range(HEADS_PER_GROUP)]
    for b0 in range(0, len(problems), batch):
        todo = problems[b0:b0 + batch]
        cols = [slice((r * HEADS_PER_GROUP + h) * ATT_HEAD_DIM, (r * HEADS_PER_GROUP + h + 1) * ATT_HEAD_DIM)
                for r, h in todo]
        qs = [q_ref[:, cs] for cs in cols]
        s_c = [lax.dot_general(q, kc_ref[:, cs], nt, preferred_element_type=F32) for q, cs in zip(qs, cols)]
        s_c = [jnp.where(valid_c, s * scale, -jnp.inf) for s in s_c]
        m = [jnp.max(s, axis=-1, keepdims=True) for s in s_c]
        if has_prev:
            s_p = [lax.dot_general(q, kp_ref[:, cs], nt, preferred_element_type=F32) for q, cs in zip(qs, cols)]
            s_p = [jnp.where(valid_p, s * scale, -jnp.inf) for s in s_p]
            m = [jnp.maximum(mm, jnp.max(s, axis=-1, keepdims=True)) for mm, s in zip(m, s_p)]
        p_c = [jnp.exp(s - mm) for s, mm in zip(s_c, m)]
        den = [jnp.sum(p, axis=-1, keepdims=True) for p in p_c]
        o = [jnp.dot(p.astype(BF16), vc_ref[:, cs], preferred_element_type=F32) for p, cs in zip(p_c, cols)]
        if has_prev:
            p_p = [jnp.exp(s - mm) for s, mm in zip(s_p, m)]
            den = [d + jnp.sum(p, axis=-1, keepdims=True) for d, p in zip(den, p_p)]
            o = [oo + jnp.dot(p.astype(BF16), vp_ref[:, cs], preferred_element_type=F32)
                 for oo, p, cs in zip(o, p_p, cols)]
        for (r, h), oo, d, mm in zip(todo, o, den, m):
            rows = pl.ds(r, ATT_BLOCK, stride=dil) if dil > 1 else slice(None)
            o_ref[h, rows, :] = oo / d
            lse_ref[h, rows, :] = jnp.broadcast_to(mm + jnp.log(d), (ATT_BLOCK, ATT_HEAD_DIM))


def _band_attn_call(q, k, v, bsz, seq, window, dil, g):
    gw = HEADS_PER_GROUP * ATT_HEAD_DIM
    t_pad = q.shape[0] * dil
    span = window // dil
    assert span <= ATT_BLOCK and seq % (dil * ATT_BLOCK) == 0
    nb = seq // (dil * ATT_BLOCK)
    cur = pl.BlockSpec((ATT_BLOCK, dil * gw), lambda b, n: (b * nb + n, 0))
    prev = pl.BlockSpec((ATT_BLOCK, dil * gw), lambda b, n: (b * nb + jnp.maximum(n - 1, 0), 0))
    out = pl.BlockSpec((HEADS_PER_GROUP, ATT_BLOCK * dil, ATT_HEAD_DIM), lambda b, n: (0, b * nb + n, 0))
    kern = functools.partial(_band_attn_kernel, dil=dil, span=span, has_prev=nb > 1)
    return pl.pallas_call(
        kern, grid=(bsz, nb),
        in_specs=[cur, cur, prev, cur, prev], out_specs=[out, out],
        out_shape=[jax.ShapeDtypeStruct((HEADS_PER_GROUP, t_pad, ATT_HEAD_DIM), F32)] * 2,
        compiler_params=_params("parallel", "parallel"), name=f"band_attn_g{g}",
    )(q, k, k, v, v)


def _sample_attn_kernel(q_ref, kc_ref, vc_ref, kn_ref, vn_ref, o_ref, lse_ref, *, dil, lq):
    scale = ATT_HEAD_DIM ** -0.5
    n_cache = kc_ref.shape[1]
    row = lax.broadcasted_iota(jnp.int32, (n_cache, HEADS_PER_GROUP, 1), 0)
    for i in range(lq):
        res = i if dil > 1 else 0
        new_rows = range(i + 1) if dil == 1 else (i,)
        q = q_ref[0, i]
        s_c = jnp.sum(kc_ref[0, :, res] * q[None], axis=-1, keepdims=True) * scale
        if dil == 1:
            s_c = jnp.where(row >= i, s_c, -jnp.inf)
        s_new = [jnp.sum(kn_ref[0, j] * q, axis=-1, keepdims=True) * scale for j in new_rows]
        m = jnp.max(s_c, axis=0)
        for s in s_new:
            m = jnp.maximum(m, s)
        p_c = jnp.exp(s_c - m[None])
        den = jnp.sum(p_c, axis=0)
        o = jnp.sum(p_c * vc_ref[0, :, res], axis=0)
        for j, s in zip(new_rows, s_new):
            p = jnp.exp(s - m)
            den = den + p
            o = o + p * vn_ref[0, j]
        o_ref[0, i] = o / den
        lse_ref[0, i] = jnp.broadcast_to(m + jnp.log(den), (HEADS_PER_GROUP, ATT_HEAD_DIM))


def _decode_mem_attn_kernel(q_ref, k_ref, v_ref, o_ref):
    scale = q_ref.shape[-1] ** -0.5
    for i in range(q_ref.shape[1]):
        q = q_ref[0, i]
        s = jnp.sum(k_ref[0] * q[None], axis=-1, keepdims=True) * scale
        m = jnp.max(s, axis=0)
        p = jnp.exp(s - m[None])
        den = jnp.sum(p, axis=0)
        o_ref[0, i] = (jnp.sum(p * v_ref[0], axis=0) / den).astype(o_ref.dtype)


def _decode_mem_attn_call(q, mem_k, mem_v):
    b, lq, nh, e = q.shape
    qs = pl.BlockSpec((1, lq, nh, e), lambda i: (i, 0, 0, 0))
    ms = pl.BlockSpec((1, mem_k.shape[1], nh, e), lambda i: (i, 0, 0, 0))
    return pl.pallas_call(
        _decode_mem_attn_kernel, grid=(b,), in_specs=[qs, ms, ms], out_specs=qs,
        out_shape=jax.ShapeDtypeStruct((b, lq, nh, e), F32),
        compiler_params=_params("parallel"), name="mem_attn_sample",
    )(q, mem_k, mem_v)


def _sample_attn_call(q, k_new, v_new, k_buf, v_buf, window, dil, g):
    b, lq, nh, e = q.shape
    assert k_buf.shape[1] == window and window % dil == 0 and (dil == 1 or dil >= lq) and lq <= window // dil
    n_res = min(dil, lq)
    n_cache = window // dil
    cache = lambda a: a.reshape(b, n_cache, dil, nh, e)
    new = pl.BlockSpec((1, lq, nh, e), lambda i: (i, 0, 0, 0))
    buf = pl.BlockSpec((1, n_cache, n_res, nh, e), lambda i: (i, 0, 0, 0, 0))
    kern = functools.partial(_sample_attn_kernel, dil=dil, lq=lq)
    return pl.pallas_call(
        kern, grid=(b,), in_specs=[new, buf, buf, new, new], out_specs=[new, new],
        out_shape=[jax.ShapeDtypeStruct((b, lq, nh, e), F32)] * 2,
        compiler_params=_params("parallel"), name=f"sample_attn_g{g}",
    )(q, cache(k_buf), cache(v_buf), k_new, v_new)


def _combine_kernel(*refs):
    n_grp = len(DIL_CONFIGS)
    o_refs, l_refs, out_ref = refs[:n_grp], refs[n_grp:2 * n_grp], refs[2 * n_grp]
    for h in range(HEADS_PER_GROUP):
        ls = [r[h] for r in l_refs]
        m = functools.reduce(jnp.maximum, ls)
        ws = [jnp.exp(l - m) for l in ls]
        tot = functools.reduce(lambda a, b: a + b, ws)
        acc = functools.reduce(lambda a, b: a + b, [w * r[h] for w, r in zip(ws, o_refs)])
        out_ref[:, h * ATT_HEAD_DIM:(h + 1) * ATT_HEAD_DIM] = (acc / tot).astype(out_ref.dtype)


def _combine_call(outs, lses, tm=512):
    nh, t, e = outs[0].shape
    row = pl.BlockSpec((nh, tm, e), lambda i: (0, i, 0))
    return pl.pallas_call(
        _combine_kernel, grid=(t // tm,), in_specs=[row] * (2 * len(outs)),
        out_specs=pl.BlockSpec((tm, nh * e), lambda i: (i, 0)),
        out_shape=jax.ShapeDtypeStruct((t, nh * e), BF16),
        compiler_params=_params("parallel"), name="combine_groups",
    )(*outs, *lses)


def _ssd_kernel(*refs, n_chunks, valid_len, has_state):
    if has_state:
        (z_ref, xbc_ref, dt_ref, h0_ref, cprev_ref, cw_ref, cb_ref, dtb_ref, alog_ref, dskip_ref, nw_ref, expand_ref,
         y_ref, hfin_ref, ctail_ref, ht_ref, xext_ref) = refs
    else:
        (z_ref, xbc_ref, dt_ref, cw_ref, cb_ref, dtb_ref, alog_ref, dskip_ref, nw_ref, expand_ref,
         y_ref, hfin_ref, ctail_ref, ht_ref, xext_ref) = refs
    c = pl.program_id(1)
    lc = SSM_CHUNK
    n_st = SSM_STATE
    gw = ht_ref.shape[1] // SSM_GROUPS
    d_inner = ht_ref.shape[1]
    heads_per_group = gw // SSM_HEAD_DIM
    n_tr = d_inner // LANES

    @pl.when(c == 0)
    def _():
        if has_state:
            xext_ref[0:SUBLANES, :] = cprev_ref[0]
            for k in range(n_tr):
                ht_ref[:, k * LANES:(k + 1) * LANES] = h0_ref[0, k * LANES:(k + 1) * LANES, :].T
        else:
            xext_ref[0:SUBLANES, :] = jnp.zeros((SUBLANES, xext_ref.shape[1]), F32)
            ht_ref[...] = jnp.zeros(ht_ref.shape, F32)

    xext_ref[SUBLANES:SUBLANES + lc, :] = xbc_ref[...]

    def conv_silu(c0, width):
        acc = cb_ref[:, c0:c0 + width]
        for s in range(CONV_W):
            acc = acc + cw_ref[CONV_W - 1 - s:CONV_W - s, c0:c0 + width] * xext_ref[pl.ds(SUBLANES - s, lc), c0:c0 + width]
        return acc * _sigmoid(acc)

    dtr = dt_ref[...] + dtb_ref[...]
    dt = jnp.maximum(dtr, 0.0) + jnp.log(1.0 + jnp.exp(-jnp.abs(dtr)))
    if valid_len < lc:
        trow = lax.broadcasted_iota(jnp.int32, dt.shape, 0)
        dt = jnp.where(trow < valid_len, dt, 0.0)
    a = dt * (-jnp.exp(alog_ref[...]))
    ti = lax.broadcasted_iota(jnp.int32, (lc, lc), 0)
    si = lax.broadcasted_iota(jnp.int32, (lc, lc), 1)
    tri = si <= ti
    f32_dot = lambda lhs, rhs: jnp.dot(lhs, rhs, preferred_element_type=F32)
    tri_b = tri.astype(BF16)
    acum = functools.reduce(lambda u, v: u + v, [f32_dot(tri_b, p) for p in _split3(a)])
    tri_t = (ti <= si).astype(BF16)
    acum_t = functools.reduce(lambda u, v: u + v, [f32_dot(p, tri_t) for p in _split3(a.T)])
    dt3, ac3 = _split3(dt), _split3(acum)
    lane = lax.broadcasted_iota(jnp.int32, (lc, LANES), 1)
    nt = (((1,), (1,)), ((), ()))

    for g in range(SSM_GROUPS):
        c0 = g * gw
        xs = conv_silu(c0, gw)
        bm = conv_silu(d_inner + g * n_st, n_st)
        cm = conv_silu(d_inner + SSM_GROUPS * n_st + g * n_st, n_st)
        expand = expand_ref[:, c0:c0 + gw]
        dt_x = functools.reduce(lambda u, v: u + v, [f32_dot(p, expand) for p in dt3])
        ac_x = functools.reduce(lambda u, v: u + v, [f32_dot(p, expand) for p in ac3])
        xdt = xs * dt_x
        xdt_b = xdt.astype(BF16)
        cm_b = cm.astype(BF16)
        cb = lax.dot_general(cm_b, bm.astype(BF16), nt, preferred_element_type=F32)
        h_old = ht_ref[:, c0:c0 + gw]
        y = jnp.dot(cm_b, h_old.astype(BF16), preferred_element_type=F32) * jnp.exp(ac_x)
        diag = []
        for k in range(gw // LANES):
            x_pair = xdt_b[:, k * LANES:(k + 1) * LANES]
            y_pair = None
            for hh in range(LANES // SSM_HEAD_DIM):
                h = g * heads_per_group + k * (LANES // SSM_HEAD_DIM) + hh
                seg = acum[:, h:h + 1] - acum_t[h:h + 1, :]
                lmat = jnp.exp(jnp.where(tri, seg, -jnp.inf))
                m_b = (cb * lmat).astype(BF16)
                in_head = (lane >= hh * SSM_HEAD_DIM) & (lane < (hh + 1) * SSM_HEAD_DIM)
                part = jnp.dot(m_b, jnp.where(in_head, x_pair, jnp.zeros_like(x_pair)), preferred_element_type=F32)
                y_pair = part if y_pair is None else y_pair + part
            diag.append(y_pair)
        y = y + jnp.concatenate(diag, axis=1) + dskip_ref[:, c0:c0 + gw] * xs
        zg = z_ref[:, c0:c0 + gw].astype(F32)
        y = y * (zg * _sigmoid(zg))
        y = y * lax.rsqrt(jnp.mean(y * y, axis=-1, keepdims=True) + RMS_EPS)
        y_ref[:, c0:c0 + gw] = (y * nw_ref[:, c0:c0 + gw]).astype(y_ref.dtype)
        a_last = ac_x[lc - 1:lc, :]
        xw = (xdt * jnp.exp(a_last - ac_x)).astype(BF16)
        ht_ref[:, c0:c0 + gw] = h_old * jnp.exp(a_last) + jnp.dot(bm.T.astype(BF16), xw, preferred_element_type=F32)

    xext_ref[0:SUBLANES, :] = xext_ref[lc:lc + SUBLANES, :]

    @pl.when(c == n_chunks - 1)
    def _():
        ctail_ref[0] = xext_ref[0:SUBLANES, :]
        for k in range(n_tr):
            hfin_ref[0, k * LANES:(k + 1) * LANES, :] = ht_ref[:, k * LANES:(k + 1) * LANES].T


def _ssd_call(z, xbc, dt_raw, n_seq, n_chunks, valid_len, state, conv_w, conv_b, dt_bias, a_log, d_skip, norm_w,
              out_rows, name):
    d_inner = z.shape[1]
    conv_dim = xbc.shape[1]
    n_heads = d_inner // SSM_HEAD_DIM
    lc = SSM_CHUNK
    pad_h = lambda v: jnp.pad(v.astype(F32), (0, LANES - n_heads)).reshape(1, LANES)
    rows = lambda w: pl.BlockSpec((lc, w), lambda b, c: (b * n_chunks + c, 0))
    const = lambda r, w: pl.BlockSpec((r, w), lambda b, c: (0, 0))
    per_seq = lambda r, w: pl.BlockSpec((1, r, w), lambda b, c: (b, 0, 0))
    args = [z, xbc, dt_raw]
    specs = [rows(d_inner), rows(conv_dim), rows(LANES)]
    if state is not None:
        args += list(state)
        specs += [per_seq(d_inner, SSM_STATE), per_seq(SUBLANES, conv_dim)]
    args += [conv_w, conv_b.reshape(1, conv_dim), pad_h(dt_bias), pad_h(a_log),
             jnp.repeat(d_skip.astype(F32), SSM_HEAD_DIM).reshape(1, d_inner), norm_w.reshape(1, d_inner),
             (jnp.arange(LANES)[:, None] == jnp.arange(d_inner)[None, :] // SSM_HEAD_DIM).astype(BF16)]
    specs += [const(CONV_W, conv_dim), const(1, conv_dim), const(1, LANES), const(1, LANES),
              const(1, d_inner), const(1, d_inner), const(LANES, d_inner)]
    kern = functools.partial(_ssd_kernel, n_chunks=n_chunks, valid_len=valid_len, has_state=state is not None)
    return pl.pallas_call(
        kern, grid=(n_seq, n_chunks), in_specs=specs,
        out_specs=[rows(d_inner), per_seq(d_inner, SSM_STATE), per_seq(SUBLANES, conv_dim)],
        out_shape=[jax.ShapeDtypeStruct((out_rows, d_inner), BF16),
                   jax.ShapeDtypeStruct((n_seq, d_inner, SSM_STATE), F32),
                   jax.ShapeDtypeStruct((n_seq, SUBLANES, conv_dim), F32)],
        scratch_shapes=[pltpu.VMEM((SSM_STATE, d_inner), F32), pltpu.VMEM((lc + 2 * SUBLANES, conv_dim), F32)],
        compiler_params=_params("parallel", "arbitrary"), name=name,
    )(*args)


def _mem_attn_kernel(q_ref, k_ref, v_ref, o_ref):
    scale = q_ref.shape[1] ** -0.5
    s = lax.dot_general(q_ref[...], k_ref[...].astype(BF16), (((1,), (1,)), ((), ())),
                        preferred_element_type=F32) * scale
    m = jnp.max(s, axis=-1, keepdims=True)
    p = jnp.exp(s - m)
    den = jnp.sum(p, axis=-1, keepdims=True)
    o = jnp.dot(p.astype(BF16), v_ref[...].astype(BF16), preferred_element_type=F32)
    o_ref[...] = (o / den).astype(o_ref.dtype)


def _mem_attn_call(q, mem_k, mem_v, n_seq, lq, tq, out_rows, name):
    d = q.shape[1]
    hd = d // MEM_HEADS
    n_mem = mem_k.shape[0] // n_seq
    nq = lq // tq
    qs = pl.BlockSpec((tq, hd), lambda b, h, i: (b * nq + i, h))
    ks = pl.BlockSpec((n_mem, hd), lambda b, h, i: (b, h))
    return pl.pallas_call(
        _mem_attn_kernel, grid=(n_seq, MEM_HEADS, nq), in_specs=[qs, ks, ks], out_specs=qs,
        out_shape=jax.ShapeDtypeStruct((out_rows, d), BF16),
        compiler_params=_params("parallel", "parallel", "parallel"), name=name,
    )(q, mem_k, mem_v)


def _router_kernel(x_ref, w_ref, b_ref, idx_ref, gate_ref):
    logits = jnp.dot(x_ref[...], w_ref[...], precision=HI, preferred_element_type=F32) + b_ref[...]
    lane = lax.broadcasted_iota(jnp.int32, logits.shape, 1)
    idx_out = jnp.zeros(logits.shape, jnp.int32)
    vals = []
    for k in range(TOP_K):
        m = jnp.max(logits, axis=-1, keepdims=True)
        pick = jnp.min(jnp.where(logits == m, lane, LANES), axis=-1, keepdims=True)
        idx_out = jnp.where(lane == k, pick, idx_out)
        logits = jnp.where(lane == pick, -jnp.inf, logits)
        vals.append(m)
    exps = [jnp.exp(v - vals[0]) for v in vals]
    tot = functools.reduce(lambda a, b: a + b, exps)
    gate_out = jnp.zeros(logits.shape, F32)
    for k in range(TOP_K):
        gate_out = jnp.where(lane == k, exps[k] / tot, gate_out)
    idx_ref[...] = idx_out
    gate_ref[...] = gate_out


def _router_call(x, w_router, b_router, tm=256):
    t, d = x.shape
    n_exp = w_router.shape[1]
    w = jnp.pad(w_router.astype(F32), ((0, 0), (0, LANES - n_exp)))
    b = jnp.pad(b_router.astype(F32), (0, LANES - n_exp), constant_values=-jnp.inf).reshape(1, LANES)
    out = pl.BlockSpec((tm, LANES), lambda i: (i, 0))
    return pl.pallas_call(
        _router_kernel, grid=(t // tm,),
        in_specs=[pl.BlockSpec((tm, d), lambda i: (i, 0)), pl.BlockSpec((d, LANES), lambda i: (0, 0)),
                  pl.BlockSpec((1, LANES), lambda i: (0, 0))],
        out_specs=[out, out],
        out_shape=[jax.ShapeDtypeStruct((t, LANES), jnp.int32), jax.ShapeDtypeStruct((t, LANES), F32)],
        compiler_params=_params("parallel"), name="router",
    )(x, w, b)


def _moe_kernel(nused_ref, bexp_ref, tokc_ref, tokn_ref, dprev_ref, dcur_ref, x_hbm, wg_ref, wu_ref, wd_ref,
                bg_ref, bu_ref, bd_ref, y_hbm, xbuf, xb, obuf, gsem, ssem, *, nj):
    i = pl.program_id(0)
    j = pl.program_id(1)
    n_used = nused_ref[0]
    slot = i % 2
    tm = xb.shape[0]
    copy_steps = max(nj - 1, 1)
    copy_rows = -(-tm // (copy_steps * SUBLANES)) * SUBLANES

    def gather_copy(tok, r, s):
        return pltpu.make_async_copy(x_hbm.at[pl.ds(tok, 1)], xbuf.at[s, pl.ds(r, 1)], gsem.at[s])

    def scatter_copy(dst, r, s):
        return pltpu.make_async_copy(obuf.at[s, pl.ds(r, 1)], y_hbm.at[pl.ds(dst, 1)], ssem.at[s])

    def start_gather(tok_ref, s):
        def body(r, carry):
            gather_copy(tok_ref[0, 0, r], r, s).start()
            return carry
        lax.fori_loop(0, tm, body, 0)

    def wait_gather(s):
        pltpu.make_async_copy(x_hbm.at[pl.ds(0, tm)], xbuf.at[s], gsem.at[s]).wait()

    def wait_scatter(s):
        pltpu.make_async_copy(obuf.at[s], y_hbm.at[pl.ds(0, tm)], ssem.at[s]).wait()

    @pl.when(i < n_used)
    def _():
        @pl.when(j == 0)
        def _():
            @pl.when(i == 0)
            def _():
                start_gather(tokc_ref, 0)
                obuf[1] = jnp.zeros(obuf.shape[1:], F32)

            wait_gather(slot)
            xb[...] = xbuf[slot].astype(BF16)

            @pl.when(i >= 1)
            def _():
                wait_scatter(slot)

            obuf[slot] = jnp.broadcast_to(bd_ref[0], obuf.shape[1:])

        for jj in range(copy_steps):
            @pl.when(j == jj)
            def _(jj=jj):
                for r in range(min(jj * copy_rows, tm), min((jj + 1) * copy_rows, tm)):
                    gather_copy(tokn_ref[0, 0, r], r, 1 - slot).start()
                    scatter_copy(dprev_ref[0, 0, r], r, 1 - slot).start()

        x = xb[...]
        gate = jnp.minimum(jnp.dot(x, wg_ref[0], preferred_element_type=F32) + bg_ref[0], SWIGLU_LIMIT)
        up = jnp.clip(jnp.dot(x, wu_ref[0], preferred_element_type=F32) + bu_ref[0], -SWIGLU_LIMIT, SWIGLU_LIMIT)
        act = (up + 1.0) * gate * _sigmoid(gate * SWIGLU_ALPHA)
        obuf[slot] += jnp.dot(act.astype(BF16), wd_ref[0], preferred_element_type=F32)

        @pl.when(j == nj - 1)
        def _():
            @pl.when(i == n_used - 1)
            def _():
                def body(r, carry):
                    scatter_copy(dcur_ref[0, 0, r], r, slot).start()
                    return carry
                lax.fori_loop(0, tm, body, 0)
                wait_gather(1 - slot)
                wait_scatter(1 - slot)
                wait_scatter(slot)


def _moe_call(x, top_idx, w_gate_up, b_gate_up, w_down, b_down):
    t, d = x.shape
    n_exp, _, two_h = w_gate_up.shape
    d_exp = two_h // 2
    tm, tn = MOE_TM, MOE_TN
    nj = d_exp // tn
    n_assign = t * TOP_K
    n_blocks = n_assign // tm + n_exp
    n_slots = n_blocks * tm

    flat_e = top_idx.reshape(n_assign)
    onehot = (flat_e[:, None] == jnp.arange(n_exp, dtype=jnp.int32)[None, :]).astype(jnp.int32)
    counts = jnp.sum(onehot, axis=0)
    rank = jnp.take_along_axis(jnp.cumsum(onehot, axis=0), flat_e[:, None], axis=1)[:, 0] - 1
    padded = (counts + tm - 1) // tm * tm
    pad_end = jnp.cumsum(padded)
    slot = (pad_end - padded)[flat_e] + rank
    assign = jnp.arange(n_assign, dtype=jnp.int32)
    slot_assign = jnp.full((n_slots,), -1, jnp.int32).at[slot].set(assign, unique_indices=True,
                                                                   mode='promise_in_bounds')
    slot_token = jnp.maximum(slot_assign, 0) // TOP_K
    slot_id = jnp.arange(n_slots + tm, dtype=jnp.int32) - tm
    dump = n_assign + ((slot_id // tm) % 2) * tm + slot_id % tm
    shifted = jnp.concatenate([jnp.full((tm,), -1, jnp.int32), slot_assign])
    slot_dest = jnp.where(shifted >= 0, (shifted % TOP_K) * t + shifted // TOP_K, dump)
    n_used = (pad_end[-1] // tm).astype(jnp.int32)
    blk = jnp.minimum(jnp.arange(n_blocks, dtype=jnp.int32), n_used - 1) * tm
    block_expert = jnp.minimum(jnp.sum((pad_end[None, :] <= blk[:, None]).astype(jnp.int32), axis=1), n_exp - 1)

    live = lambda i, nu: i < nu[0]
    tok3 = slot_token.reshape(n_blocks, 1, tm)
    dest3 = slot_dest.reshape(n_blocks + 1, 1, tm)
    smem = lambda imap: pl.BlockSpec((1, 1, tm), imap, memory_space=pltpu.SMEM)
    grid_spec = pltpu.PrefetchScalarGridSpec(
        num_scalar_prefetch=2, grid=(n_blocks, nj),
        in_specs=[
            smem(lambda i, j, nu, be: (i, 0, 0)),
            smem(lambda i, j, nu, be: (jnp.minimum(i + 1, n_blocks - 1), 0, 0)),
            smem(lambda i, j, nu, be: (i, 0, 0)),
            smem(lambda i, j, nu, be: (i + 1, 0, 0)),
            pl.BlockSpec(memory_space=pl.ANY),
            pl.BlockSpec((1, d, tn), lambda i, j, nu, be: (be[i], 0, jnp.where(live(i, nu), j, nj - 1))),
            pl.BlockSpec((1, d, tn), lambda i, j, nu, be: (be[i], 0, nj + jnp.where(live(i, nu), j, nj - 1))),
            pl.BlockSpec((1, tn, d), lambda i, j, nu, be: (be[i], jnp.where(live(i, nu), j, nj - 1), 0)),
            pl.BlockSpec((1, 1, tn), lambda i, j, nu, be: (be[i], 0, jnp.where(live(i, nu), j, nj - 1))),
            pl.BlockSpec((1, 1, tn), lambda i, j, nu, be: (be[i], 0, nj + jnp.where(live(i, nu), j, nj - 1))),
            pl.BlockSpec((1, 1, d), lambda i, j, nu, be: (be[i], 0, 0)),
        ],
        out_specs=pl.BlockSpec(memory_space=pl.ANY),
        scratch_shapes=[pltpu.VMEM((2, tm, d), F32), pltpu.VMEM((tm, d), BF16),
                        pltpu.VMEM((2, tm, d), F32), pltpu.SemaphoreType.DMA((2,)), pltpu.SemaphoreType.DMA((2,))],
    )
    bgu = b_gate_up.astype(F32).reshape(n_exp, 1, two_h)
    return pl.pallas_call(
        functools.partial(_moe_kernel, nj=nj), grid_spec=grid_spec,
        out_shape=jax.ShapeDtypeStruct((n_assign + 2 * tm, d), F32),
        compiler_params=_params("arbitrary", "arbitrary"), name="moe_experts",
    )(n_used.reshape(1), block_expert, tok3, tok3, dest3, dest3, x,
      w_gate_up, w_gate_up, w_down, bgu, bgu, b_down.astype(F32).reshape(n_exp, 1, d))


def _moe_out_kernel(*refs, n_head):
    y_refs, (gate_ref, h_ref, g_ref, b_ref, head_ref, tail_ref) = refs[:TOP_K], refs[TOP_K:]
    gates = gate_ref[...]
    y = gates[:, 0:1] * y_refs[0][...]
    for k in range(1, TOP_K):
        y = y + gates[:, k:k + 1] * y_refs[k][...]
    out = _layer_norm_rows(DN_ALPHA * h_ref[...] + y, g_ref[...], b_ref[...])

    @pl.when(pl.program_id(0) < n_head)
    def _():
        head_ref[...] = out

    @pl.when(pl.program_id(0) >= n_head)
    def _():
        tail_ref[...] = out


def _moe_out_call(y, gates, h, g, b, n_head_rows, tm=256):
    t, d = h.shape
    nb = t // tm
    n_head = n_head_rows // tm
    row = pl.BlockSpec((tm, d), lambda i: (i, 0))
    vec = pl.BlockSpec((1, d), lambda i: (0, 0))
    planes = [pl.BlockSpec((tm, d), functools.partial(lambda k, i: (k * nb + i, 0), k)) for k in range(TOP_K)]
    return pl.pallas_call(
        functools.partial(_moe_out_kernel, n_head=n_head), grid=(nb,),
        in_specs=planes + [pl.BlockSpec((tm, LANES), lambda i: (i, 0)), row, vec, vec],
        out_specs=[pl.BlockSpec((tm, d), lambda i: (jnp.minimum(i, n_head - 1), 0)),
                   pl.BlockSpec((tm, d), lambda i: (jnp.maximum(i - n_head, 0), 0))],
        out_shape=[jax.ShapeDtypeStruct((n_head_rows, d), F32), jax.ShapeDtypeStruct((t - n_head_rows, d), F32)],
        compiler_params=_params("arbitrary"), name="moe_combine_ln",
    )(*([y] * TOP_K), gates, h, g.reshape(1, d), b.reshape(1, d))


def kernel(x_prompt, x_sample, state_conv, state_ssm, cache_k_w128, cache_v_w128, cache_k_w512, cache_v_w512, cache_k_w2048, cache_v_w2048, cache_mem_k, cache_mem_v, mem_prompt, ln_in_g, ln_in_b, w_in, conv_w, conv_b, dt_bias, a_log, d_skip, ssm_norm_w, w_branch_ssm, w_branch_att, w_mix_out, ln1_g, ln1_b, w_mem_q, w_mem_k, w_mem_v, w_mem_o, ln2_g, ln2_b, w_router, b_router, w_gate_up, b_gate_up, w_down, b_down, ln3_g, ln3_b):
    assert w_in.shape[0] == DEPTH
    bp, lp, d = x_prompt.shape
    bs, ls, _ = x_sample.shape
    n_p, n_s = bp * lp, bs * ls
    t_real = n_p + n_s
    t_pad = -(-t_real // ROW_TILE) * ROW_TILE
    d_inner = ssm_norm_w.shape[1]
    conv_dim = conv_w.shape[2]
    n_heads = d_inner // SSM_HEAD_DIM
    gw = HEADS_PER_GROUP * ATT_HEAD_DIM
    att_w = len(DIL_CONFIGS) * gw
    n_mem = mem_prompt.shape[1]
    lc = SSM_CHUNK
    assert lp % lc == 0 and ls <= lc and n_s % SUBLANES == 0

    def sample_rows(a):
        return a[n_p:n_p + n_s]

    def with_sample_rows(a, rows):
        tail = jnp.concatenate([rows.astype(a.dtype), jnp.zeros((t_pad - t_real, a.shape[1]), a.dtype)], axis=0)
        return lax.dynamic_update_slice(a, tail, (n_p, 0))

    def with_sample_heads(a, rows):
        nh, _, e = a.shape
        rows = jnp.transpose(rows.reshape(n_s, nh, e), (1, 0, 2)).astype(a.dtype)
        tail = jnp.concatenate([rows, jnp.zeros((nh, t_pad - t_real, e), a.dtype)], axis=1)
        return lax.dynamic_update_slice(a, tail, (0, n_p, 0))

    x_tail = jnp.concatenate([x_sample.reshape(n_s, d), jnp.zeros((t_pad - t_real, d), x_sample.dtype)], axis=0)
    h0, h0_b = _ln_call(x_prompt.reshape(n_p, d), x_tail, ln_in_g, ln_in_b)

    o_z, o_xbc, o_dt, o_qkv, o_g = 0, d_inner, d_inner + conv_dim, d_inner + conv_dim + n_heads, \
        d_inner + conv_dim + n_heads + 3 * att_w
    w_cols = lambda a, b: w_in[0, :, a:b].astype(BF16)
    n_exp, _, two_h = w_gate_up.shape[1:]
    wgu_src = w_gate_up[0].reshape(n_exp * d, two_h)
    wd_src = w_down[0].reshape(n_exp * (two_h // 2), d)
    steps = lambda n_cols, tn: (n_cols // tn) * (t_pad // 512)
    gu_rows, wd_rows = 128, 256
    gu_chunks, wd_chunks = wgu_src.shape[0] // gu_rows, wd_src.shape[0] // wd_rows
    n_a = min(steps(conv_dim, 1024), gu_chunks)
    n_b = min(steps(d_inner, 1024), gu_chunks - n_a)
    n_c = min(steps(2 * d, 1024), wd_chunks)
    xbc, wgu_b = _mm_call(h0_b, w_cols(o_xbc, o_dt), F32, 512, 1024, "in_xbc", (wgu_src, None, 0, n_a, gu_rows))
    if n_b:
        z, wgu_b = _mm_call(h0_b, w_cols(o_z, o_xbc), BF16, 512, 1024, "in_z", (wgu_src, wgu_b, n_a, n_b, gu_rows))
    else:
        z = _mm_call(h0_b, w_cols(o_z, o_xbc), BF16, 512, 1024, "in_z")
    if gu_chunks - n_a - n_b:
        wgu_b = _cast_call(wgu_src, wgu_b, n_a + n_b, gu_chunks - n_a - n_b, gu_rows)
    w_dt = jnp.pad(w_cols(o_dt, o_qkv), ((0, 0), (0, LANES - n_heads)))
    dt_raw = _mm_call(h0_b, w_dt, F32, 512, LANES, "in_dt")
    qkv = _mm_heads_call(h0_b, w_cols(o_qkv, o_g), F32, 512, att_w, "in_qkv")
    gates, wd_b = _mm_call(h0_b, w_cols(o_g, w_in.shape[2]), BF16, 512, 1024, "in_gates",
                           (wd_src, None, 0, n_c, wd_rows))
    if wd_chunks - n_c:
        wd_b = _cast_call(wd_src, wd_b, n_c, wd_chunks - n_c, wd_rows)
    wgu_b = wgu_b.reshape(n_exp, d, two_h)
    wd_b = wd_b.reshape(n_exp, two_h // 2, d)

    ssd_w = (conv_w[0], conv_b[0], dt_bias[0], a_log[0], d_skip[0], ssm_norm_w[0])
    y_ssm, ssm_p, conv_tail = _ssd_call(z, xbc, dt_raw, bp, lp // lc, lc, None, *ssd_w, out_rows=t_pad,
                                        name="ssd_prompt")

    def pad_seq(a):
        return jnp.pad(sample_rows(a).reshape(bs, ls, a.shape[1]), ((0, 0), (0, lc - ls), (0, 0))).reshape(bs * lc, a.shape[1])

    conv_prev = jnp.pad(state_conv[0], ((0, 0), (SUBLANES - (CONV_W - 1), 0), (0, 0)))
    y_s, ssm_s, _ = _ssd_call(pad_seq(z), pad_seq(xbc), pad_seq(dt_raw), bs, 1, ls,
                              (state_ssm[0].reshape(bs, d_inner, SSM_STATE), conv_prev), *ssd_w,
                              out_rows=bs * lc, name="ssd_sample")
    y_ssm = with_sample_rows(y_ssm, y_s.reshape(bs, lc, d_inner)[:, :ls].reshape(n_s, d_inner))

    pos = jnp.concatenate([jnp.tile(jnp.arange(lp, dtype=jnp.int32), bp),
                           jnp.tile(PAST_LEN + jnp.arange(ls, dtype=jnp.int32), bs),
                           jnp.zeros((t_pad - t_real,), jnp.int32)])
    cos_t, sin_t = _rotary_tables(pos)
    split = _rotary_call(qkv, cos_t, sin_t, n_p)
    n_grp = len(DIL_CONFIGS)
    k_tok, v_tok, q_tail, q_view, k_view, v_view = [split[a * n_grp:(a + 1) * n_grp] for a in range(6)]
    caches = ((cache_k_w128, cache_v_w128), (cache_k_w512, cache_v_w512), (cache_k_w2048, cache_v_w2048))
    outs, lses, kv_p, kv_s = [], [], [], []
    heads = lambda a: a.reshape(bs, ls, HEADS_PER_GROUP, ATT_HEAD_DIM)
    for g, (window, dil) in enumerate(DIL_CONFIGS):
        o_p, l_p = _band_attn_call(q_view[g], k_view[g], v_view[g], bp, lp, window, dil, g)
        k_new, v_new = heads(sample_rows(k_tok[g])), heads(sample_rows(v_tok[g]))
        o_s, l_s = _sample_attn_call(heads(q_tail[g][:n_s]), k_new, v_new, caches[g][0][0], caches[g][1][0],
                                     window, dil, g)
        outs.append(with_sample_heads(o_p, o_s))
        lses.append(with_sample_heads(l_p, l_s))
        keep = min(window, lp)
        for a in (k_tok[g], v_tok[g]):
            kv_p.append(a[:n_p].reshape(bp, lp, HEADS_PER_GROUP, ATT_HEAD_DIM)[:, lp - keep:][None])
        kv_s += [k_new[None], v_new[None]]
    att = _combine_call(outs, lses)

    merged = _branch_call(y_ssm, att, gates, w_branch_ssm[0].astype(BF16), w_branch_att[0].astype(BF16))
    h1, h1_b = _mm_res_ln_call(merged, w_mix_out[0].astype(BF16), h0, ln1_g[0], ln1_b[0], "mix_out_ln1")

    mem_b = mem_prompt.reshape(bp * n_mem, d).astype(BF16)
    mem_k_p = _mm_call(mem_b, w_mem_k[0].astype(BF16), F32, 512, 1024, "mem_k")
    mem_v_p = _mm_call(mem_b, w_mem_v[0].astype(BF16), F32, 512, 1024, "mem_v")
    q_mem = _mm_call(h1_b, w_mem_q[0].astype(BF16), BF16, 512, 1024, "mem_q")
    o_mem = _mem_attn_call(q_mem, mem_k_p, mem_v_p, bp, lp, 512, t_pad, "mem_attn_prompt")
    q_s = sample_rows(q_mem).reshape(bs, ls, MEM_HEADS, d // MEM_HEADS).astype(F32)
    o_mem_s = _decode_mem_attn_call(q_s, cache_mem_k[0], cache_mem_v[0])
    o_mem = with_sample_rows(o_mem, o_mem_s.reshape(n_s, d))
    h2, _ = _mm_res_ln_call(o_mem, w_mem_o[0].astype(BF16), h1, ln2_g[0], ln2_b[0], "mem_o_ln2")

    idx_t, gate_t = _router_call(h2, w_router[0], b_router[0])
    y_moe = _moe_call(h2, idx_t[:, :TOP_K], wgu_b, b_gate_up[0], wd_b, b_down[0])
    h3_head, h3_tail = _moe_out_call(y_moe, gate_t, h2, ln3_g[0], ln3_b[0], n_p)

    y_prompt = h3_head.reshape(bp, lp, d)
    y_sample = h3_tail[:n_s].reshape(bs, ls, d)
    conv_p = conv_tail[:, SUBLANES - (CONV_W - 1):][None]
    xp_s = jnp.concatenate([state_conv[0].astype(xbc.dtype), sample_rows(xbc).reshape(bs, ls, conv_dim)], axis=1)
    conv_s = xp_s[:, -(CONV_W - 1):][None]
    state_shape = (n_heads, SSM_HEAD_DIM, SSM_STATE)
    ssm_p = ssm_p.reshape(1, bp, *state_shape)
    ssm_s = ssm_s.reshape(1, bs, *state_shape)
    mem_shape = (1, bp, n_mem, MEM_HEADS, d // MEM_HEADS)
    return (y_prompt, y_sample, conv_p, ssm_p, *kv_p, mem_k_p.reshape(mem_shape), mem_v_p.reshape(mem_shape),
            conv_s, ssm_s, *kv_s)
```

```python
import functools
import math

import jax
import jax.numpy as jnp
from jax import lax
from jax.experimental import pallas as pl
from jax.experimental.pallas import tpu as pltpu

F32 = jnp.float32
BF16 = jnp.bfloat16

PAST_LEN = 16384
SSM_HEAD_DIM = 64
SSM_GROUPS = 8
SSM_STATE = 128
CONV_W = 4
SSM_CHUNK = 128
RMS_EPS = 1e-5
DIL_CONFIGS = ((128, 1), (512, 4), (2048, 16))
HEADS_PER_GROUP = 4
ATT_HEAD_DIM = 128
ATT_BLOCK = 128
ROT_DIM = ATT_HEAD_DIM // 4
ROPE_THETA = 500000.0
MEM_HEADS = 4
TOP_K = 4
SWIGLU_LIMIT = 7.0
SWIGLU_ALPHA = 1.702
LN_EPS = 1e-5
DEPTH = 1
DN_ALPHA = (2.0 * DEPTH) ** 0.25

LANES = 128
SUBLANES = 8
ROW_TILE = 512
MOE_TM = 512
MOE_TN = 1024
VMEM_LIMIT = 56 * 1024 * 1024
HI = lax.Precision.HIGHEST


def _params(*sem):
    return pltpu.CompilerParams(dimension_semantics=sem, vmem_limit_bytes=VMEM_LIMIT)


def _sigmoid(x):
    return 0.5 * jnp.tanh(0.5 * x) + 0.5


def _split3(x):
    hi = x.astype(BF16)
    r1 = x - hi.astype(F32)
    mid = r1.astype(BF16)
    lo = (r1 - mid.astype(F32)).astype(BF16)
    return hi, mid, lo


def _layer_norm_rows(x, g, b):
    mu = jnp.mean(x, axis=-1, keepdims=True)
    xc = x - mu
    var = jnp.mean(xc * xc, axis=-1, keepdims=True)
    return xc * lax.rsqrt(var + LN_EPS) * g + b


def _ln_kernel(xa_ref, xb_ref, g_ref, b_ref, of_ref, ob_ref, *, n_head):
    def emit(x_ref):
        y = _layer_norm_rows(x_ref[...], g_ref[...], b_ref[...])
        of_ref[...] = y
        ob_ref[...] = y.astype(BF16)

    pl.when(pl.program_id(0) < n_head)(lambda: emit(xa_ref))
    pl.when(pl.program_id(0) >= n_head)(lambda: emit(xb_ref))


def _ln_call(x_head, x_tail, g, b, tm=256):
    d = x_head.shape[1]
    n_head, n_tail = x_head.shape[0] // tm, x_tail.shape[0] // tm
    t = (n_head + n_tail) * tm
    row = pl.BlockSpec((tm, d), lambda i: (i, 0))
    vec = pl.BlockSpec((1, d), lambda i: (0, 0))
    return pl.pallas_call(
        functools.partial(_ln_kernel, n_head=n_head), grid=(n_head + n_tail,),
        in_specs=[pl.BlockSpec((tm, d), lambda i: (jnp.minimum(i, n_head - 1), 0)),
                  pl.BlockSpec((tm, d), lambda i: (jnp.maximum(i - n_head, 0), 0)), vec, vec],
        out_specs=[row, row],
        out_shape=[jax.ShapeDtypeStruct((t, d), F32), jax.ShapeDtypeStruct((t, d), BF16)],
        compiler_params=_params("arbitrary"), name="ln_in",
    )(x_head, x_tail, g.reshape(1, d), b.reshape(1, d))


def _mm_kernel(a_ref, w_ref, *rest, n_cast):
    o_ref = rest[-2] if n_cast else rest[-1]
    if n_cast:
        src_ref, dst_ref = rest[0], rest[-1]
        step = pl.program_id(0) * pl.num_programs(1) + pl.program_id(1)

        @pl.when(step < n_cast)
        def _():
            dst_ref[...] = src_ref[...].astype(dst_ref.dtype)

    o_ref[...] = jnp.dot(a_ref[...], w_ref[...], preferred_element_type=F32).astype(o_ref.dtype)


def _mm_call(a, w, out_dtype, tm, tn, name, cast=None):
    m, k = a.shape
    n = w.shape[1]
    tm = min(tm, m)
    ni = m // tm
    args = [a, w]
    in_specs = [pl.BlockSpec((tm, k), lambda j, i: (i, 0)), pl.BlockSpec((k, tn), lambda j, i: (0, j))]
    out_specs = pl.BlockSpec((tm, tn), lambda j, i: (i, j))
    out_shape = jax.ShapeDtypeStruct((m, n), out_dtype)
    aliases = {}
    n_cast = 0
    if cast is not None:
        src, dst, chunk0, n_cast, rows = cast
        assert 0 < n_cast <= (n // tn) * ni
        chunk = pl.BlockSpec((rows, src.shape[1]), lambda j, i: (chunk0 + jnp.minimum(j * ni + i, n_cast - 1), 0))
        args.append(src)
        in_specs.append(chunk)
        if dst is not None:
            aliases = {len(args): 1}
            args.append(dst)
            in_specs.append(pl.BlockSpec(memory_space=pl.ANY))
        out_specs = [out_specs, chunk]
        out_shape = [out_shape, jax.ShapeDtypeStruct(src.shape, BF16)]
    return pl.pallas_call(
        functools.partial(_mm_kernel, n_cast=n_cast), grid=(n // tn, ni),
        in_specs=in_specs, out_specs=out_specs, out_shape=out_shape, input_output_aliases=aliases,
        compiler_params=_params("arbitrary", "arbitrary") if n_cast else _params("parallel", "parallel"), name=name,
    )(*args)


def _cast_kernel(src_ref, _, dst_ref):
    dst_ref[...] = src_ref[...].astype(dst_ref.dtype)


def _cast_call(src, dst, chunk0, n_chunks, rows):
    chunk = pl.BlockSpec((rows, src.shape[1]), lambda i: (chunk0 + i, 0))
    return pl.pallas_call(
        _cast_kernel, grid=(n_chunks,), in_specs=[chunk, pl.BlockSpec(memory_space=pl.ANY)], out_specs=chunk,
        out_shape=jax.ShapeDtypeStruct(src.shape, BF16), input_output_aliases={1: 0},
        compiler_params=_params("parallel"), name="cast_rest",
    )(src, dst)


def _mm_heads_kernel(a_ref, w_ref, o_ref):
    res = jnp.dot(a_ref[...], w_ref[...], preferred_element_type=F32)
    for h in range(o_ref.shape[0]):
        o_ref[h] = res[:, h * LANES:(h + 1) * LANES].astype(o_ref.dtype)


def _mm_heads_call(a, w, out_dtype, tm, tn, name):
    m, k = a.shape
    n = w.shape[1]
    return pl.pallas_call(
        _mm_heads_kernel, grid=(n // tn, m // tm),
        in_specs=[pl.BlockSpec((tm, k), lambda j, i: (i, 0)), pl.BlockSpec((k, tn), lambda j, i: (0, j))],
        out_specs=pl.BlockSpec((tn // LANES, tm, LANES), lambda j, i: (j, i, 0)),
        out_shape=jax.ShapeDtypeStruct((n // LANES, m, LANES), out_dtype),
        compiler_params=_params("parallel", "parallel"), name=name,
    )(a, w)


def _mm_res_ln_kernel(a_ref, w_ref, h_ref, g_ref, b_ref, of_ref, ob_ref):
    mix = jnp.dot(a_ref[...], w_ref[...], preferred_element_type=F32)
    y = _layer_norm_rows(DN_ALPHA * h_ref[...] + mix, g_ref[...], b_ref[...])
    of_ref[...] = y
    ob_ref[...] = y.astype(BF16)


def _mm_res_ln_call(a, w, h, g, b, name, tm=256):
    m, k = a.shape
    d = w.shape[1]
    row = pl.BlockSpec((tm, d), lambda i: (i, 0))
    vec = pl.BlockSpec((1, d), lambda i: (0, 0))
    return pl.pallas_call(
        _mm_res_ln_kernel, grid=(m // tm,),
        in_specs=[pl.BlockSpec((tm, k), lambda i: (i, 0)), pl.BlockSpec((k, d), lambda i: (0, 0)), row, vec, vec],
        out_specs=[row, row],
        out_shape=[jax.ShapeDtypeStruct((m, d), F32), jax.ShapeDtypeStruct((m, d), BF16)],
        compiler_params=_params("parallel"), name=name,
    )(a, w, h, g.reshape(1, d), b.reshape(1, d))


def _branch_kernel(y_ref, att_ref, gs_ref, ga_ref, ws_ref, wa_ref, o_ref):
    bs = jnp.dot(y_ref[...], ws_ref[...], preferred_element_type=F32)
    ba = jnp.dot(att_ref[...], wa_ref[...], preferred_element_type=F32)
    merged = _sigmoid(gs_ref[...].astype(F32)) * bs + _sigmoid(ga_ref[...].astype(F32)) * ba
    o_ref[...] = merged.astype(o_ref.dtype)


def _branch_call(y_ssm, att, gates, ws, wa, tm=512, tn=1024):
    m, ks = y_ssm.shape
    ka = att.shape[1]
    d = ws.shape[1]
    nj = d // tn
    return pl.pallas_call(
        _branch_kernel, grid=(nj, m // tm),
        in_specs=[pl.BlockSpec((tm, ks), lambda j, i: (i, 0)),
                  pl.BlockSpec((tm, ka), lambda j, i: (i, 0)),
                  pl.BlockSpec((tm, tn), lambda j, i: (i, j)),
                  pl.BlockSpec((tm, tn), lambda j, i: (i, nj + j)),
                  pl.BlockSpec((ks, tn), lambda j, i: (0, j)),
                  pl.BlockSpec((ka, tn), lambda j, i: (0, j))],
        out_specs=pl.BlockSpec((tm, tn), lambda j, i: (i, j)),
        out_shape=jax.ShapeDtypeStruct((m, d), BF16),
        compiler_params=_params("parallel", "parallel"), name="branch_merge",
    )(y_ssm, att, gates, gates, ws, wa)


def _rotary_kernel(qkv_ref, cos_ref, sin_ref, *out_refs, n_head):
    n_grp = len(DIL_CONFIGS)
    kt_refs, vt_refs, qt_refs, qv_refs, kv_refs, vv_refs = [out_refs[a * n_grp:(a + 1) * n_grp] for a in range(6)]
    in_tail = pl.program_id(0) >= n_head
    tm = qkv_ref.shape[1]
    half = ROT_DIM // 2
    gw = HEADS_PER_GROUP * ATT_HEAD_DIM
    n_att = n_grp * HEADS_PER_GROUP

    def rot(x, cos, sin):
        lane = lax.broadcasted_iota(jnp.int32, x.shape, 1)
        partner = jnp.where(lane < half, pltpu.roll(x, ATT_HEAD_DIM - half, 1), pltpu.roll(x, half, 1))
        return x * cos + partner * sin

    for g, (_, dil) in enumerate(DIL_CONFIGS):
        for h in range(HEADS_PER_GROUP):
            hd = g * HEADS_PER_GROUP + h
            tok = slice(h * ATT_HEAD_DIM, (h + 1) * ATT_HEAD_DIM)
            kt_refs[g][:, h, :] = rot(qkv_ref[n_att + hd], cos_ref[...], sin_ref[...])
            vt_refs[g][:, h, :] = qkv_ref[2 * n_att + hd]
        for r in range(dil):
            rows = pl.ds(r, tm // dil, stride=dil) if dil > 1 else slice(None)
            cos = cos_ref[rows, :]
            sin = sin_ref[rows, :]
            for h in range(HEADS_PER_GROUP):
                hd = g * HEADS_PER_GROUP + h
                view = slice(r * gw + h * ATT_HEAD_DIM, r * gw + (h + 1) * ATT_HEAD_DIM)
                qv_refs[g][:, view] = rot(qkv_ref[hd, rows, :], cos, sin).astype(BF16)
                kv_refs[g][:, view] = rot(qkv_ref[n_att + hd, rows, :], cos, sin).astype(BF16)
                vv_refs[g][:, view] = qkv_ref[2 * n_att + hd, rows, :].astype(BF16)

    @pl.when(in_tail)
    def _():
        for g in range(n_grp):
            for h in range(HEADS_PER_GROUP):
                tok = slice(h * ATT_HEAD_DIM, (h + 1) * ATT_HEAD_DIM)
                qt_refs[g][:, tok] = rot(qkv_ref[g * HEADS_PER_GROUP + h], cos_ref[...], sin_ref[...])


def _rotary_call(qkv, cos_t, sin_t, n_head_rows, tm=256):
    n_slab, t, _ = qkv.shape
    gw = HEADS_PER_GROUP * ATT_HEAD_DIM
    n_grp = len(DIL_CONFIGS)
    n_head = n_head_rows // tm
    row = pl.BlockSpec((tm, HEADS_PER_GROUP, ATT_HEAD_DIM), lambda i: (i, 0, 0))
    tail = pl.BlockSpec((tm, gw), lambda i: (jnp.maximum(i - n_head, 0), 0))
    tab = pl.BlockSpec((tm, ATT_HEAD_DIM), lambda i: (i, 0))
    views = [pl.BlockSpec((tm // dil, dil * gw), lambda i: (i, 0)) for _, dil in DIL_CONFIGS]
    view_shapes = [jax.ShapeDtypeStruct((t // dil, dil * gw), BF16) for _, dil in DIL_CONFIGS]
    return pl.pallas_call(
        functools.partial(_rotary_kernel, n_head=n_head), grid=(t // tm,),
        in_specs=[pl.BlockSpec((n_slab, tm, ATT_HEAD_DIM), lambda i: (0, i, 0)), tab, tab],
        out_specs=[row] * (2 * n_grp) + [tail] * n_grp + views * 3,
        out_shape=([jax.ShapeDtypeStruct((t, HEADS_PER_GROUP, ATT_HEAD_DIM), F32)] * (2 * n_grp)
                   + [jax.ShapeDtypeStruct((t - n_head_rows, gw), F32)] * n_grp + view_shapes * 3),
        compiler_params=_params("arbitrary"), name="rotary_split",
    )(qkv, cos_t, sin_t)


def _rotary_tables(pos):
    half = ROT_DIM // 2
    inv_freq = jnp.exp(-math.log(ROPE_THETA) * jnp.arange(half, dtype=F32) * (2.0 / ROT_DIM))
    ang = pos.astype(F32)[:, None] * inv_freq[None, :]
    cos, sin = jnp.cos(ang), jnp.sin(ang)
    rest = ATT_HEAD_DIM - ROT_DIM
    ones = jnp.ones((pos.shape[0], rest), F32)
    cos_t = jnp.concatenate([cos, cos, ones], axis=1)
    sin_t = jnp.concatenate([-sin, sin, jnp.zeros_like(ones)], axis=1)
    return cos_t, sin_t


def _band_attn_kernel(q_ref, kc_ref, kp_ref, vc_ref, vp_ref, o_ref, lse_ref, *, dil, span, has_prev):
    n = pl.program_id(1)
    scale = ATT_HEAD_DIM ** -0.5
    qi = lax.broadcasted_iota(jnp.int32, (ATT_BLOCK, ATT_BLOCK), 0)
    ki = lax.broadcasted_iota(jnp.int32, (ATT_BLOCK, ATT_BLOCK), 1)
    rel_c = qi - ki
    valid_c = (rel_c >= 0) & (rel_c <= span)
    rel_p = rel_c + ATT_BLOCK
    valid_p = (rel_p <= span) & (n > 0)
    nt = (((1,), (1,)), ((), ()))
    batch = 2 * HEADS_PER_GROUP
    problems = [(r, h) for r in range(dil) for h in range(HEADS_PER_GROUP)]
    for b0 in range(0, len(problems), batch):
        todo = problems[b0:b0 + batch]
        cols = [slice((r * HEADS_PER_GROUP + h) * ATT_HEAD_DIM, (r * HEADS_PER_GROUP + h + 1) * ATT_HEAD_DIM)
                for r, h in todo]
        qs = [q_ref[:, cs] for cs in cols]
        s_c = [lax.dot_general(q, kc_ref[:, cs], nt, preferred_element_type=F32) for q, cs in zip(qs, cols)]
        s_c = [jnp.where(valid_c, s * scale, -jnp.inf) for s in s_c]
        m = [jnp.max(s, axis=-1, keepdims=True) for s in s_c]
        if has_prev:
            s_p = [lax.dot_general(q, kp_ref[:, cs], nt, preferred_element_type=F32) for q, cs in zip(qs, cols)]
            s_p = [jnp.where(valid_p, s * scale, -jnp.inf) for s in s_p]
            m = [jnp.maximum(mm, jnp.max(s, axis=-1, keepdims=True)) for mm, s in zip(m, s_p)]
        p_c = [jnp.exp(s - mm) for s, mm in zip(s_c, m)]
        den = [jnp.sum(p, axis=-1, keepdims=True) for p in p_c]
        o = [jnp.dot(p.astype(BF16), vc_ref[:, cs], preferred_element_type=F32) for p, cs in zip(p_c, cols)]
        if has_prev:
            p_p = [jnp.exp(s - mm) for s, mm in zip(s_p, m)]
            den = [d + jnp.sum(p, axis=-1, keepdims=True) for d, p in zip(den, p_p)]
            o = [oo + jnp.dot(p.astype(BF16), vp_ref[:, cs], preferred_element_type=F32)
                 for oo, p, cs in zip(o, p_p, cols)]
        for (r, h), oo, d, mm in zip(todo, o, den, m):
            rows = pl.ds(r, ATT_BLOCK, stride=dil) if dil > 1 else slice(None)
            o_ref[h, rows, :] = oo / d
            lse_ref[h, rows, :] = jnp.broadcast_to(mm + jnp.log(d), (ATT_BLOCK, ATT_HEAD_DIM))


def _band_attn_call(q, k, v, bsz, seq, window, dil, g):
    gw = HEADS_PER_GROUP * ATT_HEAD_DIM
    t_pad = q.shape[0] * dil
    span = window // dil
    assert span <= ATT_BLOCK and seq % (dil * ATT_BLOCK) == 0
    nb = seq // (dil * ATT_BLOCK)
    cur = pl.BlockSpec((ATT_BLOCK, dil * gw), lambda b, n: (b * nb + n, 0))
    prev = pl.BlockSpec((ATT_BLOCK, dil * gw), lambda b, n: (b * nb + jnp.maximum(n - 1, 0), 0))
    out = pl.BlockSpec((HEADS_PER_GROUP, ATT_BLOCK * dil, ATT_HEAD_DIM), lambda b, n: (0, b * nb + n, 0))
    kern = functools.partial(_band_attn_kernel, dil=dil, span=span, has_prev=nb > 1)
    return pl.pallas_call(
        kern, grid=(bsz, nb),
        in_specs=[cur, cur, prev, cur, prev], out_specs=[out, out],
        out_shape=[jax.ShapeDtypeStruct((HEADS_PER_GROUP, t_pad, ATT_HEAD_DIM), F32)] * 2,
        compiler_params=_params("parallel", "parallel"), name=f"band_attn_g{g}",
    )(q, k, k, v, v)


def _sample_attn_kernel(q_ref, kc_ref, vc_ref, kn_ref, vn_ref, o_ref, lse_ref, *, dil, lq):
    scale = ATT_HEAD_DIM ** -0.5
    n_cache = kc_ref.shape[1]
    row = lax.broadcasted_iota(jnp.int32, (n_cache, HEADS_PER_GROUP, 1), 0)
    for i in range(lq):
        res = i if dil > 1 else 0
        new_rows = range(i + 1) if dil == 1 else (i,)
        q = q_ref[0, i]
        s_c = jnp.sum(kc_ref[0, :, res] * q[None], axis=-1, keepdims=True) * scale
        if dil == 1:
            s_c = jnp.where(row >= i, s_c, -jnp.inf)
        s_new = [jnp.sum(kn_ref[0, j] * q, axis=-1, keepdims=True) * scale for j in new_rows]
        m = jnp.max(s_c, axis=0)
        for s in s_new:
            m = jnp.maximum(m, s)
        p_c = jnp.exp(s_c - m[None])
        den = jnp.sum(p_c, axis=0)
        o = jnp.sum(p_c * vc_ref[0, :, res], axis=0)
        for j, s in zip(new_rows, s_new):
            p = jnp.exp(s - m)
            den = den + p
            o = o + p * vn_ref[0, j]
        o_ref[0, i] = o / den
        lse_ref[0, i] = jnp.broadcast_to(m + jnp.log(den), (HEADS_PER_GROUP, ATT_HEAD_DIM))


def _decode_mem_attn_kernel(q_ref, k_ref, v_ref, o_ref):
    scale = q_ref.shape[-1] ** -0.5
    for i in range(q_ref.shape[1]):
        q = q_ref[0, i]
        s = jnp.sum(k_ref[0] * q[None], axis=-1, keepdims=True) * scale
        m = jnp.max(s, axis=0)
        p = jnp.exp(s - m[None])
        den = jnp.sum(p, axis=0)
        o_ref[0, i] = (jnp.sum(p * v_ref[0], axis=0) / den).astype(o_ref.dtype)


def _decode_mem_attn_call(q, mem_k, mem_v):
    b, lq, nh, e = q.shape
    qs = pl.BlockSpec((1, lq, nh, e), lambda i: (i, 0, 0, 0))
    ms = pl.BlockSpec((1, mem_k.shape[1], nh, e), lambda i: (i, 0, 0, 0))
    return pl.pallas_call(
        _decode_mem_attn_kernel, grid=(b,), in_specs=[qs, ms, ms], out_specs=qs,
        out_shape=jax.ShapeDtypeStruct((b, lq, nh, e), F32),
        compiler_params=_params("parallel"), name="mem_attn_sample",
    )(q, mem_k, mem_v)


def _sample_attn_call(q, k_new, v_new, k_buf, v_buf, window, dil, g):
    b, lq, nh, e = q.shape
    assert k_buf.shape[1] == window and window % dil == 0 and (dil == 1 or dil >= lq) and lq <= window // dil
    n_res = min(dil, lq)
    n_cache = window // dil
    cache = lambda a: a.reshape(b, n_cache, dil, nh, e)
    new = pl.BlockSpec((1, lq, nh, e), lambda i: (i, 0, 0, 0))
    buf = pl.BlockSpec((1, n_cache, n_res, nh, e), lambda i: (i, 0, 0, 0, 0))
    kern = functools.partial(_sample_attn_kernel, dil=dil, lq=lq)
    return pl.pallas_call(
        kern, grid=(b,), in_specs=[new, buf, buf, new, new], out_specs=[new, new],
        out_shape=[jax.ShapeDtypeStruct((b, lq, nh, e), F32)] * 2,
        compiler_params=_params("parallel"), name=f"sample_attn_g{g}",
    )(q, cache(k_buf), cache(v_buf), k_new, v_new)


def _combine_kernel(*refs):
    n_grp = len(DIL_CONFIGS)
    o_refs, l_refs, out_ref = refs[:n_grp], refs[n_grp:2 * n_grp], refs[2 * n_grp]
    for h in range(HEADS_PER_GROUP):
        ls = [r[h] for r in l_refs]
        m = functools.reduce(jnp.maximum, ls)
        ws = [jnp.exp(l - m) for l in ls]
        tot = functools.reduce(lambda a, b: a + b, ws)
        acc = functools.reduce(lambda a, b: a + b, [w * r[h] for w, r in zip(ws, o_refs)])
        out_ref[:, h * ATT_HEAD_DIM:(h + 1) * ATT_HEAD_DIM] = (acc / tot).astype(out_ref.dtype)


def _combine_call(outs, lses, tm=512):
    nh, t, e = outs[0].shape
    row = pl.BlockSpec((nh, tm, e), lambda i: (0, i, 0))
    return pl.pallas_call(
        _combine_kernel, grid=(t // tm,), in_specs=[row] * (2 * len(outs)),
        out_specs=pl.BlockSpec((tm, nh * e), lambda i: (i, 0)),
        out_shape=jax.ShapeDtypeStruct((t, nh * e), BF16),
        compiler_params=_params("parallel"), name="combine_groups",
    )(*outs, *lses)


def _ssd_kernel(*refs, n_chunks, valid_len, has_state):
    if has_state:
        (z_ref, xbc_ref, dt_ref, h0_ref, cprev_ref, cw_ref, cb_ref, dtb_ref, alog_ref, dskip_ref, nw_ref, expand_ref,
         y_ref, hfin_ref, ctail_ref, ht_ref, xext_ref) = refs
    else:
        (z_ref, xbc_ref, dt_ref, cw_ref, cb_ref, dtb_ref, alog_ref, dskip_ref, nw_ref, expand_ref,
         y_ref, hfin_ref, ctail_ref, ht_ref, xext_ref) = refs
    c = pl.program_id(1)
    lc = SSM_CHUNK
    n_st = SSM_STATE
    gw = ht_ref.shape[1] // SSM_GROUPS
    d_inner = ht_ref.shape[1]
    heads_per_group = gw // SSM_HEAD_DIM
    n_tr = d_inner // LANES

    @pl.when(c == 0)
    def _():
        if has_state:
            xext_ref[0:SUBLANES, :] = cprev_ref[0]
            for k in range(n_tr):
                ht_ref[:, k * LANES:(k + 1) * LANES] = h0_ref[0, k * LANES:(k + 1) * LANES, :].T
        else:
            xext_ref[0:SUBLANES, :] = jnp.zeros((SUBLANES, xext_ref.shape[1]), F32)
            ht_ref[...] = jnp.zeros(ht_ref.shape, F32)

    xext_ref[SUBLANES:SUBLANES + lc, :] = xbc_ref[...]

    def conv_silu(c0, width):
        acc = cb_ref[:, c0:c0 + width]
        for s in range(CONV_W):
            acc = acc + cw_ref[CONV_W - 1 - s:CONV_W - s, c0:c0 + width] * xext_ref[pl.ds(SUBLANES - s, lc), c0:c0 + width]
        return acc * _sigmoid(acc)

    dtr = dt_ref[...] + dtb_ref[...]
    dt = jnp.maximum(dtr, 0.0) + jnp.log(1.0 + jnp.exp(-jnp.abs(dtr)))
    if valid_len < lc:
        trow = lax.broadcasted_iota(jnp.int32, dt.shape, 0)
        dt = jnp.where(trow < valid_len, dt, 0.0)
    a = dt * (-jnp.exp(alog_ref[...]))
    ti = lax.broadcasted_iota(jnp.int32, (lc, lc), 0)
    si = lax.broadcasted_iota(jnp.int32, (lc, lc), 1)
    tri = si <= ti
    f32_dot = lambda lhs, rhs: jnp.dot(lhs, rhs, preferred_element_type=F32)
    tri_b = tri.astype(BF16)
    acum = functools.reduce(lambda u, v: u + v, [f32_dot(tri_b, p) for p in _split3(a)])
    tri_t = (ti <= si).astype(BF16)
    acum_t = functools.reduce(lambda u, v: u + v, [f32_dot(p, tri_t) for p in _split3(a.T)])
    dt3, ac3 = _split3(dt), _split3(acum)
    lane = lax.broadcasted_iota(jnp.int32, (lc, LANES), 1)
    nt = (((1,), (1,)), ((), ()))

    for g in range(SSM_GROUPS):
        c0 = g * gw
        xs = conv_silu(c0, gw)
        bm = conv_silu(d_inner + g * n_st, n_st)
        cm = conv_silu(d_inner + SSM_GROUPS * n_st + g * n_st, n_st)
        expand = expand_ref[:, c0:c0 + gw]
        dt_x = functools.reduce(lambda u, v: u + v, [f32_dot(p, expand) for p in dt3])
        ac_x = functools.reduce(lambda u, v: u + v, [f32_dot(p, expand) for p in ac3])
        xdt = xs * dt_x
        xdt_b = xdt.astype(BF16)
        cm_b = cm.astype(BF16)
        cb = lax.dot_general(cm_b, bm.astype(BF16), nt, preferred_element_type=F32)
        h_old = ht_ref[:, c0:c0 + gw]
        y = jnp.dot(cm_b, h_old.astype(BF16), preferred_element_type=F32) * jnp.exp(ac_x)
        diag = []
        for k in range(gw // LANES):
            x_pair = xdt_b[:, k * LANES:(k + 1) * LANES]
            y_pair = None
            for hh in range(LANES // SSM_HEAD_DIM):
                h = g * heads_per_group + k * (LANES // SSM_HEAD_DIM) + hh
                seg = acum[:, h:h + 1] - acum_t[h:h + 1, :]
                lmat = jnp.exp(jnp.where(tri, seg, -jnp.inf))
                m_b = (cb * lmat).astype(BF16)
                in_head = (lane >= hh * SSM_HEAD_DIM) & (lane < (hh + 1) * SSM_HEAD_DIM)
                part = jnp.dot(m_b, jnp.where(in_head, x_pair, jnp.zeros_like(x_pair)), preferred_element_type=F32)
                y_pair = part if y_pair is None else y_pair + part
            diag.append(y_pair)
        y = y + jnp.concatenate(diag, axis=1) + dskip_ref[:, c0:c0 + gw] * xs
        zg = z_ref[:, c0:c0 + gw].astype(F32)
        y = y * (zg * _sigmoid(zg))
        y = y * lax.rsqrt(jnp.mean(y * y, axis=-1, keepdims=True) + RMS_EPS)
        y_ref[:, c0:c0 + gw] = (y * nw_ref[:, c0:c0 + gw]).astype(y_ref.dtype)
        a_last = ac_x[lc - 1:lc, :]
        xw = (xdt * jnp.exp(a_last - ac_x)).astype(BF16)
        ht_ref[:, c0:c0 + gw] = h_old * jnp.exp(a_last) + jnp.dot(bm.T.astype(BF16), xw, preferred_element_type=F32)

    xext_ref[0:SUBLANES, :] = xext_ref[lc:lc + SUBLANES, :]

    @pl.when(c == n_chunks - 1)
    def _():
        ctail_ref[0] = xext_ref[0:SUBLANES, :]
        for k in range(n_tr):
            hfin_ref[0, k * LANES:(k + 1) * LANES, :] = ht_ref[:, k * LANES:(k + 1) * LANES].T


def _ssd_call(z, xbc, dt_raw, n_seq, n_chunks, valid_len, state, conv_w, conv_b, dt_bias, a_log, d_skip, norm_w,
              out_rows, name):
    d_inner = z.shape[1]
    conv_dim = xbc.shape[1]
    n_heads = d_inner // SSM_HEAD_DIM
    lc = SSM_CHUNK
    pad_h = lambda v: jnp.pad(v.astype(F32), (0, LANES - n_heads)).reshape(1, LANES)
    rows = lambda w: pl.BlockSpec((lc, w), lambda b, c: (b * n_chunks + c, 0))
    const = lambda r, w: pl.BlockSpec((r, w), lambda b, c: (0, 0))
    per_seq = lambda r, w: pl.BlockSpec((1, r, w), lambda b, c: (b, 0, 0))
    args = [z, xbc, dt_raw]
    specs = [rows(d_inner), rows(conv_dim), rows(LANES)]
    if state is not None:
        args += list(state)
        specs += [per_seq(d_inner, SSM_STATE), per_seq(SUBLANES, conv_dim)]
    args += [conv_w, conv_b.reshape(1, conv_dim), pad_h(dt_bias), pad_h(a_log),
             jnp.repeat(d_skip.astype(F32), SSM_HEAD_DIM).reshape(1, d_inner), norm_w.reshape(1, d_inner),
             (jnp.arange(LANES)[:, None] == jnp.arange(d_inner)[None, :] // SSM_HEAD_DIM).astype(BF16)]
    specs += [const(CONV_W, conv_dim), const(1, conv_dim), const(1, LANES), const(1, LANES),
              const(1, d_inner), const(1, d_inner), const(LANES, d_inner)]
    kern = functools.partial(_ssd_kernel, n_chunks=n_chunks, valid_len=valid_len, has_state=state is not None)
    return pl.pallas_call(
        kern, grid=(n_seq, n_chunks), in_specs=specs,
        out_specs=[rows(d_inner), per_seq(d_inner, SSM_STATE), per_seq(SUBLANES, conv_dim)],
        out_shape=[jax.ShapeDtypeStruct((out_rows, d_inner), BF16),
                   jax.ShapeDtypeStruct((n_seq, d_inner, SSM_STATE), F32),
                   jax.ShapeDtypeStruct((n_seq, SUBLANES, conv_dim), F32)],
        scratch_shapes=[pltpu.VMEM((SSM_STATE, d_inner), F32), pltpu.VMEM((lc + 2 * SUBLANES, conv_dim), F32)],
        compiler_params=_params("parallel", "arbitrary"), name=name,
    )(*args)


def _mem_attn_kernel(q_ref, k_ref, v_ref, o_ref):
    scale = q_ref.shape[1] ** -0.5
    s = lax.dot_general(q_ref[...], k_ref[...].astype(BF16), (((1,), (1,)), ((), ())),
                        preferred_element_type=F32) * scale
    m = jnp.max(s, axis=-1, keepdims=True)
    p = jnp.exp(s - m)
    den = jnp.sum(p, axis=-1, keepdims=True)
    o = jnp.dot(p.astype(BF16), v_ref[...].astype(BF16), preferred_element_type=F32)
    o_ref[...] = (o / den).astype(o_ref.dtype)


def _mem_attn_call(q, mem_k, mem_v, n_seq, lq, tq, out_rows, name):
    d = q.shape[1]
    hd = d // MEM_HEADS
    n_mem = mem_k.shape[0] // n_seq
    nq = lq // tq
    qs = pl.BlockSpec((tq, hd), lambda b, h, i: (b * nq + i, h))
    ks = pl.BlockSpec((n_mem, hd), lambda b, h, i: (b, h))
    return pl.pallas_call(
        _mem_attn_kernel, grid=(n_seq, MEM_HEADS, nq), in_specs=[qs, ks, ks], out_specs=qs,
        out_shape=jax.ShapeDtypeStruct((out_rows, d), BF16),
        compiler_params=_params("parallel", "parallel", "parallel"), name=name,
    )(q, mem_k, mem_v)


def _router_kernel(x_ref, w_ref, b_ref, idx_ref, gate_ref):
    logits = jnp.dot(x_ref[...], w_ref[...], precision=HI, preferred_element_type=F32) + b_ref[...]
    lane = lax.broadcasted_iota(jnp.int32, logits.shape, 1)
    idx_out = jnp.zeros(logits.shape, jnp.int32)
    vals = []
    for k in range(TOP_K):
        m = jnp.max(logits, axis=-1, keepdims=True)
        pick = jnp.min(jnp.where(logits == m, lane, LANES), axis=-1, keepdims=True)
        idx_out = jnp.where(lane == k, pick, idx_out)
        logits = jnp.where(lane == pick, -jnp.inf, logits)
        vals.append(m)
    exps = [jnp.exp(v - vals[0]) for v in vals]
    tot = functools.reduce(lambda a, b: a + b, exps)
    gate_out = jnp.zeros(logits.shape, F32)
    for k in range(TOP_K):
        gate_out = jnp.where(lane == k, exps[k] / tot, gate_out)
    idx_ref[...] = idx_out
    gate_ref[...] = gate_out


def _router_call(x, w_router, b_router, tm=256):
    t, d = x.shape
    n_exp = w_router.shape[1]
    w = jnp.pad(w_router.astype(F32), ((0, 0), (0, LANES - n_exp)))
    b = jnp.pad(b_router.astype(F32), (0, LANES - n_exp), constant_values=-jnp.inf).reshape(1, LANES)
    out = pl.BlockSpec((tm, LANES), lambda i: (i, 0))
    return pl.pallas_call(
        _router_kernel, grid=(t // tm,),
        in_specs=[pl.BlockSpec((tm, d), lambda i: (i, 0)), pl.BlockSpec((d, LANES), lambda i: (0, 0)),
                  pl.BlockSpec((1, LANES), lambda i: (0, 0))],
        out_specs=[out, out],
        out_shape=[jax.ShapeDtypeStruct((t, LANES), jnp.int32), jax.ShapeDtypeStruct((t, LANES), F32)],
        compiler_params=_params("parallel"), name="router",
    )(x, w, b)


def _dispatch_kernel(pend_ref, slot_ref, x_ref, xg_hbm, sbuf, zbuf, sem, zsem, *, n_exp, block_rows, n_tiles):
    i = pl.program_id(0)
    s = i % 2
    tm = x_ref.shape[0]

    def zero_copy(e):
        start = pl.multiple_of(pend_ref[e] - block_rows, block_rows)
        return pltpu.make_async_copy(zbuf, xg_hbm.at[pl.ds(start, block_rows)], zsem)

    def has_rows(e):
        return pend_ref[e] > (pend_ref[e - 1] if e else 0)

    def wait_rows(s_):
        for _ in range(TOP_K):
            pltpu.make_async_copy(sbuf.at[s_], xg_hbm.at[pl.ds(0, tm)], sem.at[s_]).wait()

    @pl.when(i == 0)
    def _():
        zbuf[...] = jnp.zeros(zbuf.shape, F32)
        for e in range(n_exp):
            pl.when(has_rows(e))(lambda e=e: zero_copy(e).start())
        for e in range(n_exp):
            pl.when(has_rows(e))(lambda e=e: zero_copy(e).wait())

    @pl.when(i >= 2)
    def _():
        wait_rows(s)

    sbuf[s] = x_ref[...]
    for r in range(tm):
        for k in range(TOP_K):
            pltpu.make_async_copy(sbuf.at[s, pl.ds(r, 1)], xg_hbm.at[pl.ds(slot_ref[0, 0, r * TOP_K + k], 1)],
                                  sem.at[s]).start()

    @pl.when(i == n_tiles - 1)
    def _():
        wait_rows(s)
        if n_tiles > 1:
            wait_rows(1 - s)


def _dispatch_call(x, slot, pad_end, n_slots, tm=256):
    t, d = x.shape
    n_tiles = t // tm
    grid_spec = pltpu.PrefetchScalarGridSpec(
        num_scalar_prefetch=1, grid=(n_tiles,),
        in_specs=[pl.BlockSpec((1, 1, tm * TOP_K), lambda i, pe: (i, 0, 0), memory_space=pltpu.SMEM),
                  pl.BlockSpec((tm, d), lambda i, pe: (i, 0))],
        out_specs=pl.BlockSpec(memory_space=pl.ANY),
        scratch_shapes=[pltpu.VMEM((2, tm, d), F32), pltpu.VMEM((MOE_TM, d), F32),
                        pltpu.SemaphoreType.DMA((2,)), pltpu.SemaphoreType.DMA(())],
    )
    kern = functools.partial(_dispatch_kernel, n_exp=pad_end.shape[0], block_rows=MOE_TM, n_tiles=n_tiles)
    return pl.pallas_call(
        kern, grid_spec=grid_spec, out_shape=jax.ShapeDtypeStruct((n_slots, d), F32),
        compiler_params=_params("arbitrary"), name="moe_dispatch",
    )(pad_end, slot.reshape(n_tiles, 1, tm * TOP_K), x)


def _moe_kernel(nused_ref, bexp_ref, x_ref, wg_ref, wu_ref, wd_ref, bg_ref, bu_ref, bd_ref, y_ref, xb, *, nj):
    i = pl.program_id(0)
    j = pl.program_id(1)

    @pl.when(i < nused_ref[0])
    def _():
        @pl.when(j == 0)
        def _():
            xb[...] = x_ref[...].astype(BF16)
            y_ref[...] = jnp.broadcast_to(bd_ref[0], y_ref.shape)

        x = xb[...]
        gate = jnp.minimum(jnp.dot(x, wg_ref[0], preferred_element_type=F32) + bg_ref[0], SWIGLU_LIMIT)
        up = jnp.clip(jnp.dot(x, wu_ref[0], preferred_element_type=F32) + bu_ref[0], -SWIGLU_LIMIT, SWIGLU_LIMIT)
        act = (up + 1.0) * gate * _sigmoid(gate * SWIGLU_ALPHA)
        y_ref[...] += jnp.dot(act.astype(BF16), wd_ref[0], preferred_element_type=F32)


def _moe_call(x, top_idx, w_gate_up, b_gate_up, w_down, b_down):
    t, d = x.shape
    n_exp, _, two_h = w_gate_up.shape
    d_exp = two_h // 2
    tm, tn = MOE_TM, MOE_TN
    nj = d_exp // tn
    n_assign = t * TOP_K
    n_blocks = n_assign // tm + n_exp
    n_slots = n_blocks * tm

    flat_e = top_idx.reshape(n_assign)
    onehot = (flat_e[:, None] == jnp.arange(n_exp, dtype=jnp.int32)[None, :]).astype(jnp.int32)
    counts = jnp.sum(onehot, axis=0)
    rank = jnp.take_along_axis(jnp.cumsum(onehot, axis=0), flat_e[:, None], axis=1)[:, 0] - 1
    padded = (counts + tm - 1) // tm * tm
    pad_end = jnp.cumsum(padded)
    slot = (pad_end - padded)[flat_e] + rank
    n_used = (pad_end[-1] // tm).astype(jnp.int32)
    blk = jnp.minimum(jnp.arange(n_blocks, dtype=jnp.int32), n_used - 1) * tm
    block_expert = jnp.minimum(jnp.sum((pad_end[None, :] <= blk[:, None]).astype(jnp.int32), axis=1), n_exp - 1)
    xg = _dispatch_call(x, slot, pad_end.astype(jnp.int32), n_slots)

    live = lambda i, nu: i < nu[0]
    rows = pl.BlockSpec((tm, d), lambda i, j, nu, be: (jnp.minimum(i, nu[0] - 1), 0))
    grid_spec = pltpu.PrefetchScalarGridSpec(
        num_scalar_prefetch=2, grid=(n_blocks, nj),
        in_specs=[
            rows,
            pl.BlockSpec((1, d, tn), lambda i, j, nu, be: (be[i], 0, jnp.where(live(i, nu), j, nj - 1))),
            pl.BlockSpec((1, d, tn), lambda i, j, nu, be: (be[i], 0, nj + jnp.where(live(i, nu), j, nj - 1))),
            pl.BlockSpec((1, tn, d), lambda i, j, nu, be: (be[i], jnp.where(live(i, nu), j, nj - 1), 0)),
            pl.BlockSpec((1, 1, tn), lambda i, j, nu, be: (be[i], 0, jnp.where(live(i, nu), j, nj - 1))),
            pl.BlockSpec((1, 1, tn), lambda i, j, nu, be: (be[i], 0, nj + jnp.where(live(i, nu), j, nj - 1))),
            pl.BlockSpec((1, 1, d), lambda i, j, nu, be: (be[i], 0, 0)),
        ],
        out_specs=rows,
        scratch_shapes=[pltpu.VMEM((tm, d), BF16)],
    )
    bgu = b_gate_up.astype(F32).reshape(n_exp, 1, two_h)
    y_slots = pl.pallas_call(
        functools.partial(_moe_kernel, nj=nj), grid_spec=grid_spec,
        out_shape=jax.ShapeDtypeStruct((n_slots, d), F32),
        compiler_params=_params("arbitrary", "arbitrary"), name="moe_experts",
    )(n_used.reshape(1), block_expert, xg, w_gate_up, w_gate_up, w_down, bgu, bgu,
      b_down.astype(F32).reshape(n_exp, 1, d))
    return y_slots, slot


def _moe_out_kernel(slotc_ref, slotn_ref, y_hbm, gate_ref, h_ref, g_ref, b_ref, head_ref, tail_ref, ybuf, sem,
                    *, n_head, n_tiles):
    i = pl.program_id(0)
    s = i % 2
    tm = h_ref.shape[0]

    def fetch(tab_ref, s_):
        for r in range(TOP_K * tm):
            pltpu.make_async_copy(y_hbm.at[pl.ds(tab_ref[0, 0, r], 1)], ybuf.at[s_, pl.ds(r, 1)], sem.at[s_]).start()

    @pl.when(i == 0)
    def _():
        fetch(slotc_ref, 0)

    @pl.when(i + 1 < n_tiles)
    def _():
        fetch(slotn_ref, 1 - s)

    pltpu.make_async_copy(y_hbm.at[pl.ds(0, TOP_K * tm)], ybuf.at[s], sem.at[s]).wait()
    gates = gate_ref[...]
    y = gates[:, 0:1] * ybuf[s, 0:tm]
    for k in range(1, TOP_K):
        y = y + gates[:, k:k + 1] * ybuf[s, k * tm:(k + 1) * tm]
    out = _layer_norm_rows(DN_ALPHA * h_ref[...] + y, g_ref[...], b_ref[...])

    @pl.when(pl.program_id(0) < n_head)
    def _():
        head_ref[...] = out

    @pl.when(pl.program_id(0) >= n_head)
    def _():
        tail_ref[...] = out


def _moe_out_call(y_slots, slot, gates, h, g, b, n_head_rows, tm=256):
    t, d = h.shape
    nb = t // tm
    n_head = n_head_rows // tm
    row = pl.BlockSpec((tm, d), lambda i: (i, 0))
    vec = pl.BlockSpec((1, d), lambda i: (0, 0))
    table = jnp.transpose(slot.reshape(nb, tm, TOP_K), (0, 2, 1)).reshape(nb, 1, TOP_K * tm)
    smem = lambda imap: pl.BlockSpec((1, 1, TOP_K * tm), imap, memory_space=pltpu.SMEM)
    return pl.pallas_call(
        functools.partial(_moe_out_kernel, n_head=n_head, n_tiles=nb), grid=(nb,),
        in_specs=[smem(lambda i: (i, 0, 0)), smem(lambda i: (jnp.minimum(i + 1, nb - 1), 0, 0)),
                  pl.BlockSpec(memory_space=pl.ANY), pl.BlockSpec((tm, LANES), lambda i: (i, 0)), row, vec, vec],
        out_specs=[pl.BlockSpec((tm, d), lambda i: (jnp.minimum(i, n_head - 1), 0)),
                   pl.BlockSpec((tm, d), lambda i: (jnp.maximum(i - n_head, 0), 0))],
        out_shape=[jax.ShapeDtypeStruct((n_head_rows, d), F32), jax.ShapeDtypeStruct((t - n_head_rows, d), F32)],
        scratch_shapes=[pltpu.VMEM((2, TOP_K * tm, d), F32), pltpu.SemaphoreType.DMA((2,))],
        compiler_params=_params("arbitrary"), name="moe_combine_ln",
    )(table, table, y_slots, gates, h, g.reshape(1, d), b.reshape(1, d))


def kernel(x_prompt, x_sample, state_conv, state_ssm, cache_k_w128, cache_v_w128, cache_k_w512, cache_v_w512, cache_k_w2048, cache_v_w2048, cache_mem_k, cache_mem_v, mem_prompt, ln_in_g, ln_in_b, w_in, conv_w, conv_b, dt_bias, a_log, d_skip, ssm_norm_w, w_branch_ssm, w_branch_att, w_mix_out, ln1_g, ln1_b, w_mem_q, w_mem_k, w_mem_v, w_mem_o, ln2_g, ln2_b, w_router, b_router, w_gate_up, b_gate_up, w_down, b_down, ln3_g, ln3_b):
    assert w_in.shape[0] == DEPTH
    bp, lp, d = x_prompt.shape
    bs, ls, _ = x_sample.shape
    n_p, n_s = bp * lp, bs * ls
    t_real = n_p + n_s
    t_pad = -(-t_real // ROW_TILE) * ROW_TILE
    d_inner = ssm_norm_w.shape[1]
    conv_dim = conv_w.shape[2]
    n_heads = d_inner // SSM_HEAD_DIM
    gw = HEADS_PER_GROUP * ATT_HEAD_DIM
    att_w = len(DIL_CONFIGS) * gw
    n_mem = mem_prompt.shape[1]
    lc = SSM_CHUNK
    assert lp % lc == 0 and ls <= lc and n_s % SUBLANES == 0

    def sample_rows(a):
        return a[n_p:n_p + n_s]

    def with_sample_rows(a, rows):
        tail = jnp.concatenate([rows.astype(a.dtype), jnp.zeros((t_pad - t_real, a.shape[1]), a.dtype)], axis=0)
        return lax.dynamic_update_slice(a, tail, (n_p, 0))

    def with_sample_heads(a, rows):
        nh, _, e = a.shape
        rows = jnp.transpose(rows.reshape(n_s, nh, e), (1, 0, 2)).astype(a.dtype)
        tail = jnp.concatenate([rows, jnp.zeros((nh, t_pad - t_real, e), a.dtype)], axis=1)
        return lax.dynamic_update_slice(a, tail, (0, n_p, 0))

    x_tail = jnp.concatenate([x_sample.reshape(n_s, d), jnp.zeros((t_pad - t_real, d), x_sample.dtype)], axis=0)
    h0, h0_b = _ln_call(x_prompt.reshape(n_p, d), x_tail, ln_in_g, ln_in_b)

    o_z, o_xbc, o_dt, o_qkv, o_g = 0, d_inner, d_inner + conv_dim, d_inner + conv_dim + n_heads, \
        d_inner + conv_dim + n_heads + 3 * att_w
    w_cols = lambda a, b: w_in[0, :, a:b].astype(BF16)
    n_exp, _, two_h = w_gate_up.shape[1:]
    wgu_src = w_gate_up[0].reshape(n_exp * d, two_h)
    wd_src = w_down[0].reshape(n_exp * (two_h // 2), d)
    steps = lambda n_cols, tn: (n_cols // tn) * (t_pad // 512)
    gu_rows, wd_rows = 128, 256
    gu_chunks, wd_chunks = wgu_src.shape[0] // gu_rows, wd_src.shape[0] // wd_rows
    n_a = min(steps(conv_dim, 1024), gu_chunks)
    n_b = min(steps(d_inner, 1024), gu_chunks - n_a)
    n_c = min(steps(2 * d, 1024), wd_chunks)
    xbc, wgu_b = _mm_call(h0_b, w_cols(o_xbc, o_dt), F32, 512, 1024, "in_xbc", (wgu_src, None, 0, n_a, gu_rows))
    if n_b:
        z, wgu_b = _mm_call(h0_b, w_cols(o_z, o_xbc), BF16, 512, 1024, "in_z", (wgu_src, wgu_b, n_a, n_b, gu_rows))
    else:
        z = _mm_call(h0_b, w_cols(o_z, o_xbc), BF16, 512, 1024, "in_z")
    if gu_chunks - n_a - n_b:
        wgu_b = _cast_call(wgu_src, wgu_b, n_a + n_b, gu_chunks - n_a - n_b, gu_rows)
    w_dt = jnp.pad(w_cols(o_dt, o_qkv), ((0, 0), (0, LANES - n_heads)))
    dt_raw = _mm_call(h0_b, w_dt, F32, 512, LANES, "in_dt")
    qkv = _mm_heads_call(h0_b, w_cols(o_qkv, o_g), F32, 512, att_w, "in_qkv")
    gates, wd_b = _mm_call(h0_b, w_cols(o_g, w_in.shape[2]), BF16, 512, 1024, "in_gates",
                           (wd_src, None, 0, n_c, wd_rows))
    if wd_chunks - n_c:
        wd_b = _cast_call(wd_src, wd_b, n_c, wd_chunks - n_c, wd_rows)
    wgu_b = wgu_b.reshape(n_exp, d, two_h)
    wd_b = wd_b.reshape(n_exp, two_h // 2, d)

    ssd_w = (conv_w[0], conv_b[0], dt_bias[0], a_log[0], d_skip[0], ssm_norm_w[0])
    y_ssm, ssm_p, conv_tail = _ssd_call(z, xbc, dt_raw, bp, lp // lc, lc, None, *ssd_w, out_rows=t_pad,
                                        name="ssd_prompt")

    def pad_seq(a):
        return jnp.pad(sample_rows(a).reshape(bs, ls, a.shape[1]), ((0, 0), (0, lc - ls), (0, 0))).reshape(bs * lc, a.shape[1])

    conv_prev = jnp.pad(state_conv[0], ((0, 0), (SUBLANES - (CONV_W - 1), 0), (0, 0)))
    y_s, ssm_s, _ = _ssd_call(pad_seq(z), pad_seq(xbc), pad_seq(dt_raw), bs, 1, ls,
                              (state_ssm[0].reshape(bs, d_inner, SSM_STATE), conv_prev), *ssd_w,
                              out_rows=bs * lc, name="ssd_sample")
    y_ssm = with_sample_rows(y_ssm, y_s.reshape(bs, lc, d_inner)[:, :ls].reshape(n_s, d_inner))

    pos = jnp.concatenate([jnp.tile(jnp.arange(lp, dtype=jnp.int32), bp),
                           jnp.tile(PAST_LEN + jnp.arange(ls, dtype=jnp.int32), bs),
                           jnp.zeros((t_pad - t_real,), jnp.int32)])
    cos_t, sin_t = _rotary_tables(pos)
    split = _rotary_call(qkv, cos_t, sin_t, n_p)
    n_grp = len(DIL_CONFIGS)
    k_tok, v_tok, q_tail, q_view, k_view, v_view = [split[a * n_grp:(a + 1) * n_grp] for a in range(6)]
    caches = ((cache_k_w128, cache_v_w128), (cache_k_w512, cache_v_w512), (cache_k_w2048, cache_v_w2048))
    outs, lses, kv_p, kv_s = [], [], [], []
    heads = lambda a: a.reshape(bs, ls, HEADS_PER_GROUP, ATT_HEAD_DIM)
    for g, (window, dil) in enumerate(DIL_CONFIGS):
        o_p, l_p = _band_attn_call(q_view[g], k_view[g], v_view[g], bp, lp, window, dil, g)
        k_new, v_new = heads(sample_rows(k_tok[g])), heads(sample_rows(v_tok[g]))
        o_s, l_s = _sample_attn_call(heads(q_tail[g][:n_s]), k_new, v_new, caches[g][0][0], caches[g][1][0],
                                     window, dil, g)
        outs.append(with_sample_heads(o_p, o_s))
        lses.append(with_sample_heads(l_p, l_s))
        keep = min(window, lp)
        for a in (k_tok[g], v_tok[g]):
            kv_p.append(a[:n_p].reshape(bp, lp, HEADS_PER_GROUP, ATT_HEAD_DIM)[:, lp - keep:][None])
        kv_s += [k_new[None], v_new[None]]
    att = _combine_call(outs, lses)

    merged = _branch_call(y_ssm, att, gates, w_branch_ssm[0].astype(BF16), w_branch_att[0].astype(BF16))
    h1, h1_b = _mm_res_ln_call(merged, w_mix_out[0].astype(BF16), h0, ln1_g[0], ln1_b[0], "mix_out_ln1")

    mem_b = mem_prompt.reshape(bp * n_mem, d).astype(BF16)
    mem_k_p = _mm_call(mem_b, w_mem_k[0].astype(BF16), F32, 512, 1024, "mem_k")
    mem_v_p = _mm_call(mem_b, w_mem_v[0].astype(BF16), F32, 512, 1024, "mem_v")
    q_mem = _mm_call(h1_b, w_mem_q[0].astype(BF16), BF16, 512, 1024, "mem_q")
    o_mem = _mem_attn_call(q_mem, mem_k_p, mem_v_p, bp, lp, 512, t_pad, "mem_attn_prompt")
    q_s = sample_rows(q_mem).reshape(bs, ls, MEM_HEADS, d // MEM_HEADS).astype(F32)
    o_mem_s = _decode_mem_attn_call(q_s, cache_mem_k[0], cache_mem_v[0])
    o_mem = with_sample_rows(o_mem, o_mem_s.reshape(n_s, d))
    h2, _ = _mm_res_ln_call(o_mem, w_mem_o[0].astype(BF16), h1, ln2_g[0], ln2_b[0], "mem_o_ln2")

    idx_t, gate_t = _router_call(h2, w_router[0], b_router[0])
    y_slots, slot = _moe_call(h2, idx_t[:, :TOP_K], wgu_b, b_gate_up[0], wd_b, b_down[0])
    h3_head, h3_tail = _moe_out_call(y_slots, slot, gate_t, h2, ln3_g[0], ln3_b[0], n_p)

    y_prompt = h3_head.reshape(bp, lp, d)
    y_sample = h3_tail[:n_s].reshape(bs, ls, d)
    conv_p = conv_tail[:, SUBLANES - (CONV_W - 1):][None]
    xp_s = jnp.concatenate([state_conv[0].astype(xbc.dtype), sample_rows(xbc).reshape(bs, ls, conv_dim)], axis=1)
    conv_s = xp_s[:, -(CONV_W - 1):][None]
    state_shape = (n_heads, SSM_HEAD_DIM, SSM_STATE)
    ssm_p = ssm_p.reshape(1, bp, *state_shape)
    ssm_s = ssm_s.reshape(1, bs, *state_shape)
    mem_shape = (1, bp, n_mem, MEM_HEADS, d // MEM_HEADS)
    return (y_prompt, y_sample, conv_p, ssm_p, *kv_p, mem_k_p.reshape(mem_shape), mem_v_p.reshape(mem_shape),
            conv_s, ssm_s, *kv_s)
```

```python
import functools
import math

import jax
import jax.numpy as jnp
from jax import lax
from jax.experimental import pallas as pl
from jax.experimental.pallas import tpu as pltpu

F32 = jnp.float32
BF16 = jnp.bfloat16

PAST_LEN = 16384
SSM_HEAD_DIM = 64
SSM_GROUPS = 8
SSM_STATE = 128
CONV_W = 4
SSM_CHUNK = 128
RMS_EPS = 1e-5
DIL_CONFIGS = ((128, 1), (512, 4), (2048, 16))
HEADS_PER_GROUP = 4
ATT_HEAD_DIM = 128
ATT_BLOCK = 128
ROT_DIM = ATT_HEAD_DIM // 4
ROPE_THETA = 500000.0
MEM_HEADS = 4
TOP_K = 4
SWIGLU_LIMIT = 7.0
SWIGLU_ALPHA = 1.702
LN_EPS = 1e-5
DEPTH = 1
DN_ALPHA = (2.0 * DEPTH) ** 0.25

LANES = 128
SUBLANES = 8
ROW_TILE = 512
MOE_TM = 512
MOE_TN = 1024
VMEM_LIMIT = 56 * 1024 * 1024
HI = lax.Precision.HIGHEST


def _params(*sem):
    return pltpu.CompilerParams(dimension_semantics=sem, vmem_limit_bytes=VMEM_LIMIT)


def _sigmoid(x):
    return 0.5 * jnp.tanh(0.5 * x) + 0.5


def _split3(x):
    hi = x.astype(BF16)
    r1 = x - hi.astype(F32)
    mid = r1.astype(BF16)
    lo = (r1 - mid.astype(F32)).astype(BF16)
    return hi, mid, lo


def _layer_norm_rows(x, g, b):
    mu = jnp.mean(x, axis=-1, keepdims=True)
    xc = x - mu
    var = jnp.mean(xc * xc, axis=-1, keepdims=True)
    return xc * lax.rsqrt(var + LN_EPS) * g + b


def _ln_kernel(xa_ref, xb_ref, g_ref, b_ref, of_ref, ob_ref, *, n_head):
    def emit(x_ref):
        y = _layer_norm_rows(x_ref[...], g_ref[...], b_ref[...])
        of_ref[...] = y
        ob_ref[...] = y.astype(BF16)

    pl.when(pl.program_id(0) < n_head)(lambda: emit(xa_ref))
    pl.when(pl.program_id(0) >= n_head)(lambda: emit(xb_ref))


def _ln_call(x_head, x_tail, g, b, tm=256):
    d = x_head.shape[1]
    n_head, n_tail = x_head.shape[0] // tm, x_tail.shape[0] // tm
    t = (n_head + n_tail) * tm
    row = pl.BlockSpec((tm, d), lambda i: (i, 0))
    vec = pl.BlockSpec((1, d), lambda i: (0, 0))
    return pl.pallas_call(
        functools.partial(_ln_kernel, n_head=n_head), grid=(n_head + n_tail,),
        in_specs=[pl.BlockSpec((tm, d), lambda i: (jnp.minimum(i, n_head - 1), 0)),
                  pl.BlockSpec((tm, d), lambda i: (jnp.maximum(i - n_head, 0), 0)), vec, vec],
        out_specs=[row, row],
        out_shape=[jax.ShapeDtypeStruct((t, d), F32), jax.ShapeDtypeStruct((t, d), BF16)],
        compiler_params=_params("arbitrary"), name="ln_in",
    )(x_head, x_tail, g.reshape(1, d), b.reshape(1, d))


def _mm_kernel(a_ref, w_ref, *rest, n_cast):
    o_ref = rest[-2] if n_cast else rest[-1]
    if n_cast:
        src_ref, dst_ref = rest[0], rest[-1]
        step = pl.program_id(0) * pl.num_programs(1) + pl.program_id(1)

        @pl.when(step < n_cast)
        def _():
            dst_ref[...] = src_ref[...].astype(dst_ref.dtype)

    o_ref[...] = jnp.dot(a_ref[...], w_ref[...], preferred_element_type=F32).astype(o_ref.dtype)


def _mm_call(a, w, out_dtype, tm, tn, name, cast=None):
    m, k = a.shape
    n = w.shape[1]
    tm = min(tm, m)
    ni = m // tm
    args = [a, w]
    in_specs = [pl.BlockSpec((tm, k), lambda j, i: (i, 0)), pl.BlockSpec((k, tn), lambda j, i: (0, j))]
    out_specs = pl.BlockSpec((tm, tn), lambda j, i: (i, j))
    out_shape = jax.ShapeDtypeStruct((m, n), out_dtype)
    aliases = {}
    n_cast = 0
    if cast is not None:
        src, dst, chunk0, n_cast, rows = cast
        assert 0 < n_cast <= (n // tn) * ni
        chunk = pl.BlockSpec((rows, src.shape[1]), lambda j, i: (chunk0 + jnp.minimum(j * ni + i, n_cast - 1), 0))
        args.append(src)
        in_specs.append(chunk)
        if dst is not None:
            aliases = {len(args): 1}
            args.append(dst)
            in_specs.append(pl.BlockSpec(memory_space=pl.ANY))
        out_specs = [out_specs, chunk]
        out_shape = [out_shape, jax.ShapeDtypeStruct(src.shape, BF16)]
    return pl.pallas_call(
        functools.partial(_mm_kernel, n_cast=n_cast), grid=(n // tn, ni),
        in_specs=in_specs, out_specs=out_specs, out_shape=out_shape, input_output_aliases=aliases,
        compiler_params=_params("arbitrary", "arbitrary") if n_cast else _params("parallel", "parallel"), name=name,
    )(*args)


def _cast_kernel(src_ref, _, dst_ref):
    dst_ref[...] = src_ref[...].astype(dst_ref.dtype)


def _cast_call(src, dst, chunk0, n_chunks, rows):
    chunk = pl.BlockSpec((rows, src.shape[1]), lambda i: (chunk0 + i, 0))
    return pl.pallas_call(
        _cast_kernel, grid=(n_chunks,), in_specs=[chunk, pl.BlockSpec(memory_space=pl.ANY)], out_specs=chunk,
        out_shape=jax.ShapeDtypeStruct(src.shape, BF16), input_output_aliases={1: 0},
        compiler_params=_params("parallel"), name="cast_rest",
    )(src, dst)


def _mm_heads_kernel(a_ref, w_ref, o_ref):
    res = jnp.dot(a_ref[...], w_ref[...], preferred_element_type=F32)
    for h in range(o_ref.shape[0]):
        o_ref[h] = res[:, h * LANES:(h + 1) * LANES].astype(o_ref.dtype)


def _mm_heads_call(a, w, out_dtype, tm, tn, name):
    m, k = a.shape
    n = w.shape[1]
    return pl.pallas_call(
        _mm_heads_kernel, grid=(n // tn, m // tm),
        in_specs=[pl.BlockSpec((tm, k), lambda j, i: (i, 0)), pl.BlockSpec((k, tn), lambda j, i: (0, j))],
        out_specs=pl.BlockSpec((tn // LANES, tm, LANES), lambda j, i: (j, i, 0)),
        out_shape=jax.ShapeDtypeStruct((n // LANES, m, LANES), out_dtype),
        compiler_params=_params("parallel", "parallel"), name=name,
    )(a, w)


def _mm_res_ln_kernel(a_ref, w_ref, h_ref, g_ref, b_ref, of_ref, ob_ref):
    mix = jnp.dot(a_ref[...], w_ref[...], preferred_element_type=F32)
    y = _layer_norm_rows(DN_ALPHA * h_ref[...] + mix, g_ref[...], b_ref[...])
    of_ref[...] = y
    ob_ref[...] = y.astype(BF16)


def _mm_res_ln_call(a, w, h, g, b, name, tm=256):
    m, k = a.shape
    d = w.shape[1]
    row = pl.BlockSpec((tm, d), lambda i: (i, 0))
    vec = pl.BlockSpec((1, d), lambda i: (0, 0))
    return pl.pallas_call(
        _mm_res_ln_kernel, grid=(m // tm,),
        in_specs=[pl.BlockSpec((tm, k), lambda i: (i, 0)), pl.BlockSpec((k, d), lambda i: (0, 0)), row, vec, vec],
        out_specs=[row, row],
        out_shape=[jax.ShapeDtypeStruct((m, d), F32), jax.ShapeDtypeStruct((m, d), BF16)],
        compiler_params=_params("parallel"), name=name,
    )(a, w, h, g.reshape(1, d), b.reshape(1, d))


def _branch_kernel(y_ref, att_ref, gs_ref, ga_ref, ws_ref, wa_ref, o_ref):
    bs = jnp.dot(y_ref[...], ws_ref[...], preferred_element_type=F32)
    ba = jnp.dot(att_ref[...], wa_ref[...], preferred_element_type=F32)
    merged = _sigmoid(gs_ref[...].astype(F32)) * bs + _sigmoid(ga_ref[...].astype(F32)) * ba
    o_ref[...] = merged.astype(o_ref.dtype)


def _branch_call(y_ssm, att, gates, ws, wa, tm=512, tn=1024):
    m, ks = y_ssm.shape
    ka = att.shape[1]
    d = ws.shape[1]
    nj = d // tn
    return pl.pallas_call(
        _branch_kernel, grid=(nj, m // tm),
        in_specs=[pl.BlockSpec((tm, ks), lambda j, i: (i, 0)),
                  pl.BlockSpec((tm, ka), lambda j, i: (i, 0)),
                  pl.BlockSpec((tm, tn), lambda j, i: (i, j)),
                  pl.BlockSpec((tm, tn), lambda j, i: (i, nj + j)),
                  pl.BlockSpec((ks, tn), lambda j, i: (0, j)),
                  pl.BlockSpec((ka, tn), lambda j, i: (0, j))],
        out_specs=pl.BlockSpec((tm, tn), lambda j, i: (i, j)),
        out_shape=jax.ShapeDtypeStruct((m, d), BF16),
        compiler_params=_params("parallel", "parallel"), name="branch_merge",
    )(y_ssm, att, gates, gates, ws, wa)


def _rotary_kernel(qkv_ref, cos_ref, sin_ref, perm_ref, *out_refs, n_head):
    n_grp = len(DIL_CONFIGS)
    kt_refs, vt_refs, qt_refs, qv_refs, kv_refs, vv_refs = [out_refs[a * n_grp:(a + 1) * n_grp] for a in range(6)]
    in_tail = pl.program_id(0) >= n_head
    tm = qkv_ref.shape[1]
    half = ROT_DIM // 2
    gw = HEADS_PER_GROUP * ATT_HEAD_DIM
    n_att = n_grp * HEADS_PER_GROUP

    def rot(x, cos, sin):
        lane = lax.broadcasted_iota(jnp.int32, x.shape, 1)
        partner = jnp.where(lane < half, pltpu.roll(x, ATT_HEAD_DIM - half, 1), pltpu.roll(x, half, 1))
        return x * cos + partner * sin

    def rot_view(x, cos, sin):
        partner = jnp.dot(x.astype(BF16), perm_ref[...], preferred_element_type=F32)
        return (x * cos + partner * sin).astype(BF16)

    for g, (_, dil) in enumerate(DIL_CONFIGS):
        for h in range(HEADS_PER_GROUP):
            hd = g * HEADS_PER_GROUP + h
            tok = slice(h * ATT_HEAD_DIM, (h + 1) * ATT_HEAD_DIM)
            kt_refs[g][:, h, :] = rot(qkv_ref[n_att + hd], cos_ref[...], sin_ref[...])
            vt_refs[g][:, h, :] = qkv_ref[2 * n_att + hd]
        for r in range(dil):
            rows = pl.ds(r, tm // dil, stride=dil) if dil > 1 else slice(None)
            cos = cos_ref[rows, :]
            sin = sin_ref[rows, :]
            for h in range(HEADS_PER_GROUP):
                hd = g * HEADS_PER_GROUP + h
                view = slice(r * gw + h * ATT_HEAD_DIM, r * gw + (h + 1) * ATT_HEAD_DIM)
                qv_refs[g][:, view] = rot_view(qkv_ref[hd, rows, :], cos, sin)
                kv_refs[g][:, view] = rot_view(qkv_ref[n_att + hd, rows, :], cos, sin)
                vv_refs[g][:, view] = qkv_ref[2 * n_att + hd, rows, :].astype(BF16)

    @pl.when(in_tail)
    def _():
        for g in range(n_grp):
            for h in range(HEADS_PER_GROUP):
                tok = slice(h * ATT_HEAD_DIM, (h + 1) * ATT_HEAD_DIM)
                qt_refs[g][:, tok] = rot(qkv_ref[g * HEADS_PER_GROUP + h], cos_ref[...], sin_ref[...])


def _rotary_call(qkv, cos_t, sin_t, n_head_rows, tm=256):
    n_slab, t, _ = qkv.shape
    gw = HEADS_PER_GROUP * ATT_HEAD_DIM
    n_grp = len(DIL_CONFIGS)
    n_head = n_head_rows // tm
    row = pl.BlockSpec((tm, HEADS_PER_GROUP, ATT_HEAD_DIM), lambda i: (i, 0, 0))
    tail = pl.BlockSpec((tm, gw), lambda i: (jnp.maximum(i - n_head, 0), 0))
    tab = pl.BlockSpec((tm, ATT_HEAD_DIM), lambda i: (i, 0))
    views = [pl.BlockSpec((tm // dil, dil * gw), lambda i: (i, 0)) for _, dil in DIL_CONFIGS]
    view_shapes = [jax.ShapeDtypeStruct((t // dil, dil * gw), BF16) for _, dil in DIL_CONFIGS]
    li = jnp.arange(ATT_HEAD_DIM)[:, None]
    lj = jnp.arange(ATT_HEAD_DIM)[None, :]
    half = ROT_DIM // 2
    perm = (((lj < half) & (li == lj + half)) | ((lj >= half) & (lj < ROT_DIM) & (li == lj - half))).astype(BF16)
    return pl.pallas_call(
        functools.partial(_rotary_kernel, n_head=n_head), grid=(t // tm,),
        in_specs=[pl.BlockSpec((n_slab, tm, ATT_HEAD_DIM), lambda i: (0, i, 0)), tab, tab,
                  pl.BlockSpec((ATT_HEAD_DIM, ATT_HEAD_DIM), lambda i: (0, 0))],
        out_specs=[row] * (2 * n_grp) + [tail] * n_grp + views * 3,
        out_shape=([jax.ShapeDtypeStruct((t, HEADS_PER_GROUP, ATT_HEAD_DIM), F32)] * (2 * n_grp)
                   + [jax.ShapeDtypeStruct((t - n_head_rows, gw), F32)] * n_grp + view_shapes * 3),
        compiler_params=_params("arbitrary"), name="rotary_split",
    )(qkv, cos_t, sin_t, perm)


def _rotary_tables(pos):
    half = ROT_DIM // 2
    inv_freq = jnp.exp(-math.log(ROPE_THETA) * jnp.arange(half, dtype=F32) * (2.0 / ROT_DIM))
    ang = pos.astype(F32)[:, None] * inv_freq[None, :]
    cos, sin = jnp.cos(ang), jnp.sin(ang)
    rest = ATT_HEAD_DIM - ROT_DIM
    ones = jnp.ones((pos.shape[0], rest), F32)
    cos_t = jnp.concatenate([cos, cos, ones], axis=1)
    sin_t = jnp.concatenate([-sin, sin, jnp.zeros_like(ones)], axis=1)
    return cos_t, sin_t


def _band_attn_kernel(q_ref, kc_ref, kp_ref, vc_ref, vp_ref, o_ref, lse_ref, *, dil, span, has_prev):
    n = pl.program_id(1)
    scale = ATT_HEAD_DIM ** -0.5
    qi = lax.broadcasted_iota(jnp.int32, (ATT_BLOCK, ATT_BLOCK), 0)
    ki = lax.broadcasted_iota(jnp.int32, (ATT_BLOCK, ATT_BLOCK), 1)
    rel_c = qi - ki
    valid_c = (rel_c >= 0) & (rel_c <= span)
    rel_p = rel_c + ATT_BLOCK
    valid_p = (rel_p <= span) & (n > 0)
    nt = (((1,), (1,)), ((), ()))
    batch = 2 * HEADS_PER_GROUP
    problems = [(r, h) for r in range(dil) for h in range(HEADS_PER_GROUP)]
    for b0 in range(0, len(problems), batch):
        todo = problems[b0:b0 + batch]
        cols = [slice((r * HEADS_PER_GROUP + h) * ATT_HEAD_DIM, (r * HEADS_PER_GROUP + h + 1) * ATT_HEAD_DIM)
                for r, h in todo]
        qs = [q_ref[:, cs] for cs in cols]
        s_c = [lax.dot_general(q, kc_ref[:, cs], nt, preferred_element_type=F32) for q, cs in zip(qs, cols)]
        s_c = [jnp.where(valid_c, s * scale, -jnp.inf) for s in s_c]
        m = [jnp.max(s, axis=-1, keepdims=True) for s in s_c]
        if has_prev:
            s_p = [lax.dot_general(q, kp_ref[:, cs], nt, preferred_element_type=F32) for q, cs in zip(qs, cols)]
            s_p = [jnp.where(valid_p, s * scale, -jnp.inf) for s in s_p]
            m = [jnp.maximum(mm, jnp.max(s, axis=-1, keepdims=True)) for mm, s in zip(m, s_p)]
        p_c = [jnp.exp(s - mm) for s, mm in zip(s_c, m)]
        den = [jnp.sum(p, axis=-1, keepdims=True) for p in p_c]
        o = [jnp.dot(p.astype(BF16), vc_ref[:, cs], preferred_element_type=F32) for p, cs in zip(p_c, cols)]
        if has_prev:
            p_p = [jnp.exp(s - mm) for s, mm in zip(s_p, m)]
            den = [d + jnp.sum(p, axis=-1, keepdims=True) for d, p in zip(den, p_p)]
            o = [oo + jnp.dot(p.astype(BF16), vp_ref[:, cs], preferred_element_type=F32)
                 for oo, p, cs in zip(o, p_p, cols)]
        for (r, h), oo, d, mm in zip(todo, o, den, m):
            rows = pl.ds(r, ATT_BLOCK, stride=dil) if dil > 1 else slice(None)
            o_ref[h, rows, :] = oo / d
            lse_ref[h, rows, :] = jnp.broadcast_to(mm + jnp.log(d), (ATT_BLOCK, ATT_HEAD_DIM))


def _band_attn_call(q, k, v, bsz, seq, window, dil, g):
    gw = HEADS_PER_GROUP * ATT_HEAD_DIM
    t_pad = q.shape[0] * dil
    span = window // dil
    assert span <= ATT_BLOCK and seq % (dil * ATT_BLOCK) == 0
    nb = seq // (dil * ATT_BLOCK)
    cur = pl.BlockSpec((ATT_BLOCK, dil * gw), lambda b, n: (b * nb + n, 0))
    prev = pl.BlockSpec((ATT_BLOCK, dil * gw), lambda b, n: (b * nb + jnp.maximum(n - 1, 0), 0))
    out = pl.BlockSpec((HEADS_PER_GROUP, ATT_BLOCK * dil, ATT_HEAD_DIM), lambda b, n: (0, b * nb + n, 0))
    kern = functools.partial(_band_attn_kernel, dil=dil, span=span, has_prev=nb > 1)
    return pl.pallas_call(
        kern, grid=(bsz, nb),
        in_specs=[cur, cur, prev, cur, prev], out_specs=[out, out],
        out_shape=[jax.ShapeDtypeStruct((HEADS_PER_GROUP, t_pad, ATT_HEAD_DIM), F32)] * 2,
        compiler_params=_params("parallel", "parallel"), name=f"band_attn_g{g}",
    )(q, k, k, v, v)


def _sample_attn_kernel(q_ref, kc_ref, vc_ref, kn_ref, vn_ref, o_ref, lse_ref, *, dil, lq):
    scale = ATT_HEAD_DIM ** -0.5
    n_cache = kc_ref.shape[1]
    row = lax.broadcasted_iota(jnp.int32, (n_cache, HEADS_PER_GROUP, 1), 0)
    for i in range(lq):
        res = i if dil > 1 else 0
        new_rows = range(i + 1) if dil == 1 else (i,)
        q = q_ref[0, i]
        s_c = jnp.sum(kc_ref[0, :, res] * q[None], axis=-1, keepdims=True) * scale
        if dil == 1:
            s_c = jnp.where(row >= i, s_c, -jnp.inf)
        s_new = [jnp.sum(kn_ref[0, j] * q, axis=-1, keepdims=True) * scale for j in new_rows]
        m = jnp.max(s_c, axis=0)
        for s in s_new:
            m = jnp.maximum(m, s)
        p_c = jnp.exp(s_c - m[None])
        den = jnp.sum(p_c, axis=0)
        o = jnp.sum(p_c * vc_ref[0, :, res], axis=0)
        for j, s in zip(new_rows, s_new):
            p = jnp.exp(s - m)
            den = den + p
            o = o + p * vn_ref[0, j]
        o_ref[0, i] = o / den
        lse_ref[0, i] = jnp.broadcast_to(m + jnp.log(den), (HEADS_PER_GROUP, ATT_HEAD_DIM))


def _decode_mem_attn_kernel(q_ref, k_ref, v_ref, o_ref):
    scale = q_ref.shape[-1] ** -0.5
    for i in range(q_ref.shape[1]):
        q = q_ref[0, i]
        s = jnp.sum(k_ref[0] * q[None], axis=-1, keepdims=True) * scale
        m = jnp.max(s, axis=0)
        p = jnp.exp(s - m[None])
        den = jnp.sum(p, axis=0)
        o_ref[0, i] = (jnp.sum(p * v_ref[0], axis=0) / den).astype(o_ref.dtype)


def _decode_mem_attn_call(q, mem_k, mem_v):
    b, lq, nh, e = q.shape
    qs = pl.BlockSpec((1, lq, nh, e), lambda i: (i, 0, 0, 0))
    ms = pl.BlockSpec((1, mem_k.shape[1], nh, e), lambda i: (i, 0, 0, 0))
    return pl.pallas_call(
        _decode_mem_attn_kernel, grid=(b,), in_specs=[qs, ms, ms], out_specs=qs,
        out_shape=jax.ShapeDtypeStruct((b, lq, nh, e), F32),
        compiler_params=_params("parallel"), name="mem_attn_sample",
    )(q, mem_k, mem_v)


def _sample_attn_call(q, k_new, v_new, k_buf, v_buf, window, dil, g):
    b, lq, nh, e = q.shape
    assert k_buf.shape[1] == window and window % dil == 0 and (dil == 1 or dil >= lq) and lq <= window // dil
    n_res = min(dil, lq)
    n_cache = window // dil
    cache = lambda a: a.reshape(b, n_cache, dil, nh, e)
    new = pl.BlockSpec((1, lq, nh, e), lambda i: (i, 0, 0, 0))
    buf = pl.BlockSpec((1, n_cache, n_res, nh, e), lambda i: (i, 0, 0, 0, 0))
    kern = functools.partial(_sample_attn_kernel, dil=dil, lq=lq)
    return pl.pallas_call(
        kern, grid=(b,), in_specs=[new, buf, buf, new, new], out_specs=[new, new],
        out_shape=[jax.ShapeDtypeStruct((b, lq, nh, e), F32)] * 2,
        compiler_params=_params("parallel"), name=f"sample_attn_g{g}",
    )(q, cache(k_buf), cache(v_buf), k_new, v_new)


def _combine_kernel(*refs):
    n_grp = len(DIL_CONFIGS)
    o_refs, l_refs, out_ref = refs[:n_grp], refs[n_grp:2 * n_grp], refs[2 * n_grp]
    for h in range(HEADS_PER_GROUP):
        ls = [r[h] for r in l_refs]
        m = functools.reduce(jnp.maximum, ls)
        ws = [jnp.exp(l - m) for l in ls]
        tot = functools.reduce(lambda a, b: a + b, ws)
        acc = functools.reduce(lambda a, b: a + b, [w * r[h] for w, r in zip(ws, o_refs)])
        out_ref[:, h * ATT_HEAD_DIM:(h + 1) * ATT_HEAD_DIM] = (acc / tot).astype(out_ref.dtype)


def _combine_call(outs, lses, tm=512):
    nh, t, e = outs[0].shape
    row = pl.BlockSpec((nh, tm, e), lambda i: (0, i, 0))
    return pl.pallas_call(
        _combine_kernel, grid=(t // tm,), in_specs=[row] * (2 * len(outs)),
        out_specs=pl.BlockSpec((tm, nh * e), lambda i: (i, 0)),
        out_shape=jax.ShapeDtypeStruct((t, nh * e), BF16),
        compiler_params=_params("parallel"), name="combine_groups",
    )(*outs, *lses)


def _ssd_kernel(*refs, n_chunks, valid_len, has_state):
    if has_state:
        (z_ref, xbc_ref, dt_ref, h0_ref, cprev_ref, cw_ref, cb_ref, dtb_ref, alog_ref, dskip_ref, nw_ref, expand_ref,
         y_ref, hfin_ref, ctail_ref, ht_ref, xext_ref) = refs
    else:
        (z_ref, xbc_ref, dt_ref, cw_ref, cb_ref, dtb_ref, alog_ref, dskip_ref, nw_ref, expand_ref,
         y_ref, hfin_ref, ctail_ref, ht_ref, xext_ref) = refs
    c = pl.program_id(1)
    lc = SSM_CHUNK
    n_st = SSM_STATE
    gw = ht_ref.shape[1] // SSM_GROUPS
    d_inner = ht_ref.shape[1]
    heads_per_group = gw // SSM_HEAD_DIM
    n_tr = d_inner // LANES

    @pl.when(c == 0)
    def _():
        if has_state:
            xext_ref[0:SUBLANES, :] = cprev_ref[0]
            for k in range(n_tr):
                ht_ref[:, k * LANES:(k + 1) * LANES] = h0_ref[0, k * LANES:(k + 1) * LANES, :].T
        else:
            xext_ref[0:SUBLANES, :] = jnp.zeros((SUBLANES, xext_ref.shape[1]), F32)
            ht_ref[...] = jnp.zeros(ht_ref.shape, F32)

    xext_ref[SUBLANES:SUBLANES + lc, :] = xbc_ref[...]

    def conv_silu(c0, width):
        acc = cb_ref[:, c0:c0 + width]
        for s in range(CONV_W):
            acc = acc + cw_ref[CONV_W - 1 - s:CONV_W - s, c0:c0 + width] * xext_ref[pl.ds(SUBLANES - s, lc), c0:c0 + width]
        return acc * _sigmoid(acc)

    dtr = dt_ref[...] + dtb_ref[...]
    dt = jnp.maximum(dtr, 0.0) + jnp.log(1.0 + jnp.exp(-jnp.abs(dtr)))
    if valid_len < lc:
        trow = lax.broadcasted_iota(jnp.int32, dt.shape, 0)
        dt = jnp.where(trow < valid_len, dt, 0.0)
    a = dt * (-jnp.exp(alog_ref[...]))
    ti = lax.broadcasted_iota(jnp.int32, (lc, lc), 0)
    si = lax.broadcasted_iota(jnp.int32, (lc, lc), 1)
    tri = si <= ti
    f32_dot = lambda lhs, rhs: jnp.dot(lhs, rhs, preferred_element_type=F32)
    tri_b = tri.astype(BF16)
    acum = functools.reduce(lambda u, v: u + v, [f32_dot(tri_b, p) for p in _split3(a)])
    tri_t = (ti <= si).astype(BF16)
    acum_t = functools.reduce(lambda u, v: u + v, [f32_dot(p, tri_t) for p in _split3(a.T)])
    dt3, ac3 = _split3(dt), _split3(acum)
    lane = lax.broadcasted_iota(jnp.int32, (lc, LANES), 1)
    nt = (((1,), (1,)), ((), ()))

    for g in range(SSM_GROUPS):
        c0 = g * gw
        xs = conv_silu(c0, gw)
        bm = conv_silu(d_inner + g * n_st, n_st)
        cm = conv_silu(d_inner + SSM_GROUPS * n_st + g * n_st, n_st)
        expand = expand_ref[:, c0:c0 + gw]
        dt_x = functools.reduce(lambda u, v: u + v, [f32_dot(p, expand) for p in dt3])
        ac_x = functools.reduce(lambda u, v: u + v, [f32_dot(p, expand) for p in ac3])
        xdt = xs * dt_x
        xdt_b = xdt.astype(BF16)
        cm_b = cm.astype(BF16)
        cb = lax.dot_general(cm_b, bm.astype(BF16), nt, preferred_element_type=F32)
        h_old = ht_ref[:, c0:c0 + gw]
        y = jnp.dot(cm_b, h_old.astype(BF16), preferred_element_type=F32) * jnp.exp(ac_x)
        diag = []
        for k in range(gw // LANES):
            x_pair = xdt_b[:, k * LANES:(k + 1) * LANES]
            y_pair = None
            for hh in range(LANES // SSM_HEAD_DIM):
                h = g * heads_per_group + k * (LANES // SSM_HEAD_DIM) + hh
                seg = acum[:, h:h + 1] - acum_t[h:h + 1, :]
                lmat = jnp.exp(jnp.where(tri, seg, -jnp.inf))
                m_b = (cb * lmat).astype(BF16)
                in_head = (lane >= hh * SSM_HEAD_DIM) & (lane < (hh + 1) * SSM_HEAD_DIM)
                part = jnp.dot(m_b, jnp.where(in_head, x_pair, jnp.zeros_like(x_pair)), preferred_element_type=F32)
                y_pair = part if y_pair is None else y_pair + part
            diag.append(y_pair)
        y = y + jnp.concatenate(diag, axis=1) + dskip_ref[:, c0:c0 + gw] * xs
        zg = z_ref[:, c0:c0 + gw].astype(F32)
        y = y * (zg * _sigmoid(zg))
        y = y * lax.rsqrt(jnp.mean(y * y, axis=-1, keepdims=True) + RMS_EPS)
        y_ref[:, c0:c0 + gw] = (y * nw_ref[:, c0:c0 + gw]).astype(y_ref.dtype)
        a_last = ac_x[lc - 1:lc, :]
        xw = (xdt * jnp.exp(a_last - ac_x)).astype(BF16)
        ht_ref[:, c0:c0 + gw] = h_old * jnp.exp(a_last) + jnp.dot(bm.T.astype(BF16), xw, preferred_element_type=F32)

    xext_ref[0:SUBLANES, :] = xext_ref[lc:lc + SUBLANES, :]

    @pl.when(c == n_chunks - 1)
    def _():
        ctail_ref[0] = xext_ref[0:SUBLANES, :]
        for k in range(n_tr):
            hfin_ref[0, k * LANES:(k + 1) * LANES, :] = ht_ref[:, k * LANES:(k + 1) * LANES].T


def _ssd_call(z, xbc, dt_raw, n_seq, n_chunks, valid_len, state, conv_w, conv_b, dt_bias, a_log, d_skip, norm_w,
              out_rows, name):
    d_inner = z.shape[1]
    conv_dim = xbc.shape[1]
    n_heads = d_inner // SSM_HEAD_DIM
    lc = SSM_CHUNK
    pad_h = lambda v: jnp.pad(v.astype(F32), (0, LANES - n_heads)).reshape(1, LANES)
    rows = lambda w: pl.BlockSpec((lc, w), lambda b, c: (b * n_chunks + c, 0))
    const = lambda r, w: pl.BlockSpec((r, w), lambda b, c: (0, 0))
    per_seq = lambda r, w: pl.BlockSpec((1, r, w), lambda b, c: (b, 0, 0))
    args = [z, xbc, dt_raw]
    specs = [rows(d_inner), rows(conv_dim), rows(LANES)]
    if state is not None:
        args += list(state)
        specs += [per_seq(d_inner, SSM_STATE), per_seq(SUBLANES, conv_dim)]
    args += [conv_w, conv_b.reshape(1, conv_dim), pad_h(dt_bias), pad_h(a_log),
             jnp.repeat(d_skip.astype(F32), SSM_HEAD_DIM).reshape(1, d_inner), norm_w.reshape(1, d_inner),
             (jnp.arange(LANES)[:, None] == jnp.arange(d_inner)[None, :] // SSM_HEAD_DIM).astype(BF16)]
    specs += [const(CONV_W, conv_dim), const(1, conv_dim), const(1, LANES), const(1, LANES),
              const(1, d_inner), const(1, d_inner), const(LANES, d_inner)]
    kern = functools.partial(_ssd_kernel, n_chunks=n_chunks, valid_len=valid_len, has_state=state is not None)
    return pl.pallas_call(
        kern, grid=(n_seq, n_chunks), in_specs=specs,
        out_specs=[rows(d_inner), per_seq(d_inner, SSM_STATE), per_seq(SUBLANES, conv_dim)],
        out_shape=[jax.ShapeDtypeStruct((out_rows, d_inner), BF16),
                   jax.ShapeDtypeStruct((n_seq, d_inner, SSM_STATE), F32),
                   jax.ShapeDtypeStruct((n_seq, SUBLANES, conv_dim), F32)],
        scratch_shapes=[pltpu.VMEM((SSM_STATE, d_inner), F32), pltpu.VMEM((lc + 2 * SUBLANES, conv_dim), F32)],
        compiler_params=_params("parallel", "arbitrary"), name=name,
    )(*args)


def _mem_attn_kernel(q_ref, k_ref, v_ref, o_ref):
    hd = q_ref.shape[1] // MEM_HEADS
    scale = hd ** -0.5
    cols = [slice(h * hd, (h + 1) * hd) for h in range(MEM_HEADS)]
    nt = (((1,), (1,)), ((), ()))
    s = [lax.dot_general(q_ref[:, c], k_ref[:, c].astype(BF16), nt, preferred_element_type=F32) * scale for c in cols]
    m = [jnp.max(x, axis=-1, keepdims=True) for x in s]
    p = [jnp.exp(x - mm) for x, mm in zip(s, m)]
    den = [jnp.sum(x, axis=-1, keepdims=True) for x in p]
    o = [jnp.dot(x.astype(BF16), v_ref[:, c].astype(BF16), preferred_element_type=F32) for x, c in zip(p, cols)]
    for c, oo, dd in zip(cols, o, den):
        o_ref[:, c] = (oo / dd).astype(o_ref.dtype)


def _mem_attn_call(q, mem_k, mem_v, n_seq, lq, tq, out_rows, name):
    d = q.shape[1]
    n_mem = mem_k.shape[0] // n_seq
    nq = lq // tq
    qs = pl.BlockSpec((tq, d), lambda b, i: (b * nq + i, 0))
    ks = pl.BlockSpec((n_mem, d), lambda b, i: (b, 0))
    return pl.pallas_call(
        _mem_attn_kernel, grid=(n_seq, nq), in_specs=[qs, ks, ks], out_specs=qs,
        out_shape=jax.ShapeDtypeStruct((out_rows, d), BF16),
        compiler_params=_params("parallel", "parallel"), name=name,
    )(q, mem_k, mem_v)


def _router_kernel(x_ref, w_ref, b_ref, idx_ref, gate_ref):
    logits = jnp.dot(x_ref[...], w_ref[...], precision=HI, preferred_element_type=F32) + b_ref[...]
    lane = lax.broadcasted_iota(jnp.int32, logits.shape, 1)
    idx_out = jnp.zeros(logits.shape, jnp.int32)
    vals = []
    for k in range(TOP_K):
        m = jnp.max(logits, axis=-1, keepdims=True)
        pick = jnp.min(jnp.where(logits == m, lane, LANES), axis=-1, keepdims=True)
        idx_out = jnp.where(lane == k, pick, idx_out)
        logits = jnp.where(lane == pick, -jnp.inf, logits)
        vals.append(m)
    exps = [jnp.exp(v - vals[0]) for v in vals]
    tot = functools.reduce(lambda a, b: a + b, exps)
    gate_out = jnp.zeros(logits.shape, F32)
    for k in range(TOP_K):
        gate_out = jnp.where(lane == k, exps[k] / tot, gate_out)
    idx_ref[...] = idx_out
    gate_ref[...] = gate_out


def _router_call(x, w_router, b_router, tm=256):
    t, d = x.shape
    n_exp = w_router.shape[1]
    w = jnp.pad(w_router.astype(F32), ((0, 0), (0, LANES - n_exp)))
    b = jnp.pad(b_router.astype(F32), (0, LANES - n_exp), constant_values=-jnp.inf).reshape(1, LANES)
    out = pl.BlockSpec((tm, LANES), lambda i: (i, 0))
    return pl.pallas_call(
        _router_kernel, grid=(t // tm,),
        in_specs=[pl.BlockSpec((tm, d), lambda i: (i, 0)), pl.BlockSpec((d, LANES), lambda i: (0, 0)),
                  pl.BlockSpec((1, LANES), lambda i: (0, 0))],
        out_specs=[out, out],
        out_shape=[jax.ShapeDtypeStruct((t, LANES), jnp.int32), jax.ShapeDtypeStruct((t, LANES), F32)],
        compiler_params=_params("parallel"), name="router",
    )(x, w, b)


def _dispatch_kernel(pend_ref, slot_ref, x_ref, xg_hbm, sbuf, zbuf, sem, zsem, *, n_exp, block_rows, n_tiles):
    i = pl.program_id(0)
    s = i % 2
    tm = x_ref.shape[0]

    def zero_copy(e):
        start = pl.multiple_of(pend_ref[e] - block_rows, block_rows)
        return pltpu.make_async_copy(zbuf, xg_hbm.at[pl.ds(start, block_rows)], zsem)

    def has_rows(e):
        return pend_ref[e] > (pend_ref[e - 1] if e else 0)

    def wait_rows(s_):
        for _ in range(TOP_K):
            pltpu.make_async_copy(sbuf.at[s_], xg_hbm.at[pl.ds(0, tm)], sem.at[s_]).wait()

    @pl.when(i == 0)
    def _():
        zbuf[...] = jnp.zeros(zbuf.shape, F32)
        for e in range(n_exp):
            pl.when(has_rows(e))(lambda e=e: zero_copy(e).start())
        for e in range(n_exp):
            pl.when(has_rows(e))(lambda e=e: zero_copy(e).wait())

    @pl.when(i >= 2)
    def _():
        wait_rows(s)

    sbuf[s] = x_ref[...]
    for r in range(tm):
        for k in range(TOP_K):
            pltpu.make_async_copy(sbuf.at[s, pl.ds(r, 1)], xg_hbm.at[pl.ds(slot_ref[0, 0, r * TOP_K + k], 1)],
                                  sem.at[s]).start()

    @pl.when(i == n_tiles - 1)
    def _():
        wait_rows(s)
        if n_tiles > 1:
            wait_rows(1 - s)


def _dispatch_call(x, slot, pad_end, n_slots, tm=256):
    t, d = x.shape
    n_tiles = t // tm
    grid_spec = pltpu.PrefetchScalarGridSpec(
        num_scalar_prefetch=1, grid=(n_tiles,),
        in_specs=[pl.BlockSpec((1, 1, tm * TOP_K), lambda i, pe: (i, 0, 0), memory_space=pltpu.SMEM),
                  pl.BlockSpec((tm, d), lambda i, pe: (i, 0))],
        out_specs=pl.BlockSpec(memory_space=pl.ANY),
        scratch_shapes=[pltpu.VMEM((2, tm, d), F32), pltpu.VMEM((MOE_TM, d), F32),
                        pltpu.SemaphoreType.DMA((2,)), pltpu.SemaphoreType.DMA(())],
    )
    kern = functools.partial(_dispatch_kernel, n_exp=pad_end.shape[0], block_rows=MOE_TM, n_tiles=n_tiles)
    return pl.pallas_call(
        kern, grid_spec=grid_spec, out_shape=jax.ShapeDtypeStruct((n_slots, d), F32),
        compiler_params=_params("arbitrary"), name="moe_dispatch",
    )(pad_end, slot.reshape(n_tiles, 1, tm * TOP_K), x)


def _moe_kernel(nused_ref, bexp_ref, x_ref, wg_ref, wu_ref, wd_ref, bg_ref, bu_ref, bd_ref, y_ref, xb, *, nj):
    i = pl.program_id(0)
    j = pl.program_id(1)

    @pl.when(i < nused_ref[0])
    def _():
        @pl.when(j == 0)
        def _():
            xb[...] = x_ref[...].astype(BF16)
            y_ref[...] = jnp.broadcast_to(bd_ref[0], y_ref.shape)

        x = xb[...]
        gate = jnp.minimum(jnp.dot(x, wg_ref[0], preferred_element_type=F32) + bg_ref[0], SWIGLU_LIMIT)
        up = jnp.clip(jnp.dot(x, wu_ref[0], preferred_element_type=F32) + bu_ref[0], -SWIGLU_LIMIT, SWIGLU_LIMIT)
        act = (up + 1.0) * gate * _sigmoid(gate * SWIGLU_ALPHA)
        y_ref[...] += jnp.dot(act.astype(BF16), wd_ref[0], preferred_element_type=F32)


def _moe_call(x, top_idx, w_gate_up, b_gate_up, w_down, b_down):
    t, d = x.shape
    n_exp, _, two_h = w_gate_up.shape
    d_exp = two_h // 2
    tm, tn = MOE_TM, MOE_TN
    nj = d_exp // tn
    n_assign = t * TOP_K
    n_blocks = n_assign // tm + n_exp
    n_slots = n_blocks * tm

    flat_e = top_idx.reshape(n_assign)
    onehot = (flat_e[:, None] == jnp.arange(n_exp, dtype=jnp.int32)[None, :]).astype(jnp.int32)
    counts = jnp.sum(onehot, axis=0)
    rank = jnp.take_along_axis(jnp.cumsum(onehot, axis=0), flat_e[:, None], axis=1)[:, 0] - 1
    padded = (counts + tm - 1) // tm * tm
    pad_end = jnp.cumsum(padded)
    slot = (pad_end - padded)[flat_e] + rank
    n_used = (pad_end[-1] // tm).astype(jnp.int32)
    blk = jnp.minimum(jnp.arange(n_blocks, dtype=jnp.int32), n_used - 1) * tm
    block_expert = jnp.minimum(jnp.sum((pad_end[None, :] <= blk[:, None]).astype(jnp.int32), axis=1), n_exp - 1)
    xg = _dispatch_call(x, slot, pad_end.astype(jnp.int32), n_slots)

    live = lambda i, nu: i < nu[0]
    rows = pl.BlockSpec((tm, d), lambda i, j, nu, be: (jnp.minimum(i, nu[0] - 1), 0))
    grid_spec = pltpu.PrefetchScalarGridSpec(
        num_scalar_prefetch=2, grid=(n_blocks, nj),
        in_specs=[
            rows,
            pl.BlockSpec((1, d, tn), lambda i, j, nu, be: (be[i], 0, jnp.where(live(i, nu), j, nj - 1))),
            pl.BlockSpec((1, d, tn), lambda i, j, nu, be: (be[i], 0, nj + jnp.where(live(i, nu), j, nj - 1))),
            pl.BlockSpec((1, tn, d), lambda i, j, nu, be: (be[i], jnp.where(live(i, nu), j, nj - 1), 0)),
            pl.BlockSpec((1, 1, tn), lambda i, j, nu, be: (be[i], 0, jnp.where(live(i, nu), j, nj - 1))),
            pl.BlockSpec((1, 1, tn), lambda i, j, nu, be: (be[i], 0, nj + jnp.where(live(i, nu), j, nj - 1))),
            pl.BlockSpec((1, 1, d), lambda i, j, nu, be: (be[i], 0, 0)),
        ],
        out_specs=rows,
        scratch_shapes=[pltpu.VMEM((tm, d), BF16)],
    )
    bgu = b_gate_up.astype(F32).reshape(n_exp, 1, two_h)
    y_slots = pl.pallas_call(
        functools.partial(_moe_kernel, nj=nj), grid_spec=grid_spec,
        out_shape=jax.ShapeDtypeStruct((n_slots, d), F32),
        compiler_params=_params("arbitrary", "arbitrary"), name="moe_experts",
    )(n_used.reshape(1), block_expert, xg, w_gate_up, w_gate_up, w_down, bgu, bgu,
      b_down.astype(F32).reshape(n_exp, 1, d))
    return y_slots, slot


def _moe_out_kernel(slotc_ref, slotn_ref, y_hbm, gate_ref, h_ref, g_ref, b_ref, head_ref, tail_ref, ybuf, sem,
                    *, n_head, n_tiles):
    i = pl.program_id(0)
    s = i % 2
    tm = h_ref.shape[0]

    def fetch(tab_ref, s_):
        for r in range(TOP_K * tm):
            pltpu.make_async_copy(y_hbm.at[pl.ds(tab_ref[0, 0, r], 1)], ybuf.at[s_, pl.ds(r, 1)], sem.at[s_]).start()

    @pl.when(i == 0)
    def _():
        fetch(slotc_ref, 0)

    @pl.when(i + 1 < n_tiles)
    def _():
        fetch(slotn_ref, 1 - s)

    pltpu.make_async_copy(y_hbm.at[pl.ds(0, TOP_K * tm)], ybuf.at[s], sem.at[s]).wait()
    gates = gate_ref[...]
    y = gates[:, 0:1] * ybuf[s, 0:tm]
    for k in range(1, TOP_K):
        y = y + gates[:, k:k + 1] * ybuf[s, k * tm:(k + 1) * tm]
    out = _layer_norm_rows(DN_ALPHA * h_ref[...] + y, g_ref[...], b_ref[...])

    @pl.when(pl.program_id(0) < n_head)
    def _():
        head_ref[...] = out

    @pl.when(pl.program_id(0) >= n_head)
    def _():
        tail_ref[...] = out


def _moe_out_call(y_slots, slot, gates, h, g, b, n_head_rows, tm=256):
    t, d = h.shape
    nb = t // tm
    n_head = n_head_rows // tm
    row = pl.BlockSpec((tm, d), lambda i: (i, 0))
    vec = pl.BlockSpec((1, d), lambda i: (0, 0))
    table = jnp.transpose(slot.reshape(nb, tm, TOP_K), (0, 2, 1)).reshape(nb, 1, TOP_K * tm)
    smem = lambda imap: pl.BlockSpec((1, 1, TOP_K * tm), imap, memory_space=pltpu.SMEM)
    return pl.pallas_call(
        functools.partial(_moe_out_kernel, n_head=n_head, n_tiles=nb), grid=(nb,),
        in_specs=[smem(lambda i: (i, 0, 0)), smem(lambda i: (jnp.minimum(i + 1, nb - 1), 0, 0)),
                  pl.BlockSpec(memory_space=pl.ANY), pl.BlockSpec((tm, LANES), lambda i: (i, 0)), row, vec, vec],
        out_specs=[pl.BlockSpec((tm, d), lambda i: (jnp.minimum(i, n_head - 1), 0)),
                   pl.BlockSpec((tm, d), lambda i: (jnp.maximum(i - n_head, 0), 0))],
        out_shape=[jax.ShapeDtypeStruct((n_head_rows, d), F32), jax.ShapeDtypeStruct((t - n_head_rows, d), F32)],
        scratch_shapes=[pltpu.VMEM((2, TOP_K * tm, d), F32), pltpu.SemaphoreType.DMA((2,))],
        compiler_params=_params("arbitrary"), name="moe_combine_ln",
    )(table, table, y_slots, gates, h, g.reshape(1, d), b.reshape(1, d))


def kernel(x_prompt, x_sample, state_conv, state_ssm, cache_k_w128, cache_v_w128, cache_k_w512, cache_v_w512, cache_k_w2048, cache_v_w2048, cache_mem_k, cache_mem_v, mem_prompt, ln_in_g, ln_in_b, w_in, conv_w, conv_b, dt_bias, a_log, d_skip, ssm_norm_w, w_branch_ssm, w_branch_att, w_mix_out, ln1_g, ln1_b, w_mem_q, w_mem_k, w_mem_v, w_mem_o, ln2_g, ln2_b, w_router, b_router, w_gate_up, b_gate_up, w_down, b_down, ln3_g, ln3_b):
    assert w_in.shape[0] == DEPTH
    bp, lp, d = x_prompt.shape
    bs, ls, _ = x_sample.shape
    n_p, n_s = bp * lp, bs * ls
    t_real = n_p + n_s
    t_pad = -(-t_real // ROW_TILE) * ROW_TILE
    d_inner = ssm_norm_w.shape[1]
    conv_dim = conv_w.shape[2]
    n_heads = d_inner // SSM_HEAD_DIM
    gw = HEADS_PER_GROUP * ATT_HEAD_DIM
    att_w = len(DIL_CONFIGS) * gw
    n_mem = mem_prompt.shape[1]
    lc = SSM_CHUNK
    assert lp % lc == 0 and ls <= lc and n_s % SUBLANES == 0

    def sample_rows(a):
        return a[n_p:n_p + n_s]

    def with_sample_rows(a, rows):
        tail = jnp.concatenate([rows.astype(a.dtype), jnp.zeros((t_pad - t_real, a.shape[1]), a.dtype)], axis=0)
        return lax.dynamic_update_slice(a, tail, (n_p, 0))

    def with_sample_heads(a, rows):
        nh, _, e = a.shape
        rows = jnp.transpose(rows.reshape(n_s, nh, e), (1, 0, 2)).astype(a.dtype)
        tail = jnp.concatenate([rows, jnp.zeros((nh, t_pad - t_real, e), a.dtype)], axis=1)
        return lax.dynamic_update_slice(a, tail, (0, n_p, 0))

    x_tail = jnp.concatenate([x_sample.reshape(n_s, d), jnp.zeros((t_pad - t_real, d), x_sample.dtype)], axis=0)
    h0, h0_b = _ln_call(x_prompt.reshape(n_p, d), x_tail, ln_in_g, ln_in_b)

    o_z, o_xbc, o_dt, o_qkv, o_g = 0, d_inner, d_inner + conv_dim, d_inner + conv_dim + n_heads, \
        d_inner + conv_dim + n_heads + 3 * att_w
    w_cols = lambda a, b: w_in[0, :, a:b].astype(BF16)
    n_exp, _, two_h = w_gate_up.shape[1:]
    wgu_src = w_gate_up[0].reshape(n_exp * d, two_h)
    wd_src = w_down[0].reshape(n_exp * (two_h // 2), d)
    steps = lambda n_cols, tn: (n_cols // tn) * (t_pad // 512)
    gu_rows, wd_rows = 128, 256
    gu_chunks, wd_chunks = wgu_src.shape[0] // gu_rows, wd_src.shape[0] // wd_rows
    n_a = min(steps(conv_dim, 1024), gu_chunks)
    n_b = min(steps(d_inner, 1024), gu_chunks - n_a)
    n_c = min(steps(2 * d, 1024), wd_chunks)
    xbc, wgu_b = _mm_call(h0_b, w_cols(o_xbc, o_dt), F32, 512, 1024, "in_xbc", (wgu_src, None, 0, n_a, gu_rows))
    if n_b:
        z, wgu_b = _mm_call(h0_b, w_cols(o_z, o_xbc), BF16, 512, 1024, "in_z", (wgu_src, wgu_b, n_a, n_b, gu_rows))
    else:
        z = _mm_call(h0_b, w_cols(o_z, o_xbc), BF16, 512, 1024, "in_z")
    if gu_chunks - n_a - n_b:
        wgu_b = _cast_call(wgu_src, wgu_b, n_a + n_b, gu_chunks - n_a - n_b, gu_rows)
    w_dt = jnp.pad(w_cols(o_dt, o_qkv), ((0, 0), (0, LANES - n_heads)))
    dt_raw = _mm_call(h0_b, w_dt, F32, 512, LANES, "in_dt")
    qkv = _mm_heads_call(h0_b, w_cols(o_qkv, o_g), F32, 512, att_w, "in_qkv")
    gates, wd_b = _mm_call(h0_b, w_cols(o_g, w_in.shape[2]), BF16, 512, 1024, "in_gates",
                           (wd_src, None, 0, n_c, wd_rows))
    if wd_chunks - n_c:
        wd_b = _cast_call(wd_src, wd_b, n_c, wd_chunks - n_c, wd_rows)
    wgu_b = wgu_b.reshape(n_exp, d, two_h)
    wd_b = wd_b.reshape(n_exp, two_h // 2, d)

    ssd_w = (conv_w[0], conv_b[0], dt_bias[0], a_log[0], d_skip[0], ssm_norm_w[0])
    y_ssm, ssm_p, conv_tail = _ssd_call(z, xbc, dt_raw, bp, lp // lc, lc, None, *ssd_w, out_rows=t_pad,
                                        name="ssd_prompt")

    def pad_seq(a):
        return jnp.pad(sample_rows(a).reshape(bs, ls, a.shape[1]), ((0, 0), (0, lc - ls), (0, 0))).reshape(bs * lc, a.shape[1])

    conv_prev = jnp.pad(state_conv[0], ((0, 0), (SUBLANES - (CONV_W - 1), 0), (0, 0)))
    y_s, ssm_s, _ = _ssd_call(pad_seq(z), pad_seq(xbc), pad_seq(dt_raw), bs, 1, ls,
                              (state_ssm[0].reshape(bs, d_inner, SSM_STATE), conv_prev), *ssd_w,
                              out_rows=bs * lc, name="ssd_sample")
    y_ssm = with_sample_rows(y_ssm, y_s.reshape(bs, lc, d_inner)[:, :ls].reshape(n_s, d_inner))

    pos = jnp.concatenate([jnp.tile(jnp.arange(lp, dtype=jnp.int32), bp),
                           jnp.tile(PAST_LEN + jnp.arange(ls, dtype=jnp.int32), bs),
                           jnp.zeros((t_pad - t_real,), jnp.int32)])
    cos_t, sin_t = _rotary_tables(pos)
    split = _rotary_call(qkv, cos_t, sin_t, n_p)
    n_grp = len(DIL_CONFIGS)
    k_tok, v_tok, q_tail, q_view, k_view, v_view = [split[a * n_grp:(a + 1) * n_grp] for a in range(6)]
    caches = ((cache_k_w128, cache_v_w128), (cache_k_w512, cache_v_w512), (cache_k_w2048, cache_v_w2048))
    outs, lses, kv_p, kv_s = [], [], [], []
    heads = lambda a: a.reshape(bs, ls, HEADS_PER_GROUP, ATT_HEAD_DIM)
    for g, (window, dil) in enumerate(DIL_CONFIGS):
        o_p, l_p = _band_attn_call(q_view[g], k_view[g], v_view[g], bp, lp, window, dil, g)
        k_new, v_new = heads(sample_rows(k_tok[g])), heads(sample_rows(v_tok[g]))
        o_s, l_s = _sample_attn_call(heads(q_tail[g][:n_s]), k_new, v_new, caches[g][0][0], caches[g][1][0],
                                     window, dil, g)
        outs.append(with_sample_heads(o_p, o_s))
        lses.append(with_sample_heads(l_p, l_s))
        keep = min(window, lp)
        for a in (k_tok[g], v_tok[g]):
            kv_p.append(a[:n_p].reshape(bp, lp, HEADS_PER_GROUP, ATT_HEAD_DIM)[:, lp - keep:][None])
        kv_s += [k_new[None], v_new[None]]
    att = _combine_call(outs, lses)

    merged = _branch_call(y_ssm, att, gates, w_branch_ssm[0].astype(BF16), w_branch_att[0].astype(BF16))
    h1, h1_b = _mm_res_ln_call(merged, w_mix_out[0].astype(BF16), h0, ln1_g[0], ln1_b[0], "mix_out_ln1")

    mem_b = mem_prompt.reshape(bp * n_mem, d).astype(BF16)
    mem_k_p = _mm_call(mem_b, w_mem_k[0].astype(BF16), F32, 512, 1024, "mem_k")
    mem_v_p = _mm_call(mem_b, w_mem_v[0].astype(BF16), F32, 512, 1024, "mem_v")
    q_mem = _mm_call(h1_b, w_mem_q[0].astype(BF16), BF16, 512, 1024, "mem_q")
    o_mem = _mem_attn_call(q_mem, mem_k_p, mem_v_p, bp, lp, 512, t_pad, "mem_attn_prompt")
    q_s = sample_rows(q_mem).reshape(bs, ls, MEM_HEADS, d // MEM_HEADS).astype(F32)
    o_mem_s = _decode_mem_attn_call(q_s, cache_mem_k[0], cache_mem_v[0])
    o_mem = with_sample_rows(o_mem, o_mem_s.reshape(n_s, d))
    h2, _ = _mm_res_ln_call(o_mem, w_mem_o[0].astype(BF16), h1, ln2_g[0], ln2_b[0], "mem_o_ln2")

    idx_t, gate_t = _router_call(h2, w_router[0], b_router[0])
    y_slots, slot = _moe_call(h2, idx_t[:, :TOP_K], wgu_b, b_gate_up[0], wd_b, b_down[0])
    h3_head, h3_tail = _moe_out_call(y_slots, slot, gate_t, h2, ln3_g[0], ln3_b[0], n_p)

    y_prompt = h3_head.reshape(bp, lp, d)
    y_sample = h3_tail[:n_s].reshape(bs, ls, d)
    conv_p = conv_tail[:, SUBLANES - (CONV_W - 1):][None]
    xp_s = jnp.concatenate([state_conv[0].astype(xbc.dtype), sample_rows(xbc).reshape(bs, ls, conv_dim)], axis=1)
    conv_s = xp_s[:, -(CONV_W - 1):][None]
    state_shape = (n_heads, SSM_HEAD_DIM, SSM_STATE)
    ssm_p = ssm_p.reshape(1, bp, *state_shape)
    ssm_s = ssm_s.reshape(1, bs, *state_shape)
    mem_shape = (1, bp, n_mem, MEM_HEADS, d // MEM_HEADS)
    return (y_prompt, y_sample, conv_p, ssm_p, *kv_p, mem_k_p.reshape(mem_shape), mem_v_p.reshape(mem_shape),
            conv_s, ssm_s, *kv_s)
```

```python
import functools
import math

import jax
import jax.numpy as jnp
from jax import lax
from jax.experimental import pallas as pl
from jax.experimental.pallas import tpu as pltpu

F32 = jnp.float32
BF16 = jnp.bfloat16

PAST_LEN = 16384
SSM_HEAD_DIM = 64
SSM_GROUPS = 8
SSM_STATE = 128
CONV_W = 4
SSM_CHUNK = 128
RMS_EPS = 1e-5
DIL_CONFIGS = ((128, 1), (512, 4), (2048, 16))
HEADS_PER_GROUP = 4
ATT_HEAD_DIM = 128
ATT_BLOCK = 128
ROT_DIM = ATT_HEAD_DIM // 4
ROPE_THETA = 500000.0
MEM_HEADS = 4
TOP_K = 4
SWIGLU_LIMIT = 7.0
SWIGLU_ALPHA = 1.702
LN_EPS = 1e-5
DEPTH = 1
DN_ALPHA = (2.0 * DEPTH) ** 0.25

LANES = 128
SUBLANES = 8
ROW_TILE = 512
MOE_TM = 512
MOE_TN = 1024
VMEM_LIMIT = 56 * 1024 * 1024
HI = lax.Precision.HIGHEST


def _params(*sem):
    return pltpu.CompilerParams(dimension_semantics=sem, vmem_limit_bytes=VMEM_LIMIT)


def _sigmoid(x):
    return 0.5 * jnp.tanh(0.5 * x) + 0.5


def _split3(x):
    hi = x.astype(BF16)
    r1 = x - hi.astype(F32)
    mid = r1.astype(BF16)
    lo = (r1 - mid.astype(F32)).astype(BF16)
    return hi, mid, lo


def _layer_norm_rows(x, g, b):
    mu = jnp.mean(x, axis=-1, keepdims=True)
    xc = x - mu
    var = jnp.mean(xc * xc, axis=-1, keepdims=True)
    return xc * lax.rsqrt(var + LN_EPS) * g + b


def _ln_kernel(xa_ref, xb_ref, g_ref, b_ref, of_ref, ob_ref, *, n_head):
    def emit(x_ref):
        y = _layer_norm_rows(x_ref[...], g_ref[...], b_ref[...])
        of_ref[...] = y
        ob_ref[...] = y.astype(BF16)

    pl.when(pl.program_id(0) < n_head)(lambda: emit(xa_ref))
    pl.when(pl.program_id(0) >= n_head)(lambda: emit(xb_ref))


def _ln_call(x_head, x_tail, g, b, tm=256):
    d = x_head.shape[1]
    n_head, n_tail = x_head.shape[0] // tm, x_tail.shape[0] // tm
    t = (n_head + n_tail) * tm
    row = pl.BlockSpec((tm, d), lambda i: (i, 0))
    vec = pl.BlockSpec((1, d), lambda i: (0, 0))
    return pl.pallas_call(
        functools.partial(_ln_kernel, n_head=n_head), grid=(n_head + n_tail,),
        in_specs=[pl.BlockSpec((tm, d), lambda i: (jnp.minimum(i, n_head - 1), 0)),
                  pl.BlockSpec((tm, d), lambda i: (jnp.maximum(i - n_head, 0), 0)), vec, vec],
        out_specs=[row, row],
        out_shape=[jax.ShapeDtypeStruct((t, d), F32), jax.ShapeDtypeStruct((t, d), BF16)],
        compiler_params=_params("arbitrary"), name="ln_in",
    )(x_head, x_tail, g.reshape(1, d), b.reshape(1, d))


def _mm_kernel(a_ref, w_ref, *rest, n_cast):
    o_ref = rest[-2] if n_cast else rest[-1]
    if n_cast:
        src_ref, dst_ref = rest[0], rest[-1]
        step = pl.program_id(0) * pl.num_programs(1) + pl.program_id(1)

        @pl.when(step < n_cast)
        def _():
            dst_ref[...] = src_ref[...].astype(dst_ref.dtype)

    o_ref[...] = jnp.dot(a_ref[...], w_ref[...], preferred_element_type=F32).astype(o_ref.dtype)


def _mm_call(a, w, out_dtype, tm, tn, name, cast=None):
    m, k = a.shape
    n = w.shape[1]
    tm = min(tm, m)
    ni = m // tm
    args = [a, w]
    in_specs = [pl.BlockSpec((tm, k), lambda j, i: (i, 0)), pl.BlockSpec((k, tn), lambda j, i: (0, j))]
    out_specs = pl.BlockSpec((tm, tn), lambda j, i: (i, j))
    out_shape = jax.ShapeDtypeStruct((m, n), out_dtype)
    aliases = {}
    n_cast = 0
    if cast is not None:
        src, dst, chunk0, n_cast, rows = cast
        assert 0 < n_cast <= (n // tn) * ni
        chunk = pl.BlockSpec((rows, src.shape[1]), lambda j, i: (chunk0 + jnp.minimum(j * ni + i, n_cast - 1), 0))
        args.append(src)
        in_specs.append(chunk)
        if dst is not None:
            aliases = {len(args): 1}
            args.append(dst)
            in_specs.append(pl.BlockSpec(memory_space=pl.ANY))
        out_specs = [out_specs, chunk]
        out_shape = [out_shape, jax.ShapeDtypeStruct(src.shape, BF16)]
    return pl.pallas_call(
        functools.partial(_mm_kernel, n_cast=n_cast), grid=(n // tn, ni),
        in_specs=in_specs, out_specs=out_specs, out_shape=out_shape, input_output_aliases=aliases,
        compiler_params=_params("arbitrary", "arbitrary") if n_cast else _params("parallel", "parallel"), name=name,
    )(*args)


def _cast_kernel(src_ref, _, dst_ref):
    dst_ref[...] = src_ref[...].astype(dst_ref.dtype)


def _cast_call(src, dst, chunk0, n_chunks, rows):
    chunk = pl.BlockSpec((rows, src.shape[1]), lambda i: (chunk0 + i, 0))
    return pl.pallas_call(
        _cast_kernel, grid=(n_chunks,), in_specs=[chunk, pl.BlockSpec(memory_space=pl.ANY)], out_specs=chunk,
        out_shape=jax.ShapeDtypeStruct(src.shape, BF16), input_output_aliases={1: 0},
        compiler_params=_params("parallel"), name="cast_rest",
    )(src, dst)


def _mm_heads_kernel(a_ref, w_ref, o_ref):
    res = jnp.dot(a_ref[...], w_ref[...], preferred_element_type=F32)
    for h in range(o_ref.shape[0]):
        o_ref[h] = res[:, h * LANES:(h + 1) * LANES].astype(o_ref.dtype)


def _mm_heads_call(a, w, out_dtype, tm, tn, name):
    m, k = a.shape
    n = w.shape[1]
    return pl.pallas_call(
        _mm_heads_kernel, grid=(n // tn, m // tm),
        in_specs=[pl.BlockSpec((tm, k), lambda j, i: (i, 0)), pl.BlockSpec((k, tn), lambda j, i: (0, j))],
        out_specs=pl.BlockSpec((tn // LANES, tm, LANES), lambda j, i: (j, i, 0)),
        out_shape=jax.ShapeDtypeStruct((n // LANES, m, LANES), out_dtype),
        compiler_params=_params("parallel", "parallel"), name=name,
    )(a, w)


def _mm_res_ln_kernel(a_ref, w_ref, h_ref, g_ref, b_ref, of_ref, ob_ref):
    mix = jnp.dot(a_ref[...], w_ref[...], preferred_element_type=F32)
    y = _layer_norm_rows(DN_ALPHA * h_ref[...] + mix, g_ref[...], b_ref[...])
    of_ref[...] = y
    ob_ref[...] = y.astype(BF16)


def _mm_res_ln_call(a, w, h, g, b, name, tm=256):
    m, k = a.shape
    d = w.shape[1]
    row = pl.BlockSpec((tm, d), lambda i: (i, 0))
    vec = pl.BlockSpec((1, d), lambda i: (0, 0))
    return pl.pallas_call(
        _mm_res_ln_kernel, grid=(m // tm,),
        in_specs=[pl.BlockSpec((tm, k), lambda i: (i, 0)), pl.BlockSpec((k, d), lambda i: (0, 0)), row, vec, vec],
        out_specs=[row, row],
        out_shape=[jax.ShapeDtypeStruct((m, d), F32), jax.ShapeDtypeStruct((m, d), BF16)],
        compiler_params=_params("parallel"), name=name,
    )(a, w, h, g.reshape(1, d), b.reshape(1, d))


def _branch_kernel(y_ref, att_ref, gs_ref, ga_ref, ws_ref, wa_ref, o_ref):
    bs = jnp.dot(y_ref[...], ws_ref[...], preferred_element_type=F32)
    ba = jnp.dot(att_ref[...], wa_ref[...], preferred_element_type=F32)
    merged = _sigmoid(gs_ref[...].astype(F32)) * bs + _sigmoid(ga_ref[...].astype(F32)) * ba
    o_ref[...] = merged.astype(o_ref.dtype)


def _branch_call(y_ssm, att, gates, ws, wa, tm=512, tn=1024):
    m, ks = y_ssm.shape
    ka = att.shape[1]
    d = ws.shape[1]
    nj = d // tn
    return pl.pallas_call(
        _branch_kernel, grid=(nj, m // tm),
        in_specs=[pl.BlockSpec((tm, ks), lambda j, i: (i, 0)),
                  pl.BlockSpec((tm, ka), lambda j, i: (i, 0)),
                  pl.BlockSpec((tm, tn), lambda j, i: (i, j)),
                  pl.BlockSpec((tm, tn), lambda j, i: (i, nj + j)),
                  pl.BlockSpec((ks, tn), lambda j, i: (0, j)),
                  pl.BlockSpec((ka, tn), lambda j, i: (0, j))],
        out_specs=pl.BlockSpec((tm, tn), lambda j, i: (i, j)),
        out_shape=jax.ShapeDtypeStruct((m, d), BF16),
        compiler_params=_params("parallel", "parallel"), name="branch_merge",
    )(y_ssm, att, gates, gates, ws, wa)


def _rotary_kernel(qkv_ref, cos_ref, sin_ref, perm_ref, *out_refs, n_head):
    n_grp = len(DIL_CONFIGS)
    (kh_refs, vh_refs, kt_refs, vt_refs, qt_refs, qv_refs, kv_refs, vv_refs) = [
        out_refs[a * n_grp:(a + 1) * n_grp] for a in range(8)]
    in_tail = pl.program_id(0) >= n_head
    tm = qkv_ref.shape[1]
    half = ROT_DIM // 2
    gw = HEADS_PER_GROUP * ATT_HEAD_DIM
    n_att = n_grp * HEADS_PER_GROUP

    def rot(x, cos, sin):
        lane = lax.broadcasted_iota(jnp.int32, x.shape, 1)
        partner = jnp.where(lane < half, pltpu.roll(x, ATT_HEAD_DIM - half, 1), pltpu.roll(x, half, 1))
        return x * cos + partner * sin

    def rot_view(x, cos, sin):
        partner = jnp.dot(x.astype(BF16), perm_ref[...], preferred_element_type=F32)
        return (x * cos + partner * sin).astype(BF16)

    def emit_token_major(k_refs, v_refs):
        for g in range(n_grp):
            for h in range(HEADS_PER_GROUP):
                hd = g * HEADS_PER_GROUP + h
                k_refs[g][:, h, :] = rot(qkv_ref[n_att + hd], cos_ref[...], sin_ref[...])
                v_refs[g][:, h, :] = qkv_ref[2 * n_att + hd]

    pl.when(jnp.logical_not(in_tail))(lambda: emit_token_major(kh_refs, vh_refs))
    pl.when(in_tail)(lambda: emit_token_major(kt_refs, vt_refs))

    for g, (_, dil) in enumerate(DIL_CONFIGS):
        for r in range(dil):
            rows = pl.ds(r, tm // dil, stride=dil) if dil > 1 else slice(None)
            cos = cos_ref[rows, :]
            sin = sin_ref[rows, :]
            for h in range(HEADS_PER_GROUP):
                hd = g * HEADS_PER_GROUP + h
                view = slice(r * gw + h * ATT_HEAD_DIM, r * gw + (h + 1) * ATT_HEAD_DIM)
                qv_refs[g][:, view] = rot_view(qkv_ref[hd, rows, :], cos, sin)
                kv_refs[g][:, view] = rot_view(qkv_ref[n_att + hd, rows, :], cos, sin)
                vv_refs[g][:, view] = qkv_ref[2 * n_att + hd, rows, :].astype(BF16)

    @pl.when(in_tail)
    def _():
        for g in range(n_grp):
            for h in range(HEADS_PER_GROUP):
                tok = slice(h * ATT_HEAD_DIM, (h + 1) * ATT_HEAD_DIM)
                qt_refs[g][:, tok] = rot(qkv_ref[g * HEADS_PER_GROUP + h], cos_ref[...], sin_ref[...])


def _rotary_call(qkv, cos_t, sin_t, n_head_rows, tm=256):
    n_slab, t, _ = qkv.shape
    gw = HEADS_PER_GROUP * ATT_HEAD_DIM
    n_grp = len(DIL_CONFIGS)
    n_head = n_head_rows // tm
    hshape = (tm, HEADS_PER_GROUP, ATT_HEAD_DIM)
    head_rows = pl.BlockSpec(hshape, lambda i: (jnp.minimum(i, n_head - 1), 0, 0))
    tail_rows = pl.BlockSpec(hshape, lambda i: (jnp.maximum(i - n_head, 0), 0, 0))
    tail = pl.BlockSpec((tm, gw), lambda i: (jnp.maximum(i - n_head, 0), 0))
    tab = pl.BlockSpec((tm, ATT_HEAD_DIM), lambda i: (i, 0))
    views = [pl.BlockSpec((tm // dil, dil * gw), lambda i: (i, 0)) for _, dil in DIL_CONFIGS]
    view_shapes = [jax.ShapeDtypeStruct((t // dil, dil * gw), BF16) for _, dil in DIL_CONFIGS]
    li = jnp.arange(ATT_HEAD_DIM)[:, None]
    lj = jnp.arange(ATT_HEAD_DIM)[None, :]
    half = ROT_DIM // 2
    perm = (((lj < half) & (li == lj + half)) | ((lj >= half) & (lj < ROT_DIM) & (li == lj - half))).astype(BF16)
    return pl.pallas_call(
        functools.partial(_rotary_kernel, n_head=n_head), grid=(t // tm,),
        in_specs=[pl.BlockSpec((n_slab, tm, ATT_HEAD_DIM), lambda i: (0, i, 0)), tab, tab,
                  pl.BlockSpec((ATT_HEAD_DIM, ATT_HEAD_DIM), lambda i: (0, 0))],
        out_specs=[head_rows] * (2 * n_grp) + [tail_rows] * (2 * n_grp) + [tail] * n_grp + views * 3,
        out_shape=([jax.ShapeDtypeStruct((n_head_rows,) + hshape[1:], F32)] * (2 * n_grp)
                   + [jax.ShapeDtypeStruct((t - n_head_rows,) + hshape[1:], F32)] * (2 * n_grp)
                   + [jax.ShapeDtypeStruct((t - n_head_rows, gw), F32)] * n_grp + view_shapes * 3),
        compiler_params=_params("arbitrary"), name="rotary_split",
    )(qkv, cos_t, sin_t, perm)


def _rotary_tables(pos):
    half = ROT_DIM // 2
    inv_freq = jnp.exp(-math.log(ROPE_THETA) * jnp.arange(half, dtype=F32) * (2.0 / ROT_DIM))
    ang = pos.astype(F32)[:, None] * inv_freq[None, :]
    cos, sin = jnp.cos(ang), jnp.sin(ang)
    rest = ATT_HEAD_DIM - ROT_DIM
    ones = jnp.ones((pos.shape[0], rest), F32)
    cos_t = jnp.concatenate([cos, cos, ones], axis=1)
    sin_t = jnp.concatenate([-sin, sin, jnp.zeros_like(ones)], axis=1)
    return cos_t, sin_t


def _band_attn_kernel(q_ref, kc_ref, kp_ref, vc_ref, vp_ref, o_ref, lse_ref, *, dil, span, has_prev):
    n = pl.program_id(1)
    scale = ATT_HEAD_DIM ** -0.5
    qi = lax.broadcasted_iota(jnp.int32, (ATT_BLOCK, ATT_BLOCK), 0)
    ki = lax.broadcasted_iota(jnp.int32, (ATT_BLOCK, ATT_BLOCK), 1)
    rel_c = qi - ki
    valid_c = (rel_c >= 0) & (rel_c <= span)
    rel_p = rel_c + ATT_BLOCK
    valid_p = (rel_p <= span) & (n > 0)
    nt = (((1,), (1,)), ((), ()))
    batch = 2 * HEADS_PER_GROUP
    problems = [(r, h) for r in range(dil) for h in range(HEADS_PER_GROUP)]
    for b0 in range(0, len(problems), batch):
        todo = problems[b0:b0 + batch]
        cols = [slice((r * HEADS_PER_GROUP + h) * ATT_HEAD_DIM, (r * HEADS_PER_GROUP + h + 1) * ATT_HEAD_DIM)
                for r, h in todo]
        qs = [q_ref[:, cs] for cs in cols]
        s_c = [lax.dot_general(q, kc_ref[:, cs], nt, preferred_element_type=F32) for q, cs in zip(qs, cols)]
        s_c = [jnp.where(valid_c, s * scale, -jnp.inf) for s in s_c]
        m = [jnp.max(s, axis=-1, keepdims=True) for s in s_c]
        if has_prev:
            s_p = [lax.dot_general(q, kp_ref[:, cs], nt, preferred_element_type=F32) for q, cs in zip(qs, cols)]
            s_p = [jnp.where(valid_p, s * scale, -jnp.inf) for s in s_p]
            m = [jnp.maximum(mm, jnp.max(s, axis=-1, keepdims=True)) for mm, s in zip(m, s_p)]
        p_c = [jnp.exp(s - mm) for s, mm in zip(s_c, m)]
        den = [jnp.sum(p, axis=-1, keepdims=True) for p in p_c]
        o = [jnp.dot(p.astype(BF16), vc_ref[:, cs], preferred_element_type=F32) for p, cs in zip(p_c, cols)]
        if has_prev:
            p_p = [jnp.exp(s - mm) for s, mm in zip(s_p, m)]
            den = [d + jnp.sum(p, axis=-1, keepdims=True) for d, p in zip(den, p_p)]
            o = [oo + jnp.dot(p.astype(BF16), vp_ref[:, cs], preferred_element_type=F32)
                 for oo, p, cs in zip(o, p_p, cols)]
        for (r, h), oo, d, mm in zip(todo, o, den, m):
            rows = pl.ds(r, ATT_BLOCK, stride=dil) if dil > 1 else slice(None)
            o_ref[h, rows, :] = oo / d
            lse_ref[h, rows, :] = jnp.broadcast_to(mm + jnp.log(d), (ATT_BLOCK, ATT_HEAD_DIM))


def _band_attn_call(q, k, v, bsz, seq, window, dil, g):
    gw = HEADS_PER_GROUP * ATT_HEAD_DIM
    t_pad = q.shape[0] * dil
    span = window // dil
    assert span <= ATT_BLOCK and seq % (dil * ATT_BLOCK) == 0
    nb = seq // (dil * ATT_BLOCK)
    cur = pl.BlockSpec((ATT_BLOCK, dil * gw), lambda b, n: (b * nb + n, 0))
    prev = pl.BlockSpec((ATT_BLOCK, dil * gw), lambda b, n: (b * nb + jnp.maximum(n - 1, 0), 0))
    out = pl.BlockSpec((HEADS_PER_GROUP, ATT_BLOCK * dil, ATT_HEAD_DIM), lambda b, n: (0, b * nb + n, 0))
    kern = functools.partial(_band_attn_kernel, dil=dil, span=span, has_prev=nb > 1)
    return pl.pallas_call(
        kern, grid=(bsz, nb),
        in_specs=[cur, cur, prev, cur, prev], out_specs=[out, out],
        out_shape=[jax.ShapeDtypeStruct((HEADS_PER_GROUP, t_pad, ATT_HEAD_DIM), F32)] * 2,
        compiler_params=_params("parallel", "parallel"), name=f"band_attn_g{g}",
    )(q, k, k, v, v)


def _sample_attn_kernel(q_ref, kc_ref, vc_ref, kn_ref, vn_ref, o_ref, lse_ref, *, dil, lq):
    scale = ATT_HEAD_DIM ** -0.5
    n_cache = kc_ref.shape[1]
    row = lax.broadcasted_iota(jnp.int32, (n_cache, HEADS_PER_GROUP, 1), 0)
    for i in range(lq):
        res = i if dil > 1 else 0
        new_rows = range(i + 1) if dil == 1 else (i,)
        q = q_ref[0, i]
        s_c = jnp.sum(kc_ref[0, :, res] * q[None], axis=-1, keepdims=True) * scale
        if dil == 1:
            s_c = jnp.where(row >= i, s_c, -jnp.inf)
        s_new = [jnp.sum(kn_ref[0, j] * q, axis=-1, keepdims=True) * scale for j in new_rows]
        m = jnp.max(s_c, axis=0)
        for s in s_new:
            m = jnp.maximum(m, s)
        p_c = jnp.exp(s_c - m[None])
        den = jnp.sum(p_c, axis=0)
        o = jnp.sum(p_c * vc_ref[0, :, res], axis=0)
        for j, s in zip(new_rows, s_new):
            p = jnp.exp(s - m)
            den = den + p
            o = o + p * vn_ref[0, j]
        o_ref[0, i] = o / den
        lse_ref[0, i] = jnp.broadcast_to(m + jnp.log(den), (HEADS_PER_GROUP, ATT_HEAD_DIM))


def _decode_mem_attn_kernel(q_ref, k_ref, v_ref, o_ref):
    scale = q_ref.shape[-1] ** -0.5
    for i in range(q_ref.shape[1]):
        q = q_ref[0, i]
        s = jnp.sum(k_ref[0] * q[None], axis=-1, keepdims=True) * scale
        m = jnp.max(s, axis=0)
        p = jnp.exp(s - m[None])
        den = jnp.sum(p, axis=0)
        o_ref[0, i] = (jnp.sum(p * v_ref[0], axis=0) / den).astype(o_ref.dtype)


def _decode_mem_attn_call(q, mem_k, mem_v):
    b, lq, nh, e = q.shape
    qs = pl.BlockSpec((1, lq, nh, e), lambda i: (i, 0, 0, 0))
    ms = pl.BlockSpec((1, mem_k.shape[1], nh, e), lambda i: (i, 0, 0, 0))
    return pl.pallas_call(
        _decode_mem_attn_kernel, grid=(b,), in_specs=[qs, ms, ms], out_specs=qs,
        out_shape=jax.ShapeDtypeStruct((b, lq, nh, e), F32),
        compiler_params=_params("parallel"), name="mem_attn_sample",
    )(q, mem_k, mem_v)


def _sample_attn_call(q, k_new, v_new, k_buf, v_buf, window, dil, g):
    b, lq, nh, e = q.shape
    assert k_buf.shape[1] == window and window % dil == 0 and (dil == 1 or dil >= lq) and lq <= window // dil
    n_res = min(dil, lq)
    n_cache = window // dil
    cache = lambda a: a.reshape(b, n_cache, dil, nh, e)
    new = pl.BlockSpec((1, lq, nh, e), lambda i: (i, 0, 0, 0))
    buf = pl.BlockSpec((1, n_cache, n_res, nh, e), lambda i: (i, 0, 0, 0, 0))
    kern = functools.partial(_sample_attn_kernel, dil=dil, lq=lq)
    return pl.pallas_call(
        kern, grid=(b,), in_specs=[new, buf, buf, new, new], out_specs=[new, new],
        out_shape=[jax.ShapeDtypeStruct((b, lq, nh, e), F32)] * 2,
        compiler_params=_params("parallel"), name=f"sample_attn_g{g}",
    )(q, cache(k_buf), cache(v_buf), k_new, v_new)


def _combine_kernel(*refs):
    n_grp = len(DIL_CONFIGS)
    o_refs, l_refs, out_ref = refs[:n_grp], refs[n_grp:2 * n_grp], refs[2 * n_grp]
    for h in range(HEADS_PER_GROUP):
        ls = [r[h] for r in l_refs]
        m = functools.reduce(jnp.maximum, ls)
        ws = [jnp.exp(l - m) for l in ls]
        tot = functools.reduce(lambda a, b: a + b, ws)
        acc = functools.reduce(lambda a, b: a + b, [w * r[h] for w, r in zip(ws, o_refs)])
        out_ref[:, h * ATT_HEAD_DIM:(h + 1) * ATT_HEAD_DIM] = (acc / tot).astype(out_ref.dtype)


def _combine_call(outs, lses, tm=512):
    nh, t, e = outs[0].shape
    row = pl.BlockSpec((nh, tm, e), lambda i: (0, i, 0))
    return pl.pallas_call(
        _combine_kernel, grid=(t // tm,), in_specs=[row] * (2 * len(outs)),
        out_specs=pl.BlockSpec((tm, nh * e), lambda i: (i, 0)),
        out_shape=jax.ShapeDtypeStruct((t, nh * e), BF16),
        compiler_params=_params("parallel"), name="combine_groups",
    )(*outs, *lses)


def _ssd_kernel(*refs, n_chunks, valid_len, has_state):
    if has_state:
        (z_ref, xbc_ref, dt_ref, h0_ref, cprev_ref, cw_ref, cb_ref, dtb_ref, alog_ref, dskip_ref, nw_ref, expand_ref,
         y_ref, hfin_ref, ctail_ref, ht_ref, xext_ref) = refs
    else:
        (z_ref, xbc_ref, dt_ref, cw_ref, cb_ref, dtb_ref, alog_ref, dskip_ref, nw_ref, expand_ref,
         y_ref, hfin_ref, ctail_ref, ht_ref, xext_ref) = refs
    c = pl.program_id(1)
    lc = SSM_CHUNK
    n_st = SSM_STATE
    gw = ht_ref.shape[1] // SSM_GROUPS
    d_inner = ht_ref.shape[1]
    heads_per_group = gw // SSM_HEAD_DIM
    n_tr = d_inner // LANES

    @pl.when(c == 0)
    def _():
        if has_state:
            xext_ref[0:SUBLANES, :] = cprev_ref[0]
            for k in range(n_tr):
                ht_ref[:, k * LANES:(k + 1) * LANES] = h0_ref[0, k * LANES:(k + 1) * LANES, :].T
        else:
            xext_ref[0:SUBLANES, :] = jnp.zeros((SUBLANES, xext_ref.shape[1]), F32)
            ht_ref[...] = jnp.zeros(ht_ref.shape, F32)

    xext_ref[SUBLANES:SUBLANES + lc, :] = xbc_ref[...]

    def conv_silu(c0, width):
        window = xext_ref[0:SUBLANES + lc, c0:c0 + width]
        acc = cb_ref[:, c0:c0 + width] + cw_ref[CONV_W - 1:CONV_W, c0:c0 + width] * window[SUBLANES:]
        for s in range(1, CONV_W):
            shifted = pltpu.roll(window, s, 0)[SUBLANES:]
            acc = acc + cw_ref[CONV_W - 1 - s:CONV_W - s, c0:c0 + width] * shifted
        return acc * _sigmoid(acc)

    dtr = dt_ref[...] + dtb_ref[...]
    dt = jnp.maximum(dtr, 0.0) + jnp.log(1.0 + jnp.exp(-jnp.abs(dtr)))
    if valid_len < lc:
        trow = lax.broadcasted_iota(jnp.int32, dt.shape, 0)
        dt = jnp.where(trow < valid_len, dt, 0.0)
    a = dt * (-jnp.exp(alog_ref[...]))
    ti = lax.broadcasted_iota(jnp.int32, (lc, lc), 0)
    si = lax.broadcasted_iota(jnp.int32, (lc, lc), 1)
    tri = si <= ti
    f32_dot = lambda lhs, rhs: jnp.dot(lhs, rhs, preferred_element_type=F32)
    tri_b = tri.astype(BF16)
    acum = functools.reduce(lambda u, v: u + v, [f32_dot(tri_b, p) for p in _split3(a)])
    tri_t = (ti <= si).astype(BF16)
    acum_t = functools.reduce(lambda u, v: u + v, [f32_dot(p, tri_t) for p in _split3(a.T)])
    dt3, ac3 = _split3(dt), _split3(acum)
    lane = lax.broadcasted_iota(jnp.int32, (lc, LANES), 1)
    nt = (((1,), (1,)), ((), ()))

    for g in range(SSM_GROUPS):
        c0 = g * gw
        xs = conv_silu(c0, gw)
        bm = conv_silu(d_inner + g * n_st, n_st)
        cm = conv_silu(d_inner + SSM_GROUPS * n_st + g * n_st, n_st)
        expand = expand_ref[:, c0:c0 + gw]
        dt_x = functools.reduce(lambda u, v: u + v, [f32_dot(p, expand) for p in dt3])
        ac_x = functools.reduce(lambda u, v: u + v, [f32_dot(p, expand) for p in ac3])
        xdt = xs * dt_x
        xdt_b = xdt.astype(BF16)
        cm_b = cm.astype(BF16)
        cb = lax.dot_general(cm_b, bm.astype(BF16), nt, preferred_element_type=F32)
        h_old = ht_ref[:, c0:c0 + gw]
        y = jnp.dot(cm_b, h_old.astype(BF16), preferred_element_type=F32) * jnp.exp(ac_x)
        diag = []
        for k in range(gw // LANES):
            x_pair = xdt_b[:, k * LANES:(k + 1) * LANES]
            y_pair = None
            for hh in range(LANES // SSM_HEAD_DIM):
                h = g * heads_per_group + k * (LANES // SSM_HEAD_DIM) + hh
                seg = acum[:, h:h + 1] - acum_t[h:h + 1, :]
                lmat = jnp.exp(jnp.where(tri, seg, -jnp.inf))
                m_b = (cb * lmat).astype(BF16)
                in_head = (lane >= hh * SSM_HEAD_DIM) & (lane < (hh + 1) * SSM_HEAD_DIM)
                part = jnp.dot(m_b, jnp.where(in_head, x_pair, jnp.zeros_like(x_pair)), preferred_element_type=F32)
                y_pair = part if y_pair is None else y_pair + part
            diag.append(y_pair)
        y = y + jnp.concatenate(diag, axis=1) + dskip_ref[:, c0:c0 + gw] * xs
        zg = z_ref[:, c0:c0 + gw].astype(F32)
        y = y * (zg * _sigmoid(zg))
        y = y * lax.rsqrt(jnp.mean(y * y, axis=-1, keepdims=True) + RMS_EPS)
        y_ref[:, c0:c0 + gw] = (y * nw_ref[:, c0:c0 + gw]).astype(y_ref.dtype)
        a_last = ac_x[lc - 1:lc, :]
        xw = (xdt * jnp.exp(a_last - ac_x)).astype(BF16)
        ht_ref[:, c0:c0 + gw] = h_old * jnp.exp(a_last) + jnp.dot(bm.T.astype(BF16), xw, preferred_element_type=F32)

    xext_ref[0:SUBLANES, :] = xext_ref[lc:lc + SUBLANES, :]

    @pl.when(c == n_chunks - 1)
    def _():
        ctail_ref[0] = xext_ref[0:SUBLANES, :]
        for k in range(n_tr):
            hfin_ref[0, k * LANES:(k + 1) * LANES, :] = ht_ref[:, k * LANES:(k + 1) * LANES].T


def _ssd_call(z, xbc, dt_raw, n_seq, n_chunks, valid_len, state, conv_w, conv_b, dt_bias, a_log, d_skip, norm_w,
              out_rows, name):
    d_inner = z.shape[1]
    conv_dim = xbc.shape[1]
    n_heads = d_inner // SSM_HEAD_DIM
    lc = SSM_CHUNK
    pad_h = lambda v: jnp.pad(v.astype(F32), (0, LANES - n_heads)).reshape(1, LANES)
    rows = lambda w: pl.BlockSpec((lc, w), lambda b, c: (b * n_chunks + c, 0))
    const = lambda r, w: pl.BlockSpec((r, w), lambda b, c: (0, 0))
    per_seq = lambda r, w: pl.BlockSpec((1, r, w), lambda b, c: (b, 0, 0))
    args = [z, xbc, dt_raw]
    specs = [rows(d_inner), rows(conv_dim), rows(LANES)]
    if state is not None:
        args += list(state)
        specs += [per_seq(d_inner, SSM_STATE), per_seq(SUBLANES, conv_dim)]
    args += [conv_w, conv_b.reshape(1, conv_dim), pad_h(dt_bias), pad_h(a_log),
             jnp.repeat(d_skip.astype(F32), SSM_HEAD_DIM).reshape(1, d_inner), norm_w.reshape(1, d_inner),
             (jnp.arange(LANES)[:, None] == jnp.arange(d_inner)[None, :] // SSM_HEAD_DIM).astype(BF16)]
    specs += [const(CONV_W, conv_dim), const(1, conv_dim), const(1, LANES), const(1, LANES),
              const(1, d_inner), const(1, d_inner), const(LANES, d_inner)]
    kern = functools.partial(_ssd_kernel, n_chunks=n_chunks, valid_len=valid_len, has_state=state is not None)
    return pl.pallas_call(
        kern, grid=(n_seq, n_chunks), in_specs=specs,
        out_specs=[rows(d_inner), per_seq(d_inner, SSM_STATE), per_seq(SUBLANES, conv_dim)],
        out_shape=[jax.ShapeDtypeStruct((out_rows, d_inner), BF16),
                   jax.ShapeDtypeStruct((n_seq, d_inner, SSM_STATE), F32),
                   jax.ShapeDtypeStruct((n_seq, SUBLANES, conv_dim), F32)],
        scratch_shapes=[pltpu.VMEM((SSM_STATE, d_inner), F32), pltpu.VMEM((lc + 2 * SUBLANES, conv_dim), F32)],
        compiler_params=_params("parallel", "arbitrary"), name=name,
    )(*args)


def _mem_attn_kernel(q_ref, k_ref, v_ref, o_ref):
    hd = q_ref.shape[1] // MEM_HEADS
    scale = hd ** -0.5
    cols = [slice(h * hd, (h + 1) * hd) for h in range(MEM_HEADS)]
    nt = (((1,), (1,)), ((), ()))
    s = [lax.dot_general(q_ref[:, c], k_ref[:, c].astype(BF16), nt, preferred_element_type=F32) * scale for c in cols]
    m = [jnp.max(x, axis=-1, keepdims=True) for x in s]
    p = [jnp.exp(x - mm) for x, mm in zip(s, m)]
    den = [jnp.sum(x, axis=-1, keepdims=True) for x in p]
    o = [jnp.dot(x.astype(BF16), v_ref[:, c].astype(BF16), preferred_element_type=F32) for x, c in zip(p, cols)]
    for c, oo, dd in zip(cols, o, den):
        o_ref[:, c] = (oo / dd).astype(o_ref.dtype)


def _mem_attn_call(q, mem_k, mem_v, n_seq, lq, tq, out_rows, name):
    d = q.shape[1]
    n_mem = mem_k.shape[0] // n_seq
    nq = lq // tq
    qs = pl.BlockSpec((tq, d), lambda b, i: (b * nq + i, 0))
    ks = pl.BlockSpec((n_mem, d), lambda b, i: (b, 0))
    return pl.pallas_call(
        _mem_attn_kernel, grid=(n_seq, nq), in_specs=[qs, ks, ks], out_specs=qs,
        out_shape=jax.ShapeDtypeStruct((out_rows, d), BF16),
        compiler_params=_params("parallel", "parallel"), name=name,
    )(q, mem_k, mem_v)


def _router_kernel(x_ref, w_ref, b_ref, idx_ref, gate_ref):
    logits = jnp.dot(x_ref[...], w_ref[...], precision=HI, preferred_element_type=F32) + b_ref[...]
    lane = lax.broadcasted_iota(jnp.int32, logits.shape, 1)
    idx_out = jnp.zeros(logits.shape, jnp.int32)
    vals = []
    for k in range(TOP_K):
        m = jnp.max(logits, axis=-1, keepdims=True)
        pick = jnp.min(jnp.where(logits == m, lane, LANES), axis=-1, keepdims=True)
        idx_out = jnp.where(lane == k, pick, idx_out)
        logits = jnp.where(lane == pick, -jnp.inf, logits)
        vals.append(m)
    exps = [jnp.exp(v - vals[0]) for v in vals]
    tot = functools.reduce(lambda a, b: a + b, exps)
    gate_out = jnp.zeros(logits.shape, F32)
    for k in range(TOP_K):
        gate_out = jnp.where(lane == k, exps[k] / tot, gate_out)
    idx_ref[...] = idx_out
    gate_ref[...] = gate_out


def _router_call(x, w_router, b_router, tm=256):
    t, d = x.shape
    n_exp = w_router.shape[1]
    w = jnp.pad(w_router.astype(F32), ((0, 0), (0, LANES - n_exp)))
    b = jnp.pad(b_router.astype(F32), (0, LANES - n_exp), constant_values=-jnp.inf).reshape(1, LANES)
    out = pl.BlockSpec((tm, LANES), lambda i: (i, 0))
    return pl.pallas_call(
        _router_kernel, grid=(t // tm,),
        in_specs=[pl.BlockSpec((tm, d), lambda i: (i, 0)), pl.BlockSpec((d, LANES), lambda i: (0, 0)),
                  pl.BlockSpec((1, LANES), lambda i: (0, 0))],
        out_specs=[out, out],
        out_shape=[jax.ShapeDtypeStruct((t, LANES), jnp.int32), jax.ShapeDtypeStruct((t, LANES), F32)],
        compiler_params=_params("parallel"), name="router",
    )(x, w, b)


def _dispatch_kernel(pend_ref, slot_ref, x_ref, xg_hbm, sbuf, zbuf, sem, zsem, *, n_exp, block_rows, n_tiles):
    i = pl.program_id(0)
    s = i % 2
    tm = x_ref.shape[0]

    def zero_copy(e):
        start = pl.multiple_of(pend_ref[e] - block_rows, block_rows)
        return pltpu.make_async_copy(zbuf, xg_hbm.at[pl.ds(start, block_rows)], zsem)

    def has_rows(e):
        return pend_ref[e] > (pend_ref[e - 1] if e else 0)

    def wait_rows(s_):
        for _ in range(TOP_K):
            pltpu.make_async_copy(sbuf.at[s_], xg_hbm.at[pl.ds(0, tm)], sem.at[s_]).wait()

    @pl.when(i == 0)
    def _():
        zbuf[...] = jnp.zeros(zbuf.shape, F32)
        for e in range(n_exp):
            pl.when(has_rows(e))(lambda e=e: zero_copy(e).start())
        for e in range(n_exp):
            pl.when(has_rows(e))(lambda e=e: zero_copy(e).wait())

    @pl.when(i >= 2)
    def _():
        wait_rows(s)

    sbuf[s] = x_ref[...]
    for r in range(tm):
        for k in range(TOP_K):
            pltpu.make_async_copy(sbuf.at[s, pl.ds(r, 1)], xg_hbm.at[pl.ds(slot_ref[0, 0, r * TOP_K + k], 1)],
                                  sem.at[s]).start()

    @pl.when(i == n_tiles - 1)
    def _():
        wait_rows(s)
        if n_tiles > 1:
            wait_rows(1 - s)


def _dispatch_call(x, slot, pad_end, n_slots, tm=256):
    t, d = x.shape
    n_tiles = t // tm
    grid_spec = pltpu.PrefetchScalarGridSpec(
        num_scalar_prefetch=1, grid=(n_tiles,),
        in_specs=[pl.BlockSpec((1, 1, tm * TOP_K), lambda i, pe: (i, 0, 0), memory_space=pltpu.SMEM),
                  pl.BlockSpec((tm, d), lambda i, pe: (i, 0))],
        out_specs=pl.BlockSpec(memory_space=pl.ANY),
        scratch_shapes=[pltpu.VMEM((2, tm, d), F32), pltpu.VMEM((MOE_TM, d), F32),
                        pltpu.SemaphoreType.DMA((2,)), pltpu.SemaphoreType.DMA(())],
    )
    kern = functools.partial(_dispatch_kernel, n_exp=pad_end.shape[0], block_rows=MOE_TM, n_tiles=n_tiles)
    return pl.pallas_call(
        kern, grid_spec=grid_spec, out_shape=jax.ShapeDtypeStruct((n_slots, d), F32),
        compiler_params=_params("arbitrary"), name="moe_dispatch",
    )(pad_end, slot.reshape(n_tiles, 1, tm * TOP_K), x)


def _moe_kernel(nused_ref, bexp_ref, x_ref, wg_ref, wu_ref, wd_ref, bg_ref, bu_ref, bd_ref, y_ref, *, nj):
    i = pl.program_id(0)
    j = pl.program_id(1)

    @pl.when(i < nused_ref[0])
    def _():
        @pl.when(j == 0)
        def _():
            y_ref[...] = jnp.broadcast_to(bd_ref[0], y_ref.shape)

        x = x_ref[...].astype(BF16)
        gate = jnp.minimum(jnp.dot(x, wg_ref[0], preferred_element_type=F32) + bg_ref[0], SWIGLU_LIMIT)
        up = jnp.clip(jnp.dot(x, wu_ref[0], preferred_element_type=F32) + bu_ref[0], -SWIGLU_LIMIT, SWIGLU_LIMIT)
        act = (up + 1.0) * gate * _sigmoid(gate * SWIGLU_ALPHA)
        y_ref[...] += jnp.dot(act.astype(BF16), wd_ref[0], preferred_element_type=F32)


def _moe_call(x, top_idx, w_gate_up, b_gate_up, w_down, b_down):
    t, d = x.shape
    n_exp, _, two_h = w_gate_up.shape
    d_exp = two_h // 2
    tm, tn = MOE_TM, MOE_TN
    nj = d_exp // tn
    n_assign = t * TOP_K
    n_blocks = n_assign // tm + n_exp
    n_slots = n_blocks * tm

    flat_e = top_idx.reshape(n_assign)
    onehot = (flat_e[:, None] == jnp.arange(n_exp, dtype=jnp.int32)[None, :]).astype(jnp.int32)
    counts = jnp.sum(onehot, axis=0)
    rank = jnp.take_along_axis(jnp.cumsum(onehot, axis=0), flat_e[:, None], axis=1)[:, 0] - 1
    padded = (counts + tm - 1) // tm * tm
    pad_end = jnp.cumsum(padded)
    slot = (pad_end - padded)[flat_e] + rank
    n_used = (pad_end[-1] // tm).astype(jnp.int32)
    blk = jnp.minimum(jnp.arange(n_blocks, dtype=jnp.int32), n_used - 1) * tm
    block_expert = jnp.minimum(jnp.sum((pad_end[None, :] <= blk[:, None]).astype(jnp.int32), axis=1), n_exp - 1)
    xg = _dispatch_call(x, slot, pad_end.astype(jnp.int32), n_slots)

    live = lambda i, nu: i < nu[0]
    rows = pl.BlockSpec((tm, d), lambda i, j, nu, be: (jnp.minimum(i, nu[0] - 1), 0))
    grid_spec = pltpu.PrefetchScalarGridSpec(
        num_scalar_prefetch=2, grid=(n_blocks, nj),
        in_specs=[
            rows,
            pl.BlockSpec((1, d, tn), lambda i, j, nu, be: (be[i], 0, jnp.where(live(i, nu), j, nj - 1))),
            pl.BlockSpec((1, d, tn), lambda i, j, nu, be: (be[i], 0, nj + jnp.where(live(i, nu), j, nj - 1))),
            pl.BlockSpec((1, tn, d), lambda i, j, nu, be: (be[i], jnp.where(live(i, nu), j, nj - 1), 0)),
            pl.BlockSpec((1, 1, tn), lambda i, j, nu, be: (be[i], 0, jnp.where(live(i, nu), j, nj - 1))),
            pl.BlockSpec((1, 1, tn), lambda i, j, nu, be: (be[i], 0, nj + jnp.where(live(i, nu), j, nj - 1))),
            pl.BlockSpec((1, 1, d), lambda i, j, nu, be: (be[i], 0, 0)),
        ],
        out_specs=rows,
    )
    bgu = b_gate_up.astype(F32).reshape(n_exp, 1, two_h)
    y_slots = pl.pallas_call(
        functools.partial(_moe_kernel, nj=nj), grid_spec=grid_spec,
        out_shape=jax.ShapeDtypeStruct((n_slots, d), F32),
        compiler_params=_params("arbitrary", "arbitrary"), name="moe_experts",
    )(n_used.reshape(1), block_expert, xg, w_gate_up, w_gate_up, w_down, bgu, bgu,
      b_down.astype(F32).reshape(n_exp, 1, d))
    return y_slots, slot


def _moe_out_kernel(slotc_ref, slotn_ref, y_hbm, gate_ref, h_ref, g_ref, b_ref, head_ref, tail_ref, ybuf, sem,
                    *, n_head, n_tiles):
    i = pl.program_id(0)
    s = i % 2
    tm = h_ref.shape[0]

    def fetch(tab_ref, s_):
        for r in range(TOP_K * tm):
            pltpu.make_async_copy(y_hbm.at[pl.ds(tab_ref[0, 0, r], 1)], ybuf.at[s_, pl.ds(r, 1)], sem.at[s_]).start()

    @pl.when(i == 0)
    def _():
        fetch(slotc_ref, 0)

    @pl.when(i + 1 < n_tiles)
    def _():
        fetch(slotn_ref, 1 - s)

    pltpu.make_async_copy(y_hbm.at[pl.ds(0, TOP_K * tm)], ybuf.at[s], sem.at[s]).wait()
    gates = gate_ref[...]
    y = gates[:, 0:1] * ybuf[s, 0:tm]
    for k in range(1, TOP_K):
        y = y + gates[:, k:k + 1] * ybuf[s, k * tm:(k + 1) * tm]
    out = _layer_norm_rows(DN_ALPHA * h_ref[...] + y, g_ref[...], b_ref[...])

    @pl.when(pl.program_id(0) < n_head)
    def _():
        head_ref[...] = out

    @pl.when(pl.program_id(0) >= n_head)
    def _():
        tail_ref[...] = out


def _moe_out_call(y_slots, slot, gates, h, g, b, n_head_rows, tm=256):
    t, d = h.shape
    nb = t // tm
    n_head = n_head_rows // tm
    row = pl.BlockSpec((tm, d), lambda i: (i, 0))
    vec = pl.BlockSpec((1, d), lambda i: (0, 0))
    table = jnp.transpose(slot.reshape(nb, tm, TOP_K), (0, 2, 1)).reshape(nb, 1, TOP_K * tm)
    smem = lambda imap: pl.BlockSpec((1, 1, TOP_K * tm), imap, memory_space=pltpu.SMEM)
    return pl.pallas_call(
        functools.partial(_moe_out_kernel, n_head=n_head, n_tiles=nb), grid=(nb,),
        in_specs=[smem(lambda i: (i, 0, 0)), smem(lambda i: (jnp.minimum(i + 1, nb - 1), 0, 0)),
                  pl.BlockSpec(memory_space=pl.ANY), pl.BlockSpec((tm, LANES), lambda i: (i, 0)), row, vec, vec],
        out_specs=[pl.BlockSpec((tm, d), lambda i: (jnp.minimum(i, n_head - 1), 0)),
                   pl.BlockSpec((tm, d), lambda i: (jnp.maximum(i - n_head, 0), 0))],
        out_shape=[jax.ShapeDtypeStruct((n_head_rows, d), F32), jax.ShapeDtypeStruct((t - n_head_rows, d), F32)],
        scratch_shapes=[pltpu.VMEM((2, TOP_K * tm, d), F32), pltpu.SemaphoreType.DMA((2,))],
        compiler_params=_params("arbitrary"), name="moe_combine_ln",
    )(table, table, y_slots, gates, h, g.reshape(1, d), b.reshape(1, d))


def kernel(x_prompt, x_sample, state_conv, state_ssm, cache_k_w128, cache_v_w128, cache_k_w512, cache_v_w512, cache_k_w2048, cache_v_w2048, cache_mem_k, cache_mem_v, mem_prompt, ln_in_g, ln_in_b, w_in, conv_w, conv_b, dt_bias, a_log, d_skip, ssm_norm_w, w_branch_ssm, w_branch_att, w_mix_out, ln1_g, ln1_b, w_mem_q, w_mem_k, w_mem_v, w_mem_o, ln2_g, ln2_b, w_router, b_router, w_gate_up, b_gate_up, w_down, b_down, ln3_g, ln3_b):
    assert w_in.shape[0] == DEPTH
    bp, lp, d = x_prompt.shape
    bs, ls, _ = x_sample.shape
    n_p, n_s = bp * lp, bs * ls
    t_real = n_p + n_s
    t_pad = -(-t_real // ROW_TILE) * ROW_TILE
    d_inner = ssm_norm_w.shape[1]
    conv_dim = conv_w.shape[2]
    n_heads = d_inner // SSM_HEAD_DIM
    gw = HEADS_PER_GROUP * ATT_HEAD_DIM
    att_w = len(DIL_CONFIGS) * gw
    n_mem = mem_prompt.shape[1]
    lc = SSM_CHUNK
    assert lp % lc == 0 and ls <= lc and n_s % SUBLANES == 0

    def sample_rows(a):
        return a[n_p:n_p + n_s]

    def with_sample_rows(a, rows):
        tail = jnp.concatenate([rows.astype(a.dtype), jnp.zeros((t_pad - t_real, a.shape[1]), a.dtype)], axis=0)
        return lax.dynamic_update_slice(a, tail, (n_p, 0))

    def with_sample_heads(a, rows):
        nh, _, e = a.shape
        rows = jnp.transpose(rows.reshape(n_s, nh, e), (1, 0, 2)).astype(a.dtype)
        tail = jnp.concatenate([rows, jnp.zeros((nh, t_pad - t_real, e), a.dtype)], axis=1)
        return lax.dynamic_update_slice(a, tail, (0, n_p, 0))

    x_tail = jnp.concatenate([x_sample.reshape(n_s, d), jnp.zeros((t_pad - t_real, d), x_sample.dtype)], axis=0)
    h0, h0_b = _ln_call(x_prompt.reshape(n_p, d), x_tail, ln_in_g, ln_in_b)

    o_z, o_xbc, o_dt, o_qkv, o_g = 0, d_inner, d_inner + conv_dim, d_inner + conv_dim + n_heads, \
        d_inner + conv_dim + n_heads + 3 * att_w
    w_cols = lambda a, b: w_in[0, :, a:b].astype(BF16)
    n_exp, _, two_h = w_gate_up.shape[1:]
    wgu_src = w_gate_up[0].reshape(n_exp * d, two_h)
    wd_src = w_down[0].reshape(n_exp * (two_h // 2), d)
    steps = lambda n_cols, tn: (n_cols // tn) * (t_pad // 512)
    gu_rows, wd_rows = 128, 256
    gu_chunks, wd_chunks = wgu_src.shape[0] // gu_rows, wd_src.shape[0] // wd_rows
    n_a = min(steps(conv_dim, 1024), gu_chunks)
    n_b = min(steps(d_inner, 1024), gu_chunks - n_a)
    n_c = min(steps(2 * d, 1024), wd_chunks)
    xbc, wgu_b = _mm_call(h0_b, w_cols(o_xbc, o_dt), F32, 512, 1024, "in_xbc", (wgu_src, None, 0, n_a, gu_rows))
    if n_b:
        z, wgu_b = _mm_call(h0_b, w_cols(o_z, o_xbc), BF16, 512, 1024, "in_z", (wgu_src, wgu_b, n_a, n_b, gu_rows))
    else:
        z = _mm_call(h0_b, w_cols(o_z, o_xbc), BF16, 512, 1024, "in_z")
    if gu_chunks - n_a - n_b:
        wgu_b = _cast_call(wgu_src, wgu_b, n_a + n_b, gu_chunks - n_a - n_b, gu_rows)
    w_dt = jnp.pad(w_cols(o_dt, o_qkv), ((0, 0), (0, LANES - n_heads)))
    dt_raw = _mm_call(h0_b, w_dt, F32, 512, LANES, "in_dt")
    qkv = _mm_heads_call(h0_b, w_cols(o_qkv, o_g), F32, 512, att_w, "in_qkv")
    gates, wd_b = _mm_call(h0_b, w_cols(o_g, w_in.shape[2]), BF16, 512, 1024, "in_gates",
                           (wd_src, None, 0, n_c, wd_rows))
    if wd_chunks - n_c:
        wd_b = _cast_call(wd_src, wd_b, n_c, wd_chunks - n_c, wd_rows)
    wgu_b = wgu_b.reshape(n_exp, d, two_h)
    wd_b = wd_b.reshape(n_exp, two_h // 2, d)

    ssd_w = (conv_w[0], conv_b[0], dt_bias[0], a_log[0], d_skip[0], ssm_norm_w[0])
    y_ssm, ssm_p, conv_tail = _ssd_call(z, xbc, dt_raw, bp, lp // lc, lc, None, *ssd_w, out_rows=t_pad,
                                        name="ssd_prompt")

    def pad_seq(a):
        return jnp.pad(sample_rows(a).reshape(bs, ls, a.shape[1]), ((0, 0), (0, lc - ls), (0, 0))).reshape(bs * lc, a.shape[1])

    conv_prev = jnp.pad(state_conv[0], ((0, 0), (SUBLANES - (CONV_W - 1), 0), (0, 0)))
    y_s, ssm_s, _ = _ssd_call(pad_seq(z), pad_seq(xbc), pad_seq(dt_raw), bs, 1, ls,
                              (state_ssm[0].reshape(bs, d_inner, SSM_STATE), conv_prev), *ssd_w,
                              out_rows=bs * lc, name="ssd_sample")
    y_ssm = with_sample_rows(y_ssm, y_s.reshape(bs, lc, d_inner)[:, :ls].reshape(n_s, d_inner))

    pos = jnp.concatenate([jnp.tile(jnp.arange(lp, dtype=jnp.int32), bp),
                           jnp.tile(PAST_LEN + jnp.arange(ls, dtype=jnp.int32), bs),
                           jnp.zeros((t_pad - t_real,), jnp.int32)])
    cos_t, sin_t = _rotary_tables(pos)
    split = _rotary_call(qkv, cos_t, sin_t, n_p)
    n_grp = len(DIL_CONFIGS)
    k_head, v_head, k_tail, v_tail, q_tail, q_view, k_view, v_view = [split[a * n_grp:(a + 1) * n_grp]
                                                                      for a in range(8)]
    caches = ((cache_k_w128, cache_v_w128), (cache_k_w512, cache_v_w512), (cache_k_w2048, cache_v_w2048))
    outs, lses, kv_p, kv_s = [], [], [], []
    heads = lambda a: a.reshape(bs, ls, HEADS_PER_GROUP, ATT_HEAD_DIM)
    for g, (window, dil) in enumerate(DIL_CONFIGS):
        o_p, l_p = _band_attn_call(q_view[g], k_view[g], v_view[g], bp, lp, window, dil, g)
        k_new, v_new = heads(k_tail[g][:n_s]), heads(v_tail[g][:n_s])
        o_s, l_s = _sample_attn_call(heads(q_tail[g][:n_s]), k_new, v_new, caches[g][0][0], caches[g][1][0],
                                     window, dil, g)
        outs.append(with_sample_heads(o_p, o_s))
        lses.append(with_sample_heads(l_p, l_s))
        keep = min(window, lp)
        for a in (k_head[g], v_head[g]):
            kv_p.append(a.reshape(bp, lp, HEADS_PER_GROUP, ATT_HEAD_DIM)[:, lp - keep:][None])
        kv_s += [k_new[None], v_new[None]]
    att = _combine_call(outs, lses)

    merged = _branch_call(y_ssm, att, gates, w_branch_ssm[0].astype(BF16), w_branch_att[0].astype(BF16))
    h1, h1_b = _mm_res_ln_call(merged, w_mix_out[0].astype(BF16), h0, ln1_g[0], ln1_b[0], "mix_out_ln1")

    mem_b = mem_prompt.reshape(bp * n_mem, d).astype(BF16)
    mem_k_p = _mm_call(mem_b, w_mem_k[0].astype(BF16), F32, 512, 1024, "mem_k")
    mem_v_p = _mm_call(mem_b, w_mem_v[0].astype(BF16), F32, 512, 1024, "mem_v")
    q_mem = _mm_call(h1_b, w_mem_q[0].astype(BF16), BF16, 512, 1024, "mem_q")
    o_mem = _mem_attn_call(q_mem, mem_k_p, mem_v_p, bp, lp, 512, t_pad, "mem_attn_prompt")
    q_s = sample_rows(q_mem).reshape(bs, ls, MEM_HEADS, d // MEM_HEADS).astype(F32)
    o_mem_s = _decode_mem_attn_call(q_s, cache_mem_k[0], cache_mem_v[0])
    o_mem = with_sample_rows(o_mem, o_mem_s.reshape(n_s, d))
    h2, _ = _mm_res_ln_call(o_mem, w_mem_o[0].astype(BF16), h1, ln2_g[0], ln2_b[0], "mem_o_ln2")

    idx_t, gate_t = _router_call(h2, w_router[0], b_router[0])
    y_slots, slot = _moe_call(h2, idx_t[:, :TOP_K], wgu_b, b_gate_up[0], wd_b, b_down[0])
    h3_head, h3_tail = _moe_out_call(y_slots, slot, gate_t, h2, ln3_g[0], ln3_b[0], n_p)

    y_prompt = h3_head.reshape(bp, lp, d)
    y_sample = h3_tail[:n_s].reshape(bs, ls, d)
    conv_p = conv_tail[:, SUBLANES - (CONV_W - 1):][None]
    xp_s = jnp.concatenate([state_conv[0].astype(xbc.dtype), sample_rows(xbc).reshape(bs, ls, conv_dim)], axis=1)
    conv_s = xp_s[:, -(CONV_W - 1):][None]
    state_shape = (n_heads, SSM_HEAD_DIM, SSM_STATE)
    ssm_p = ssm_p.reshape(1, bp, *state_shape)
    ssm_s = ssm_s.reshape(1, bs, *state_shape)
    mem_shape = (1, bp, n_mem, MEM_HEADS, d // MEM_HEADS)
    return (y_prompt, y_sample, conv_p, ssm_p, *kv_p, mem_k_p.reshape(mem_shape), mem_v_p.reshape(mem_shape),
            conv_s, ssm_s, *kv_s)
```

```python
import functools
import math

import jax
import jax.numpy as jnp
from jax import lax
from jax.experimental import pallas as pl
from jax.experimental.pallas import tpu as pltpu

F32 = jnp.float32
BF16 = jnp.bfloat16

PAST_LEN = 16384
SSM_HEAD_DIM = 64
SSM_GROUPS = 8
SSM_STATE = 128
CONV_W = 4
SSM_CHUNK = 128
RMS_EPS = 1e-5
DIL_CONFIGS = ((128, 1), (512, 4), (2048, 16))
HEADS_PER_GROUP = 4
ATT_HEAD_DIM = 128
ATT_BLOCK = 128
ROT_DIM = ATT_HEAD_DIM // 4
ROPE_THETA = 500000.0
MEM_HEADS = 4
TOP_K = 4
SWIGLU_LIMIT = 7.0
SWIGLU_ALPHA = 1.702
LN_EPS = 1e-5
DEPTH = 1
DN_ALPHA = (2.0 * DEPTH) ** 0.25

LANES = 128
SUBLANES = 8
ROW_TILE = 512
MOE_TM = 512
MOE_TN = 1024
VMEM_LIMIT = 56 * 1024 * 1024


def _params(*sem):
    return pltpu.CompilerParams(dimension_semantics=sem, vmem_limit_bytes=VMEM_LIMIT)


def _sigmoid(x):
    return 0.5 * jnp.tanh(0.5 * x) + 0.5


def _split3(x):
    hi = x.astype(BF16)
    r1 = x - hi.astype(F32)
    mid = r1.astype(BF16)
    lo = (r1 - mid.astype(F32)).astype(BF16)
    return hi, mid, lo


def _layer_norm_rows(x, g, b):
    mu = jnp.mean(x, axis=-1, keepdims=True)
    xc = x - mu
    var = jnp.mean(xc * xc, axis=-1, keepdims=True)
    return xc * lax.rsqrt(var + LN_EPS) * g + b


def _ln_kernel(xa_ref, xb_ref, g_ref, b_ref, of_ref, ob_ref, *, n_head):
    def emit(x_ref):
        y = _layer_norm_rows(x_ref[...], g_ref[...], b_ref[...])
        of_ref[...] = y
        ob_ref[...] = y.astype(BF16)

    pl.when(pl.program_id(0) < n_head)(lambda: emit(xa_ref))
    pl.when(pl.program_id(0) >= n_head)(lambda: emit(xb_ref))


def _ln_call(x_head, x_tail, g, b, tm=256):
    d = x_head.shape[1]
    n_head, n_tail = x_head.shape[0] // tm, x_tail.shape[0] // tm
    t = (n_head + n_tail) * tm
    row = pl.BlockSpec((tm, d), lambda i: (i, 0))
    vec = pl.BlockSpec((1, d), lambda i: (0, 0))
    return pl.pallas_call(
        functools.partial(_ln_kernel, n_head=n_head), grid=(n_head + n_tail,),
        in_specs=[pl.BlockSpec((tm, d), lambda i: (jnp.minimum(i, n_head - 1), 0)),
                  pl.BlockSpec((tm, d), lambda i: (jnp.maximum(i - n_head, 0), 0)), vec, vec],
        out_specs=[row, row],
        out_shape=[jax.ShapeDtypeStruct((t, d), F32), jax.ShapeDtypeStruct((t, d), BF16)],
        compiler_params=_params("arbitrary"), name="ln_in",
    )(x_head, x_tail, g.reshape(1, d), b.reshape(1, d))


def _mm_kernel(a_ref, w_ref, *rest, n_cast):
    o_ref = rest[-2] if n_cast else rest[-1]
    if n_cast:
        src_ref, dst_ref = rest[0], rest[-1]
        step = pl.program_id(0) * pl.num_programs(1) + pl.program_id(1)

        @pl.when(step < n_cast)
        def _():
            dst_ref[...] = src_ref[...].astype(dst_ref.dtype)

    o_ref[...] = jnp.dot(a_ref[...], w_ref[...], preferred_element_type=F32).astype(o_ref.dtype)


def _mm_call(a, w, out_dtype, tm, tn, name, cast=None):
    m, k = a.shape
    n = w.shape[1]
    tm = min(tm, m)
    ni = m // tm
    args = [a, w]
    in_specs = [pl.BlockSpec((tm, k), lambda j, i: (i, 0)), pl.BlockSpec((k, tn), lambda j, i: (0, j))]
    out_specs = pl.BlockSpec((tm, tn), lambda j, i: (i, j))
    out_shape = jax.ShapeDtypeStruct((m, n), out_dtype)
    aliases = {}
    n_cast = 0
    if cast is not None:
        src, dst, chunk0, n_cast, rows = cast
        assert 0 < n_cast <= (n // tn) * ni
        chunk = pl.BlockSpec((rows, src.shape[1]), lambda j, i: (chunk0 + jnp.minimum(j * ni + i, n_cast - 1), 0))
        args.append(src)
        in_specs.append(chunk)
        if dst is not None:
            aliases = {len(args): 1}
            args.append(dst)
            in_specs.append(pl.BlockSpec(memory_space=pl.ANY))
        out_specs = [out_specs, chunk]
        out_shape = [out_shape, jax.ShapeDtypeStruct(src.shape, BF16)]
    return pl.pallas_call(
        functools.partial(_mm_kernel, n_cast=n_cast), grid=(n // tn, ni),
        in_specs=in_specs, out_specs=out_specs, out_shape=out_shape, input_output_aliases=aliases,
        compiler_params=_params("arbitrary", "arbitrary") if n_cast else _params("parallel", "parallel"), name=name,
    )(*args)


def _cast_kernel(src_ref, _, dst_ref):
    dst_ref[...] = src_ref[...].astype(dst_ref.dtype)


def _cast_call(src, dst, chunk0, n_chunks, rows):
    chunk = pl.BlockSpec((rows, src.shape[1]), lambda i: (chunk0 + i, 0))
    return pl.pallas_call(
        _cast_kernel, grid=(n_chunks,), in_specs=[chunk, pl.BlockSpec(memory_space=pl.ANY)], out_specs=chunk,
        out_shape=jax.ShapeDtypeStruct(src.shape, BF16), input_output_aliases={1: 0},
        compiler_params=_params("parallel"), name="cast_rest",
    )(src, dst)


def _mm_heads_kernel(a_ref, w_ref, o_ref):
    res = jnp.dot(a_ref[...], w_ref[...], preferred_element_type=F32)
    for h in range(o_ref.shape[0]):
        o_ref[h] = res[:, h * LANES:(h + 1) * LANES].astype(o_ref.dtype)


def _mm_heads_call(a, w, out_dtype, tm, tn, name):
    m, k = a.shape
    n = w.shape[1]
    return pl.pallas_call(
        _mm_heads_kernel, grid=(n // tn, m // tm),
        in_specs=[pl.BlockSpec((tm, k), lambda j, i: (i, 0)), pl.BlockSpec((k, tn), lambda j, i: (0, j))],
        out_specs=pl.BlockSpec((tn // LANES, tm, LANES), lambda j, i: (j, i, 0)),
        out_shape=jax.ShapeDtypeStruct((n // LANES, m, LANES), out_dtype),
        compiler_params=_params("parallel", "parallel"), name=name,
    )(a, w)


def _mm_res_ln_kernel(a_ref, w_ref, h_ref, g_ref, b_ref, of_ref, ob_ref):
    mix = jnp.dot(a_ref[...], w_ref[...], preferred_element_type=F32)
    y = _layer_norm_rows(DN_ALPHA * h_ref[...] + mix, g_ref[...], b_ref[...])
    of_ref[...] = y
    ob_ref[...] = y.astype(BF16)


def _mm_res_ln_call(a, w, h, g, b, name, tm=256):
    m, k = a.shape
    d = w.shape[1]
    row = pl.BlockSpec((tm, d), lambda i: (i, 0))
    vec = pl.BlockSpec((1, d), lambda i: (0, 0))
    return pl.pallas_call(
        _mm_res_ln_kernel, grid=(m // tm,),
        in_specs=[pl.BlockSpec((tm, k), lambda i: (i, 0)), pl.BlockSpec((k, d), lambda i: (0, 0)), row, vec, vec],
        out_specs=[row, row],
        out_shape=[jax.ShapeDtypeStruct((m, d), F32), jax.ShapeDtypeStruct((m, d), BF16)],
        compiler_params=_params("parallel"), name=name,
    )(a, w, h, g.reshape(1, d), b.reshape(1, d))


def _branch_kernel(y_ref, att_ref, gs_ref, ga_ref, ws_ref, wa_ref, o_ref):
    bs = jnp.dot(y_ref[...], ws_ref[...], preferred_element_type=F32)
    ba = jnp.dot(att_ref[...], wa_ref[...], preferred_element_type=F32)
    merged = _sigmoid(gs_ref[...].astype(F32)) * bs + _sigmoid(ga_ref[...].astype(F32)) * ba
    o_ref[...] = merged.astype(o_ref.dtype)


def _branch_call(y_ssm, att, gates, ws, wa, tm=512, tn=1024):
    m, ks = y_ssm.shape
    ka = att.shape[1]
    d = ws.shape[1]
    nj = d // tn
    return pl.pallas_call(
        _branch_kernel, grid=(nj, m // tm),
        in_specs=[pl.BlockSpec((tm, ks), lambda j, i: (i, 0)),
                  pl.BlockSpec((tm, ka), lambda j, i: (i, 0)),
                  pl.BlockSpec((tm, tn), lambda j, i: (i, j)),
                  pl.BlockSpec((tm, tn), lambda j, i: (i, nj + j)),
                  pl.BlockSpec((ks, tn), lambda j, i: (0, j)),
                  pl.BlockSpec((ka, tn), lambda j, i: (0, j))],
        out_specs=pl.BlockSpec((tm, tn), lambda j, i: (i, j)),
        out_shape=jax.ShapeDtypeStruct((m, d), BF16),
        compiler_params=_params("parallel", "parallel"), name="branch_merge",
    )(y_ssm, att, gates, gates, ws, wa)


def _rotary_kernel(qkv_ref, cos_ref, sin_ref, perm_ref, *out_refs, n_head):
    n_grp = len(DIL_CONFIGS)
    (kh_refs, vh_refs, kt_refs, vt_refs, qt_refs, qv_refs, kv_refs, vv_refs) = [
        out_refs[a * n_grp:(a + 1) * n_grp] for a in range(8)]
    in_tail = pl.program_id(0) >= n_head
    tm = qkv_ref.shape[1]
    half = ROT_DIM // 2
    gw = HEADS_PER_GROUP * ATT_HEAD_DIM
    n_att = n_grp * HEADS_PER_GROUP

    def rot(x, cos, sin):
        lane = lax.broadcasted_iota(jnp.int32, x.shape, 1)
        partner = jnp.where(lane < half, pltpu.roll(x, ATT_HEAD_DIM - half, 1), pltpu.roll(x, half, 1))
        return x * cos + partner * sin

    def rot_view(x, cos, sin):
        partner = jnp.dot(x.astype(BF16), perm_ref[...], preferred_element_type=F32)
        return (x * cos + partner * sin).astype(BF16)

    def emit_token_major(k_refs, v_refs):
        for g in range(n_grp):
            for h in range(HEADS_PER_GROUP):
                hd = g * HEADS_PER_GROUP + h
                k_refs[g][:, h, :] = rot(qkv_ref[n_att + hd], cos_ref[...], sin_ref[...])
                v_refs[g][:, h, :] = qkv_ref[2 * n_att + hd]

    pl.when(jnp.logical_not(in_tail))(lambda: emit_token_major(kh_refs, vh_refs))
    pl.when(in_tail)(lambda: emit_token_major(kt_refs, vt_refs))

    for g, (_, dil) in enumerate(DIL_CONFIGS):
        for r in range(dil):
            rows = pl.ds(r, tm // dil, stride=dil) if dil > 1 else slice(None)
            cos = cos_ref[rows, :]
            sin = sin_ref[rows, :]
            for h in range(HEADS_PER_GROUP):
                hd = g * HEADS_PER_GROUP + h
                view = slice(r * gw + h * ATT_HEAD_DIM, r * gw + (h + 1) * ATT_HEAD_DIM)
                qv_refs[g][:, view] = rot_view(qkv_ref[hd, rows, :], cos, sin)
                kv_refs[g][:, view] = rot_view(qkv_ref[n_att + hd, rows, :], cos, sin)
                vv_refs[g][:, view] = qkv_ref[2 * n_att + hd, rows, :].astype(BF16)

    @pl.when(in_tail)
    def _():
        for g in range(n_grp):
            for h in range(HEADS_PER_GROUP):
                tok = slice(h * ATT_HEAD_DIM, (h + 1) * ATT_HEAD_DIM)
                qt_refs[g][:, tok] = rot(qkv_ref[g * HEADS_PER_GROUP + h], cos_ref[...], sin_ref[...])


def _rotary_call(qkv, cos_t, sin_t, n_head_rows, tm=256):
    n_slab, t, _ = qkv.shape
    gw = HEADS_PER_GROUP * ATT_HEAD_DIM
    n_grp = len(DIL_CONFIGS)
    n_head = n_head_rows // tm
    hshape = (tm, HEADS_PER_GROUP, ATT_HEAD_DIM)
    head_rows = pl.BlockSpec(hshape, lambda i: (jnp.minimum(i, n_head - 1), 0, 0))
    tail_rows = pl.BlockSpec(hshape, lambda i: (jnp.maximum(i - n_head, 0), 0, 0))
    tail = pl.BlockSpec((tm, gw), lambda i: (jnp.maximum(i - n_head, 0), 0))
    tab = pl.BlockSpec((tm, ATT_HEAD_DIM), lambda i: (i, 0))
    views = [pl.BlockSpec((tm // dil, dil * gw), lambda i: (i, 0)) for _, dil in DIL_CONFIGS]
    view_shapes = [jax.ShapeDtypeStruct((t // dil, dil * gw), BF16) for _, dil in DIL_CONFIGS]
    li = jnp.arange(ATT_HEAD_DIM)[:, None]
    lj = jnp.arange(ATT_HEAD_DIM)[None, :]
    half = ROT_DIM // 2
    perm = (((lj < half) & (li == lj + half)) | ((lj >= half) & (lj < ROT_DIM) & (li == lj - half))).astype(BF16)
    return pl.pallas_call(
        functools.partial(_rotary_kernel, n_head=n_head), grid=(t // tm,),
        in_specs=[pl.BlockSpec((n_slab, tm, ATT_HEAD_DIM), lambda i: (0, i, 0)), tab, tab,
                  pl.BlockSpec((ATT_HEAD_DIM, ATT_HEAD_DIM), lambda i: (0, 0))],
        out_specs=[head_rows] * (2 * n_grp) + [tail_rows] * (2 * n_grp) + [tail] * n_grp + views * 3,
        out_shape=([jax.ShapeDtypeStruct((n_head_rows,) + hshape[1:], F32)] * (2 * n_grp)
                   + [jax.ShapeDtypeStruct((t - n_head_rows,) + hshape[1:], F32)] * (2 * n_grp)
                   + [jax.ShapeDtypeStruct((t - n_head_rows, gw), F32)] * n_grp + view_shapes * 3),
        compiler_params=_params("arbitrary"), name="rotary_split",
    )(qkv, cos_t, sin_t, perm)


def _rotary_tables(pos):
    half = ROT_DIM // 2
    inv_freq = jnp.exp(-math.log(ROPE_THETA) * jnp.arange(half, dtype=F32) * (2.0 / ROT_DIM))
    ang = pos.astype(F32)[:, None] * inv_freq[None, :]
    cos, sin = jnp.cos(ang), jnp.sin(ang)
    rest = ATT_HEAD_DIM - ROT_DIM
    ones = jnp.ones((pos.shape[0], rest), F32)
    cos_t = jnp.concatenate([cos, cos, ones], axis=1)
    sin_t = jnp.concatenate([-sin, sin, jnp.zeros_like(ones)], axis=1)
    return cos_t, sin_t


def _band_attn_kernel(q_ref, kc_ref, kp_ref, vc_ref, vp_ref, o_ref, lse_ref, *, dil, span, has_prev):
    n = pl.program_id(1)
    scale = ATT_HEAD_DIM ** -0.5
    qi = lax.broadcasted_iota(jnp.int32, (ATT_BLOCK, ATT_BLOCK), 0)
    ki = lax.broadcasted_iota(jnp.int32, (ATT_BLOCK, ATT_BLOCK), 1)
    rel_c = qi - ki
    valid_c = (rel_c >= 0) & (rel_c <= span)
    rel_p = rel_c + ATT_BLOCK
    valid_p = (rel_p <= span) & (n > 0)
    nt = (((1,), (1,)), ((), ()))
    batch = 2 * HEADS_PER_GROUP
    problems = [(r, h) for r in range(dil) for h in range(HEADS_PER_GROUP)]
    for b0 in range(0, len(problems), batch):
        todo = problems[b0:b0 + batch]
        cols = [slice((r * HEADS_PER_GROUP + h) * ATT_HEAD_DIM, (r * HEADS_PER_GROUP + h + 1) * ATT_HEAD_DIM)
                for r, h in todo]
        qs = [q_ref[:, cs] for cs in cols]
        s_c = [lax.dot_general(q, kc_ref[:, cs], nt, preferred_element_type=F32) for q, cs in zip(qs, cols)]
        s_c = [jnp.where(valid_c, s * scale, -jnp.inf) for s in s_c]
        m = [jnp.max(s, axis=-1, keepdims=True) for s in s_c]
        if has_prev:
            s_p = [lax.dot_general(q, kp_ref[:, cs], nt, preferred_element_type=F32) for q, cs in zip(qs, cols)]
            s_p = [jnp.where(valid_p, s * scale, -jnp.inf) for s in s_p]
            m = [jnp.maximum(mm, jnp.max(s, axis=-1, keepdims=True)) for mm, s in zip(m, s_p)]
        p_c = [jnp.exp(s - mm) for s, mm in zip(s_c, m)]
        den = [jnp.sum(p, axis=-1, keepdims=True) for p in p_c]
        o = [jnp.dot(p.astype(BF16), vc_ref[:, cs], preferred_element_type=F32) for p, cs in zip(p_c, cols)]
        if has_prev:
            p_p = [jnp.exp(s - mm) for s, mm in zip(s_p, m)]
            den = [d + jnp.sum(p, axis=-1, keepdims=True) for d, p in zip(den, p_p)]
            o = [oo + jnp.dot(p.astype(BF16), vp_ref[:, cs], preferred_element_type=F32)
                 for oo, p, cs in zip(o, p_p, cols)]
        for (r, h), oo, d, mm in zip(todo, o, den, m):
            rows = pl.ds(r, ATT_BLOCK, stride=dil) if dil > 1 else slice(None)
            o_ref[h, rows, :] = oo / d
            lse_ref[h, rows, :] = jnp.broadcast_to(mm + jnp.log(d), (ATT_BLOCK, ATT_HEAD_DIM))


def _band_attn_call(q, k, v, bsz, seq, window, dil, g):
    gw = HEADS_PER_GROUP * ATT_HEAD_DIM
    t_pad = q.shape[0] * dil
    span = window // dil
    assert span <= ATT_BLOCK and seq % (dil * ATT_BLOCK) == 0
    nb = seq // (dil * ATT_BLOCK)
    cur = pl.BlockSpec((ATT_BLOCK, dil * gw), lambda b, n: (b * nb + n, 0))
    prev = pl.BlockSpec((ATT_BLOCK, dil * gw), lambda b, n: (b * nb + jnp.maximum(n - 1, 0), 0))
    out = pl.BlockSpec((HEADS_PER_GROUP, ATT_BLOCK * dil, ATT_HEAD_DIM), lambda b, n: (0, b * nb + n, 0))
    kern = functools.partial(_band_attn_kernel, dil=dil, span=span, has_prev=nb > 1)
    return pl.pallas_call(
        kern, grid=(bsz, nb),
        in_specs=[cur, cur, prev, cur, prev], out_specs=[out, out],
        out_shape=[jax.ShapeDtypeStruct((HEADS_PER_GROUP, t_pad, ATT_HEAD_DIM), F32)] * 2,
        compiler_params=_params("parallel", "parallel"), name=f"band_attn_g{g}",
    )(q, k, k, v, v)


def _sample_attn_kernel(q_ref, kc_ref, vc_ref, kn_ref, vn_ref, o_ref, lse_ref, *, dil, lq):
    scale = ATT_HEAD_DIM ** -0.5
    n_cache = kc_ref.shape[1]
    row = lax.broadcasted_iota(jnp.int32, (n_cache, HEADS_PER_GROUP, 1), 0)
    for i in range(lq):
        res = i if dil > 1 else 0
        new_rows = range(i + 1) if dil == 1 else (i,)
        q = q_ref[0, i]
        s_c = jnp.sum(kc_ref[0, :, res] * q[None], axis=-1, keepdims=True) * scale
        if dil == 1:
            s_c = jnp.where(row >= i, s_c, -jnp.inf)
        s_new = [jnp.sum(kn_ref[0, j] * q, axis=-1, keepdims=True) * scale for j in new_rows]
        m = jnp.max(s_c, axis=0)
        for s in s_new:
            m = jnp.maximum(m, s)
        p_c = jnp.exp(s_c - m[None])
        den = jnp.sum(p_c, axis=0)
        o = jnp.sum(p_c * vc_ref[0, :, res], axis=0)
        for j, s in zip(new_rows, s_new):
            p = jnp.exp(s - m)
            den = den + p
            o = o + p * vn_ref[0, j]
        o_ref[0, i] = o / den
        lse_ref[0, i] = jnp.broadcast_to(m + jnp.log(den), (HEADS_PER_GROUP, ATT_HEAD_DIM))


def _decode_mem_attn_kernel(q_ref, k_ref, v_ref, o_ref):
    scale = q_ref.shape[-1] ** -0.5
    for i in range(q_ref.shape[1]):
        q = q_ref[0, i]
        s = jnp.sum(k_ref[0] * q[None], axis=-1, keepdims=True) * scale
        m = jnp.max(s, axis=0)
        p = jnp.exp(s - m[None])
        den = jnp.sum(p, axis=0)
        o_ref[0, i] = (jnp.sum(p * v_ref[0], axis=0) / den).astype(o_ref.dtype)


def _decode_mem_attn_call(q, mem_k, mem_v):
    b, lq, nh, e = q.shape
    qs = pl.BlockSpec((1, lq, nh, e), lambda i: (i, 0, 0, 0))
    ms = pl.BlockSpec((1, mem_k.shape[1], nh, e), lambda i: (i, 0, 0, 0))
    return pl.pallas_call(
        _decode_mem_attn_kernel, grid=(b,), in_specs=[qs, ms, ms], out_specs=qs,
        out_shape=jax.ShapeDtypeStruct((b, lq, nh, e), F32),
        compiler_params=_params("parallel"), name="mem_attn_sample",
    )(q, mem_k, mem_v)


def _sample_attn_call(q, k_new, v_new, k_buf, v_buf, window, dil, g):
    b, lq, nh, e = q.shape
    assert k_buf.shape[1] == window and window % dil == 0 and (dil == 1 or dil >= lq) and lq <= window // dil
    n_res = min(dil, lq)
    n_cache = window // dil
    cache = lambda a: a.reshape(b, n_cache, dil, nh, e)
    new = pl.BlockSpec((1, lq, nh, e), lambda i: (i, 0, 0, 0))
    buf = pl.BlockSpec((1, n_cache, n_res, nh, e), lambda i: (i, 0, 0, 0, 0))
    kern = functools.partial(_sample_attn_kernel, dil=dil, lq=lq)
    return pl.pallas_call(
        kern, grid=(b,), in_specs=[new, buf, buf, new, new], out_specs=[new, new],
        out_shape=[jax.ShapeDtypeStruct((b, lq, nh, e), F32)] * 2,
        compiler_params=_params("parallel"), name=f"sample_attn_g{g}",
    )(q, cache(k_buf), cache(v_buf), k_new, v_new)


def _combine_kernel(*refs):
    n_grp = len(DIL_CONFIGS)
    o_refs, l_refs, out_ref = refs[:n_grp], refs[n_grp:2 * n_grp], refs[2 * n_grp]
    for h in range(HEADS_PER_GROUP):
        ls = [r[h] for r in l_refs]
        m = functools.reduce(jnp.maximum, ls)
        ws = [jnp.exp(l - m) for l in ls]
        tot = functools.reduce(lambda a, b: a + b, ws)
        acc = functools.reduce(lambda a, b: a + b, [w * r[h] for w, r in zip(ws, o_refs)])
        out_ref[:, h * ATT_HEAD_DIM:(h + 1) * ATT_HEAD_DIM] = (acc / tot).astype(out_ref.dtype)


def _combine_call(outs, lses, tm=512):
    nh, t, e = outs[0].shape
    row = pl.BlockSpec((nh, tm, e), lambda i: (0, i, 0))
    return pl.pallas_call(
        _combine_kernel, grid=(t // tm,), in_specs=[row] * (2 * len(outs)),
        out_specs=pl.BlockSpec((tm, nh * e), lambda i: (i, 0)),
        out_shape=jax.ShapeDtypeStruct((t, nh * e), BF16),
        compiler_params=_params("parallel"), name="combine_groups",
    )(*outs, *lses)


def _ssd_kernel(*refs, n_chunks, valid_len, has_state):
    if has_state:
        (z_ref, xbc_ref, dt_ref, h0_ref, cprev_ref, cw_ref, cb_ref, dtb_ref, alog_ref, dskip_ref, nw_ref, expand_ref,
         y_ref, hfin_ref, ctail_ref, ht_ref, xext_ref) = refs
    else:
        (z_ref, xbc_ref, dt_ref, cw_ref, cb_ref, dtb_ref, alog_ref, dskip_ref, nw_ref, expand_ref,
         y_ref, hfin_ref, ctail_ref, ht_ref, xext_ref) = refs
    c = pl.program_id(1)
    lc = SSM_CHUNK
    n_st = SSM_STATE
    gw = ht_ref.shape[1] // SSM_GROUPS
    d_inner = ht_ref.shape[1]
    heads_per_group = gw // SSM_HEAD_DIM
    n_tr = d_inner // LANES

    @pl.when(c == 0)
    def _():
        if has_state:
            xext_ref[0:SUBLANES, :] = cprev_ref[0]
            for k in range(n_tr):
                ht_ref[:, k * LANES:(k + 1) * LANES] = h0_ref[0, k * LANES:(k + 1) * LANES, :].T
        else:
            xext_ref[0:SUBLANES, :] = jnp.zeros((SUBLANES, xext_ref.shape[1]), F32)
            ht_ref[...] = jnp.zeros(ht_ref.shape, F32)

    xext_ref[SUBLANES:SUBLANES + lc, :] = xbc_ref[...]

    def conv_silu(c0, width):
        window = xext_ref[0:SUBLANES + lc, c0:c0 + width]
        acc = cb_ref[:, c0:c0 + width] + cw_ref[CONV_W - 1:CONV_W, c0:c0 + width] * window[SUBLANES:]
        for s in range(1, CONV_W):
            shifted = pltpu.roll(window, s, 0)[SUBLANES:]
            acc = acc + cw_ref[CONV_W - 1 - s:CONV_W - s, c0:c0 + width] * shifted
        return acc * _sigmoid(acc)

    dtr = dt_ref[...] + dtb_ref[...]
    dt = jnp.maximum(dtr, 0.0) + jnp.log(1.0 + jnp.exp(-jnp.abs(dtr)))
    if valid_len < lc:
        trow = lax.broadcasted_iota(jnp.int32, dt.shape, 0)
        dt = jnp.where(trow < valid_len, dt, 0.0)
    a = dt * (-jnp.exp(alog_ref[...]))
    ti = lax.broadcasted_iota(jnp.int32, (lc, lc), 0)
    si = lax.broadcasted_iota(jnp.int32, (lc, lc), 1)
    tri = si <= ti
    f32_dot = lambda lhs, rhs: jnp.dot(lhs, rhs, preferred_element_type=F32)
    tri_b = tri.astype(BF16)
    acum = functools.reduce(lambda u, v: u + v, [f32_dot(tri_b, p) for p in _split3(a)])
    tri_t = (ti <= si).astype(BF16)
    acum_t = functools.reduce(lambda u, v: u + v, [f32_dot(p, tri_t) for p in _split3(a.T)])
    dt3, ac3 = _split3(dt), _split3(acum)
    lane = lax.broadcasted_iota(jnp.int32, (lc, LANES), 1)
    nt = (((1,), (1,)), ((), ()))

    for g in range(SSM_GROUPS):
        c0 = g * gw
        xs = conv_silu(c0, gw)
        bm = conv_silu(d_inner + g * n_st, n_st)
        cm = conv_silu(d_inner + SSM_GROUPS * n_st + g * n_st, n_st)
        expand = expand_ref[:, c0:c0 + gw]
        dt_x = functools.reduce(lambda u, v: u + v, [f32_dot(p, expand) for p in dt3])
        ac_x = functools.reduce(lambda u, v: u + v, [f32_dot(p, expand) for p in ac3])
        xdt = xs * dt_x
        xdt_b = xdt.astype(BF16)
        cm_b = cm.astype(BF16)
        cb = lax.dot_general(cm_b, bm.astype(BF16), nt, preferred_element_type=F32)
        h_old = ht_ref[:, c0:c0 + gw]
        y = jnp.dot(cm_b, h_old.astype(BF16), preferred_element_type=F32) * jnp.exp(ac_x)
        diag = []
        for k in range(gw // LANES):
            x_pair = xdt_b[:, k * LANES:(k + 1) * LANES]
            y_pair = None
            for hh in range(LANES // SSM_HEAD_DIM):
                h = g * heads_per_group + k * (LANES // SSM_HEAD_DIM) + hh
                seg = acum[:, h:h + 1] - acum_t[h:h + 1, :]
                lmat = jnp.exp(jnp.where(tri, seg, -jnp.inf))
                m_b = (cb * lmat).astype(BF16)
                in_head = (lane >= hh * SSM_HEAD_DIM) & (lane < (hh + 1) * SSM_HEAD_DIM)
                part = jnp.dot(m_b, jnp.where(in_head, x_pair, jnp.zeros_like(x_pair)), preferred_element_type=F32)
                y_pair = part if y_pair is None else y_pair + part
            diag.append(y_pair)
        y = y + jnp.concatenate(diag, axis=1) + dskip_ref[:, c0:c0 + gw] * xs
        zg = z_ref[:, c0:c0 + gw].astype(F32)
        y = y * (zg * _sigmoid(zg))
        y = y * lax.rsqrt(jnp.mean(y * y, axis=-1, keepdims=True) + RMS_EPS)
        y_ref[:, c0:c0 + gw] = (y * nw_ref[:, c0:c0 + gw]).astype(y_ref.dtype)
        a_last = ac_x[lc - 1:lc, :]
        xw = (xdt * jnp.exp(a_last - ac_x)).astype(BF16)
        ht_ref[:, c0:c0 + gw] = h_old * jnp.exp(a_last) + jnp.dot(bm.T.astype(BF16), xw, preferred_element_type=F32)

    xext_ref[0:SUBLANES, :] = xext_ref[lc:lc + SUBLANES, :]

    @pl.when(c == n_chunks - 1)
    def _():
        ctail_ref[0] = xext_ref[0:SUBLANES, :]
        for k in range(n_tr):
            hfin_ref[0, k * LANES:(k + 1) * LANES, :] = ht_ref[:, k * LANES:(k + 1) * LANES].T


def _ssd_call(z, xbc, dt_raw, n_seq, n_chunks, valid_len, state, conv_w, conv_b, dt_bias, a_log, d_skip, norm_w,
              out_rows, name):
    d_inner = z.shape[1]
    conv_dim = xbc.shape[1]
    n_heads = d_inner // SSM_HEAD_DIM
    lc = SSM_CHUNK
    pad_h = lambda v: jnp.pad(v.astype(F32), (0, LANES - n_heads)).reshape(1, LANES)
    rows = lambda w: pl.BlockSpec((lc, w), lambda b, c: (b * n_chunks + c, 0))
    const = lambda r, w: pl.BlockSpec((r, w), lambda b, c: (0, 0))
    per_seq = lambda r, w: pl.BlockSpec((1, r, w), lambda b, c: (b, 0, 0))
    args = [z, xbc, dt_raw]
    specs = [rows(d_inner), rows(conv_dim), rows(LANES)]
    if state is not None:
        args += list(state)
        specs += [per_seq(d_inner, SSM_STATE), per_seq(SUBLANES, conv_dim)]
    args += [conv_w, conv_b.reshape(1, conv_dim), pad_h(dt_bias), pad_h(a_log),
             jnp.repeat(d_skip.astype(F32), SSM_HEAD_DIM).reshape(1, d_inner), norm_w.reshape(1, d_inner),
             (jnp.arange(LANES)[:, None] == jnp.arange(d_inner)[None, :] // SSM_HEAD_DIM).astype(BF16)]
    specs += [const(CONV_W, conv_dim), const(1, conv_dim), const(1, LANES), const(1, LANES),
              const(1, d_inner), const(1, d_inner), const(LANES, d_inner)]
    kern = functools.partial(_ssd_kernel, n_chunks=n_chunks, valid_len=valid_len, has_state=state is not None)
    return pl.pallas_call(
        kern, grid=(n_seq, n_chunks), in_specs=specs,
        out_specs=[rows(d_inner), per_seq(d_inner, SSM_STATE), per_seq(SUBLANES, conv_dim)],
        out_shape=[jax.ShapeDtypeStruct((out_rows, d_inner), BF16),
                   jax.ShapeDtypeStruct((n_seq, d_inner, SSM_STATE), F32),
                   jax.ShapeDtypeStruct((n_seq, SUBLANES, conv_dim), F32)],
        scratch_shapes=[pltpu.VMEM((SSM_STATE, d_inner), F32), pltpu.VMEM((lc + 2 * SUBLANES, conv_dim), F32)],
        compiler_params=_params("parallel", "arbitrary"), name=name,
    )(*args)


def _mem_attn_kernel(q_ref, k_ref, v_ref, o_ref):
    hd = q_ref.shape[1] // MEM_HEADS
    scale = hd ** -0.5
    cols = [slice(h * hd, (h + 1) * hd) for h in range(MEM_HEADS)]
    nt = (((1,), (1,)), ((), ()))
    s = [lax.dot_general(q_ref[:, c], k_ref[:, c].astype(BF16), nt, preferred_element_type=F32) * scale for c in cols]
    m = [jnp.max(x, axis=-1, keepdims=True) for x in s]
    p = [jnp.exp(x - mm) for x, mm in zip(s, m)]
    den = [jnp.sum(x, axis=-1, keepdims=True) for x in p]
    o = [jnp.dot(x.astype(BF16), v_ref[:, c].astype(BF16), preferred_element_type=F32) for x, c in zip(p, cols)]
    for c, oo, dd in zip(cols, o, den):
        o_ref[:, c] = (oo / dd).astype(o_ref.dtype)


def _mem_attn_call(q, mem_k, mem_v, n_seq, lq, tq, out_rows, name):
    d = q.shape[1]
    n_mem = mem_k.shape[0] // n_seq
    nq = lq // tq
    qs = pl.BlockSpec((tq, d), lambda b, i: (b * nq + i, 0))
    ks = pl.BlockSpec((n_mem, d), lambda b, i: (b, 0))
    return pl.pallas_call(
        _mem_attn_kernel, grid=(n_seq, nq), in_specs=[qs, ks, ks], out_specs=qs,
        out_shape=jax.ShapeDtypeStruct((out_rows, d), BF16),
        compiler_params=_params("parallel", "parallel"), name=name,
    )(q, mem_k, mem_v)


def _router_kernel(x_ref, wh_ref, wm_ref, b_ref, idx_ref, gate_ref):
    x = x_ref[...]
    xh = x.astype(BF16)
    xm = (x - xh.astype(F32)).astype(BF16)
    f32_dot = lambda a, b: jnp.dot(a, b, preferred_element_type=F32)
    logits = f32_dot(xh, wh_ref[...]) + (f32_dot(xh, wm_ref[...]) + f32_dot(xm, wh_ref[...])) + b_ref[...]
    lane = lax.broadcasted_iota(jnp.int32, logits.shape, 1)
    idx_out = jnp.zeros(logits.shape, jnp.int32)
    vals = []
    for k in range(TOP_K):
        m = jnp.max(logits, axis=-1, keepdims=True)
        pick = jnp.min(jnp.where(logits == m, lane, LANES), axis=-1, keepdims=True)
        idx_out = jnp.where(lane == k, pick, idx_out)
        logits = jnp.where(lane == pick, -jnp.inf, logits)
        vals.append(m)
    exps = [jnp.exp(v - vals[0]) for v in vals]
    tot = functools.reduce(lambda a, b: a + b, exps)
    gate_out = jnp.zeros(logits.shape, F32)
    for k in range(TOP_K):
        gate_out = jnp.where(lane == k, exps[k] / tot, gate_out)
    idx_ref[...] = idx_out
    gate_ref[...] = gate_out


def _router_call(x, w_router, b_router, tm=256):
    t, d = x.shape
    n_exp = w_router.shape[1]
    w = jnp.pad(w_router.astype(F32), ((0, 0), (0, LANES - n_exp)))
    w_hi = w.astype(BF16)
    w_mid = (w - w_hi.astype(F32)).astype(BF16)
    b = jnp.pad(b_router.astype(F32), (0, LANES - n_exp), constant_values=-jnp.inf).reshape(1, LANES)
    out = pl.BlockSpec((tm, LANES), lambda i: (i, 0))
    wspec = pl.BlockSpec((d, LANES), lambda i: (0, 0))
    return pl.pallas_call(
        _router_kernel, grid=(t // tm,),
        in_specs=[pl.BlockSpec((tm, d), lambda i: (i, 0)), wspec, wspec, pl.BlockSpec((1, LANES), lambda i: (0, 0))],
        out_specs=[out, out],
        out_shape=[jax.ShapeDtypeStruct((t, LANES), jnp.int32), jax.ShapeDtypeStruct((t, LANES), F32)],
        compiler_params=_params("parallel"), name="router",
    )(x, w_hi, w_mid, b)


def _dispatch_kernel(pend_ref, slot_ref, x_ref, xg_hbm, sbuf, zbuf, sem, zsem, *, n_exp, block_rows, n_tiles):
    i = pl.program_id(0)
    s = i % 2
    tm = x_ref.shape[0]

    def zero_copy(e):
        start = pl.multiple_of(pend_ref[e] - block_rows, block_rows)
        return pltpu.make_async_copy(zbuf, xg_hbm.at[pl.ds(start, block_rows)], zsem)

    def has_rows(e):
        return pend_ref[e] > (pend_ref[e - 1] if e else 0)

    def wait_rows(s_):
        for _ in range(TOP_K):
            pltpu.make_async_copy(sbuf.at[s_], xg_hbm.at[pl.ds(0, tm)], sem.at[s_]).wait()

    @pl.when(i == 0)
    def _():
        zbuf[...] = jnp.zeros(zbuf.shape, F32)
        for e in range(n_exp):
            pl.when(has_rows(e))(lambda e=e: zero_copy(e).start())
        for e in range(n_exp):
            pl.when(has_rows(e))(lambda e=e: zero_copy(e).wait())

    @pl.when(i >= 2)
    def _():
        wait_rows(s)

    sbuf[s] = x_ref[...]
    for r in range(tm):
        for k in range(TOP_K):
            pltpu.make_async_copy(sbuf.at[s, pl.ds(r, 1)], xg_hbm.at[pl.ds(slot_ref[0, 0, r * TOP_K + k], 1)],
                                  sem.at[s]).start()

    @pl.when(i == n_tiles - 1)
    def _():
        wait_rows(s)
        if n_tiles > 1:
            wait_rows(1 - s)


def _dispatch_call(x, slot, pad_end, n_slots, tm=256):
    t, d = x.shape
    n_tiles = t // tm
    grid_spec = pltpu.PrefetchScalarGridSpec(
        num_scalar_prefetch=1, grid=(n_tiles,),
        in_specs=[pl.BlockSpec((1, 1, tm * TOP_K), lambda i, pe: (i, 0, 0), memory_space=pltpu.SMEM),
                  pl.BlockSpec((tm, d), lambda i, pe: (i, 0))],
        out_specs=pl.BlockSpec(memory_space=pl.ANY),
        scratch_shapes=[pltpu.VMEM((2, tm, d), F32), pltpu.VMEM((MOE_TM, d), F32),
                        pltpu.SemaphoreType.DMA((2,)), pltpu.SemaphoreType.DMA(())],
    )
    kern = functools.partial(_dispatch_kernel, n_exp=pad_end.shape[0], block_rows=MOE_TM, n_tiles=n_tiles)
    return pl.pallas_call(
        kern, grid_spec=grid_spec, out_shape=jax.ShapeDtypeStruct((n_slots, d), F32),
        compiler_params=_params("arbitrary"), name="moe_dispatch",
    )(pad_end, slot.reshape(n_tiles, 1, tm * TOP_K), x)


def _moe_kernel(nused_ref, bexp_ref, x_ref, wg_ref, wu_ref, wd_ref, bg_ref, bu_ref, bd_ref, y_ref, *, nj):
    i = pl.program_id(0)
    j = pl.program_id(1)

    @pl.when(i < nused_ref[0])
    def _():
        @pl.when(j == 0)
        def _():
            y_ref[...] = jnp.broadcast_to(bd_ref[0], y_ref.shape)

        x = x_ref[...].astype(BF16)
        gate = jnp.minimum(jnp.dot(x, wg_ref[0], preferred_element_type=F32) + bg_ref[0], SWIGLU_LIMIT)
        up = jnp.clip(jnp.dot(x, wu_ref[0], preferred_element_type=F32) + bu_ref[0], -SWIGLU_LIMIT, SWIGLU_LIMIT)
        act = (up + 1.0) * gate * _sigmoid(gate * SWIGLU_ALPHA)
        y_ref[...] += jnp.dot(act.astype(BF16), wd_ref[0], preferred_element_type=F32)


def _moe_call(x, top_idx, w_gate_up, b_gate_up, w_down, b_down):
    t, d = x.shape
    n_exp, _, two_h = w_gate_up.shape
    d_exp = two_h // 2
    tm, tn = MOE_TM, MOE_TN
    nj = d_exp // tn
    n_assign = t * TOP_K
    n_blocks = n_assign // tm + n_exp
    n_slots = n_blocks * tm

    flat_e = top_idx.reshape(n_assign)
    onehot = (flat_e[:, None] == jnp.arange(n_exp, dtype=jnp.int32)[None, :]).astype(jnp.int32)
    counts = jnp.sum(onehot, axis=0)
    rank = jnp.take_along_axis(jnp.cumsum(onehot, axis=0), flat_e[:, None], axis=1)[:, 0] - 1
    padded = (counts + tm - 1) // tm * tm
    pad_end = jnp.cumsum(padded)
    slot = (pad_end - padded)[flat_e] + rank
    n_used = (pad_end[-1] // tm).astype(jnp.int32)
    blk = jnp.minimum(jnp.arange(n_blocks, dtype=jnp.int32), n_used - 1) * tm
    block_expert = jnp.minimum(jnp.sum((pad_end[None, :] <= blk[:, None]).astype(jnp.int32), axis=1), n_exp - 1)
    xg = _dispatch_call(x, slot, pad_end.astype(jnp.int32), n_slots)

    live = lambda i, nu: i < nu[0]
    rows = pl.BlockSpec((tm, d), lambda i, j, nu, be: (jnp.minimum(i, nu[0] - 1), 0))
    grid_spec = pltpu.PrefetchScalarGridSpec(
        num_scalar_prefetch=2, grid=(n_blocks, nj),
        in_specs=[
            rows,
            pl.BlockSpec((1, d, tn), lambda i, j, nu, be: (be[i], 0, jnp.where(live(i, nu), j, nj - 1))),
            pl.BlockSpec((1, d, tn), lambda i, j, nu, be: (be[i], 0, nj + jnp.where(live(i, nu), j, nj - 1))),
            pl.BlockSpec((1, tn, d), lambda i, j, nu, be: (be[i], jnp.where(live(i, nu), j, nj - 1), 0)),
            pl.BlockSpec((1, 1, tn), lambda i, j, nu, be: (be[i], 0, jnp.where(live(i, nu), j, nj - 1))),
            pl.BlockSpec((1, 1, tn), lambda i, j, nu, be: (be[i], 0, nj + jnp.where(live(i, nu), j, nj - 1))),
            pl.BlockSpec((1, 1, d), lambda i, j, nu, be: (be[i], 0, 0)),
        ],
        out_specs=rows,
    )
    bgu = b_gate_up.astype(F32).reshape(n_exp, 1, two_h)
    y_slots = pl.pallas_call(
        functools.partial(_moe_kernel, nj=nj), grid_spec=grid_spec,
        out_shape=jax.ShapeDtypeStruct((n_slots, d), F32),
        compiler_params=_params("arbitrary", "arbitrary"), name="moe_experts",
    )(n_used.reshape(1), block_expert, xg, w_gate_up, w_gate_up, w_down, bgu, bgu,
      b_down.astype(F32).reshape(n_exp, 1, d))
    return y_slots, slot


def _moe_out_kernel(slotc_ref, slotn_ref, y_hbm, gate_ref, h_ref, g_ref, b_ref, head_ref, tail_ref, ybuf, sem,
                    *, n_head, n_tiles):
    i = pl.program_id(0)
    s = i % 2
    tm = h_ref.shape[0]

    def fetch(tab_ref, s_):
        for r in range(TOP_K * tm):
            pltpu.make_async_copy(y_hbm.at[pl.ds(tab_ref[0, 0, r], 1)], ybuf.at[s_, pl.ds(r, 1)], sem.at[s_]).start()

    @pl.when(i == 0)
    def _():
        fetch(slotc_ref, 0)

    @pl.when(i + 1 < n_tiles)
    def _():
        fetch(slotn_ref, 1 - s)

    pltpu.make_async_copy(y_hbm.at[pl.ds(0, TOP_K * tm)], ybuf.at[s], sem.at[s]).wait()
    gates = gate_ref[...]
    y = gates[:, 0:1] * ybuf[s, 0:tm]
    for k in range(1, TOP_K):
        y = y + gates[:, k:k + 1] * ybuf[s, k * tm:(k + 1) * tm]
    out = _layer_norm_rows(DN_ALPHA * h_ref[...] + y, g_ref[...], b_ref[...])

    @pl.when(pl.program_id(0) < n_head)
    def _():
        head_ref[...] = out

    @pl.when(pl.program_id(0) >= n_head)
    def _():
        tail_ref[...] = out


def _moe_out_call(y_slots, slot, gates, h, g, b, n_head_rows, tm=256):
    t, d = h.shape
    nb = t // tm
    n_head = n_head_rows // tm
    row = pl.BlockSpec((tm, d), lambda i: (i, 0))
    vec = pl.BlockSpec((1, d), lambda i: (0, 0))
    table = jnp.transpose(slot.reshape(nb, tm, TOP_K), (0, 2, 1)).reshape(nb, 1, TOP_K * tm)
    smem = lambda imap: pl.BlockSpec((1, 1, TOP_K * tm), imap, memory_space=pltpu.SMEM)
    return pl.pallas_call(
        functools.partial(_moe_out_kernel, n_head=n_head, n_tiles=nb), grid=(nb,),
        in_specs=[smem(lambda i: (i, 0, 0)), smem(lambda i: (jnp.minimum(i + 1, nb - 1), 0, 0)),
                  pl.BlockSpec(memory_space=pl.ANY), pl.BlockSpec((tm, LANES), lambda i: (i, 0)), row, vec, vec],
        out_specs=[pl.BlockSpec((tm, d), lambda i: (jnp.minimum(i, n_head - 1), 0)),
                   pl.BlockSpec((tm, d), lambda i: (jnp.maximum(i - n_head, 0), 0))],
        out_shape=[jax.ShapeDtypeStruct((n_head_rows, d), F32), jax.ShapeDtypeStruct((t - n_head_rows, d), F32)],
        scratch_shapes=[pltpu.VMEM((2, TOP_K * tm, d), F32), pltpu.SemaphoreType.DMA((2,))],
        compiler_params=_params("arbitrary"), name="moe_combine_ln",
    )(table, table, y_slots, gates, h, g.reshape(1, d), b.reshape(1, d))


def kernel(x_prompt, x_sample, state_conv, state_ssm, cache_k_w128, cache_v_w128, cache_k_w512, cache_v_w512, cache_k_w2048, cache_v_w2048, cache_mem_k, cache_mem_v, mem_prompt, ln_in_g, ln_in_b, w_in, conv_w, conv_b, dt_bias, a_log, d_skip, ssm_norm_w, w_branch_ssm, w_branch_att, w_mix_out, ln1_g, ln1_b, w_mem_q, w_mem_k, w_mem_v, w_mem_o, ln2_g, ln2_b, w_router, b_router, w_gate_up, b_gate_up, w_down, b_down, ln3_g, ln3_b):
    assert w_in.shape[0] == DEPTH
    bp, lp, d = x_prompt.shape
    bs, ls, _ = x_sample.shape
    n_p, n_s = bp * lp, bs * ls
    t_real = n_p + n_s
    t_pad = -(-t_real // ROW_TILE) * ROW_TILE
    d_inner = ssm_norm_w.shape[1]
    conv_dim = conv_w.shape[2]
    n_heads = d_inner // SSM_HEAD_DIM
    gw = HEADS_PER_GROUP * ATT_HEAD_DIM
    att_w = len(DIL_CONFIGS) * gw
    n_mem = mem_prompt.shape[1]
    lc = SSM_CHUNK
    assert lp % lc == 0 and ls <= lc and n_s % SUBLANES == 0

    def sample_rows(a):
        return a[n_p:n_p + n_s]

    def with_sample_rows(a, rows):
        tail = jnp.concatenate([rows.astype(a.dtype), jnp.zeros((t_pad - t_real, a.shape[1]), a.dtype)], axis=0)
        return lax.dynamic_update_slice(a, tail, (n_p, 0))

    def with_sample_heads(a, rows):
        nh, _, e = a.shape
        rows = jnp.transpose(rows.reshape(n_s, nh, e), (1, 0, 2)).astype(a.dtype)
        tail = jnp.concatenate([rows, jnp.zeros((nh, t_pad - t_real, e), a.dtype)], axis=1)
        return lax.dynamic_update_slice(a, tail, (0, n_p, 0))

    x_tail = jnp.concatenate([x_sample.reshape(n_s, d), jnp.zeros((t_pad - t_real, d), x_sample.dtype)], axis=0)
    h0, h0_b = _ln_call(x_prompt.reshape(n_p, d), x_tail, ln_in_g, ln_in_b)

    o_z, o_xbc, o_dt, o_qkv, o_g = 0, d_inner, d_inner + conv_dim, d_inner + conv_dim + n_heads, \
        d_inner + conv_dim + n_heads + 3 * att_w
    w_cols = lambda a, b: w_in[0, :, a:b].astype(BF16)
    n_exp, _, two_h = w_gate_up.shape[1:]
    wgu_src = w_gate_up[0].reshape(n_exp * d, two_h)
    wd_src = w_down[0].reshape(n_exp * (two_h // 2), d)
    steps = lambda n_cols, tn: (n_cols // tn) * (t_pad // 512)
    gu_rows, wd_rows = 128, 256
    gu_chunks, wd_chunks = wgu_src.shape[0] // gu_rows, wd_src.shape[0] // wd_rows
    n_a = min(steps(conv_dim, 1024), gu_chunks)
    n_b = min(steps(d_inner, 1024), gu_chunks - n_a)
    n_c = min(steps(2 * d, 1024), wd_chunks)
    xbc, wgu_b = _mm_call(h0_b, w_cols(o_xbc, o_dt), F32, 512, 1024, "in_xbc", (wgu_src, None, 0, n_a, gu_rows))
    if n_b:
        z, wgu_b = _mm_call(h0_b, w_cols(o_z, o_xbc), BF16, 512, 1024, "in_z", (wgu_src, wgu_b, n_a, n_b, gu_rows))
    else:
        z = _mm_call(h0_b, w_cols(o_z, o_xbc), BF16, 512, 1024, "in_z")
    if gu_chunks - n_a - n_b:
        wgu_b = _cast_call(wgu_src, wgu_b, n_a + n_b, gu_chunks - n_a - n_b, gu_rows)
    w_dt = jnp.pad(w_cols(o_dt, o_qkv), ((0, 0), (0, LANES - n_heads)))
    dt_raw = _mm_call(h0_b, w_dt, F32, 512, LANES, "in_dt")
    qkv = _mm_heads_call(h0_b, w_cols(o_qkv, o_g), F32, 512, att_w, "in_qkv")
    gates, wd_b = _mm_call(h0_b, w_cols(o_g, w_in.shape[2]), BF16, 512, 1024, "in_gates",
                           (wd_src, None, 0, n_c, wd_rows))
    if wd_chunks - n_c:
        wd_b = _cast_call(wd_src, wd_b, n_c, wd_chunks - n_c, wd_rows)
    wgu_b = wgu_b.reshape(n_exp, d, two_h)
    wd_b = wd_b.reshape(n_exp, two_h // 2, d)

    ssd_w = (conv_w[0], conv_b[0], dt_bias[0], a_log[0], d_skip[0], ssm_norm_w[0])
    y_ssm, ssm_p, conv_tail = _ssd_call(z, xbc, dt_raw, bp, lp // lc, lc, None, *ssd_w, out_rows=t_pad,
                                        name="ssd_prompt")

    def pad_seq(a):
        return jnp.pad(sample_rows(a).reshape(bs, ls, a.shape[1]), ((0, 0), (0, lc - ls), (0, 0))).reshape(bs * lc, a.shape[1])

    conv_prev = jnp.pad(state_conv[0], ((0, 0), (SUBLANES - (CONV_W - 1), 0), (0, 0)))
    y_s, ssm_s, _ = _ssd_call(pad_seq(z), pad_seq(xbc), pad_seq(dt_raw), bs, 1, ls,
                              (state_ssm[0].reshape(bs, d_inner, SSM_STATE), conv_prev), *ssd_w,
                              out_rows=bs * lc, name="ssd_sample")
    y_ssm = with_sample_rows(y_ssm, y_s.reshape(bs, lc, d_inner)[:, :ls].reshape(n_s, d_inner))

    pos = jnp.concatenate([jnp.tile(jnp.arange(lp, dtype=jnp.int32), bp),
                           jnp.tile(PAST_LEN + jnp.arange(ls, dtype=jnp.int32), bs),
                           jnp.zeros((t_pad - t_real,), jnp.int32)])
    cos_t, sin_t = _rotary_tables(pos)
    split = _rotary_call(qkv, cos_t, sin_t, n_p)
    n_grp = len(DIL_CONFIGS)
    k_head, v_head, k_tail, v_tail, q_tail, q_view, k_view, v_view = [split[a * n_grp:(a + 1) * n_grp]
                                                                      for a in range(8)]
    caches = ((cache_k_w128, cache_v_w128), (cache_k_w512, cache_v_w512), (cache_k_w2048, cache_v_w2048))
    outs, lses, kv_p, kv_s = [], [], [], []
    heads = lambda a: a.reshape(bs, ls, HEADS_PER_GROUP, ATT_HEAD_DIM)
    for g, (window, dil) in enumerate(DIL_CONFIGS):
        o_p, l_p = _band_attn_call(q_view[g], k_view[g], v_view[g], bp, lp, window, dil, g)
        k_new, v_new = heads(k_tail[g][:n_s]), heads(v_tail[g][:n_s])
        o_s, l_s = _sample_attn_call(heads(q_tail[g][:n_s]), k_new, v_new, caches[g][0][0], caches[g][1][0],
                                     window, dil, g)
        outs.append(with_sample_heads(o_p, o_s))
        lses.append(with_sample_heads(l_p, l_s))
        keep = min(window, lp)
        for a in (k_head[g], v_head[g]):
            kv_p.append(a.reshape(bp, lp, HEADS_PER_GROUP, ATT_HEAD_DIM)[:, lp - keep:][None])
        kv_s += [k_new[None], v_new[None]]
    att = _combine_call(outs, lses)

    merged = _branch_call(y_ssm, att, gates, w_branch_ssm[0].astype(BF16), w_branch_att[0].astype(BF16))
    h1, h1_b = _mm_res_ln_call(merged, w_mix_out[0].astype(BF16), h0, ln1_g[0], ln1_b[0], "mix_out_ln1")

    mem_b = mem_prompt.reshape(bp * n_mem, d).astype(BF16)
    mem_k_p = _mm_call(mem_b, w_mem_k[0].astype(BF16), F32, 512, 1024, "mem_k")
    mem_v_p = _mm_call(mem_b, w_mem_v[0].astype(BF16), F32, 512, 1024, "mem_v")
    q_mem = _mm_call(h1_b, w_mem_q[0].astype(BF16), BF16, 512, 1024, "mem_q")
    o_mem = _mem_attn_call(q_mem, mem_k_p, mem_v_p, bp, lp, 512, t_pad, "mem_attn_prompt")
    q_s = sample_rows(q_mem).reshape(bs, ls, MEM_HEADS, d // MEM_HEADS).astype(F32)
    o_mem_s = _decode_mem_attn_call(q_s, cache_mem_k[0], cache_mem_v[0])
    o_mem = with_sample_rows(o_mem, o_mem_s.reshape(n_s, d))
    h2, _ = _mm_res_ln_call(o_mem, w_mem_o[0].astype(BF16), h1, ln2_g[0], ln2_b[0], "mem_o_ln2")

    idx_t, gate_t = _router_call(h2, w_router[0], b_router[0])
    y_slots, slot = _moe_call(h2, idx_t[:, :TOP_K], wgu_b, b_gate_up[0], wd_b, b_down[0])
    h3_head, h3_tail = _moe_out_call(y_slots, slot, gate_t, h2, ln3_g[0], ln3_b[0], n_p)

    y_prompt = h3_head.reshape(bp, lp, d)
    y_sample = h3_tail[:n_s].reshape(bs, ls, d)
    conv_p = conv_tail[:, SUBLANES - (CONV_W - 1):][None]
    xp_s = jnp.concatenate([state_conv[0].astype(xbc.dtype), sample_rows(xbc).reshape(bs, ls, conv_dim)], axis=1)
    conv_s = xp_s[:, -(CONV_W - 1):][None]
    state_shape = (n_heads, SSM_HEAD_DIM, SSM_STATE)
    ssm_p = ssm_p.reshape(1, bp, *state_shape)
    ssm_s = ssm_s.reshape(1, bs, *state_shape)
    mem_shape = (1, bp, n_mem, MEM_HEADS, d // MEM_HEADS)
    return (y_prompt, y_sample, conv_p, ssm_p, *kv_p, mem_k_p.reshape(mem_shape), mem_v_p.reshape(mem_shape),
            conv_s, ssm_s, *kv_s)
```

```python
import functools
import math

import jax
import jax.numpy as jnp
from jax import lax
from jax.experimental import pallas as pl
from jax.experimental.pallas import tpu as pltpu

F32 = jnp.float32
BF16 = jnp.bfloat16

PAST_LEN = 16384
SSM_HEAD_DIM = 64
SSM_GROUPS = 8
SSM_STATE = 128
CONV_W = 4
SSM_CHUNK = 128
RMS_EPS = 1e-5
DIL_CONFIGS = ((128, 1), (512, 4), (2048, 16))
HEADS_PER_GROUP = 4
ATT_HEAD_DIM = 128
ATT_BLOCK = 128
ROT_DIM = ATT_HEAD_DIM // 4
ROPE_THETA = 500000.0
MEM_HEADS = 4
TOP_K = 4
SWIGLU_LIMIT = 7.0
SWIGLU_ALPHA = 1.702
LN_EPS = 1e-5
DEPTH = 1
DN_ALPHA = (2.0 * DEPTH) ** 0.25

LANES = 128
SUBLANES = 8
ROW_TILE = 512
MOE_TM = 512
MOE_TN = 1024
VMEM_LIMIT = 56 * 1024 * 1024


def _params(*sem):
    return pltpu.CompilerParams(dimension_semantics=sem, vmem_limit_bytes=VMEM_LIMIT)


def _sigmoid(x):
    return 0.5 * jnp.tanh(0.5 * x) + 0.5


def _split3(x):
    hi = x.astype(BF16)
    r1 = x - hi.astype(F32)
    mid = r1.astype(BF16)
    lo = (r1 - mid.astype(F32)).astype(BF16)
    return hi, mid, lo


def _layer_norm_rows(x, g, b):
    mu = jnp.mean(x, axis=-1, keepdims=True)
    xc = x - mu
    var = jnp.mean(xc * xc, axis=-1, keepdims=True)
    return xc * lax.rsqrt(var + LN_EPS) * g + b


def _ln_kernel(xa_ref, xb_ref, g_ref, b_ref, of_ref, ob_ref, *, n_head):
    def emit(x_ref):
        y = _layer_norm_rows(x_ref[...], g_ref[...], b_ref[...])
        of_ref[...] = y
        ob_ref[...] = y.astype(BF16)

    pl.when(pl.program_id(0) < n_head)(lambda: emit(xa_ref))
    pl.when(pl.program_id(0) >= n_head)(lambda: emit(xb_ref))


def _ln_call(x_head, x_tail, g, b, tm=256):
    d = x_head.shape[1]
    n_head, n_tail = x_head.shape[0] // tm, x_tail.shape[0] // tm
    t = (n_head + n_tail) * tm
    row = pl.BlockSpec((tm, d), lambda i: (i, 0))
    vec = pl.BlockSpec((1, d), lambda i: (0, 0))
    return pl.pallas_call(
        functools.partial(_ln_kernel, n_head=n_head), grid=(n_head + n_tail,),
        in_specs=[pl.BlockSpec((tm, d), lambda i: (jnp.minimum(i, n_head - 1), 0)),
                  pl.BlockSpec((tm, d), lambda i: (jnp.maximum(i - n_head, 0), 0)), vec, vec],
        out_specs=[row, row],
        out_shape=[jax.ShapeDtypeStruct((t, d), F32), jax.ShapeDtypeStruct((t, d), BF16)],
        compiler_params=_params("arbitrary"), name="ln_in",
    )(x_head, x_tail, g.reshape(1, d), b.reshape(1, d))


def _mm_kernel(a_ref, w_ref, *rest, n_cast):
    o_ref = rest[-2] if n_cast else rest[-1]
    if n_cast:
        src_ref, dst_ref = rest[0], rest[-1]
        step = pl.program_id(0) * pl.num_programs(1) + pl.program_id(1)

        @pl.when(step < n_cast)
        def _():
            dst_ref[...] = src_ref[...].astype(dst_ref.dtype)

    o_ref[...] = jnp.dot(a_ref[...], w_ref[...], preferred_element_type=F32).astype(o_ref.dtype)


def _mm_call(a, w, out_dtype, tm, tn, name, cast=None):
    m, k = a.shape
    n = w.shape[1]
    tm = min(tm, m)
    ni = m // tm
    args = [a, w]
    in_specs = [pl.BlockSpec((tm, k), lambda j, i: (i, 0)), pl.BlockSpec((k, tn), lambda j, i: (0, j))]
    out_specs = pl.BlockSpec((tm, tn), lambda j, i: (i, j))
    out_shape = jax.ShapeDtypeStruct((m, n), out_dtype)
    aliases = {}
    n_cast = 0
    if cast is not None:
        src, dst, chunk0, n_cast, rows = cast
        assert 0 < n_cast <= (n // tn) * ni
        chunk = pl.BlockSpec((rows, src.shape[1]), lambda j, i: (chunk0 + jnp.minimum(j * ni + i, n_cast - 1), 0))
        args.append(src)
        in_specs.append(chunk)
        if dst is not None:
            aliases = {len(args): 1}
            args.append(dst)
            in_specs.append(pl.BlockSpec(memory_space=pl.ANY))
        out_specs = [out_specs, chunk]
        out_shape = [out_shape, jax.ShapeDtypeStruct(src.shape, BF16)]
    return pl.pallas_call(
        functools.partial(_mm_kernel, n_cast=n_cast), grid=(n // tn, ni),
        in_specs=in_specs, out_specs=out_specs, out_shape=out_shape, input_output_aliases=aliases,
        compiler_params=_params("arbitrary", "arbitrary") if n_cast else _params("parallel", "parallel"), name=name,
    )(*args)


def _cast_kernel(src_ref, _, dst_ref):
    dst_ref[...] = src_ref[...].astype(dst_ref.dtype)


def _cast_call(src, dst, chunk0, n_chunks, rows):
    chunk = pl.BlockSpec((rows, src.shape[1]), lambda i: (chunk0 + i, 0))
    return pl.pallas_call(
        _cast_kernel, grid=(n_chunks,), in_specs=[chunk, pl.BlockSpec(memory_space=pl.ANY)], out_specs=chunk,
        out_shape=jax.ShapeDtypeStruct(src.shape, BF16), input_output_aliases={1: 0},
        compiler_params=_params("parallel"), name="cast_rest",
    )(src, dst)


def _mm_heads_kernel(a_ref, w_ref, o_ref):
    res = jnp.dot(a_ref[...], w_ref[...], preferred_element_type=F32)
    for h in range(o_ref.shape[0]):
        o_ref[h] = res[:, h * LANES:(h + 1) * LANES].astype(o_ref.dtype)


def _mm_heads_call(a, w, out_dtype, tm, tn, name):
    m, k = a.shape
    n = w.shape[1]
    return pl.pallas_call(
        _mm_heads_kernel, grid=(n // tn, m // tm),
        in_specs=[pl.BlockSpec((tm, k), lambda j, i: (i, 0)), pl.BlockSpec((k, tn), lambda j, i: (0, j))],
        out_specs=pl.BlockSpec((tn // LANES, tm, LANES), lambda j, i: (j, i, 0)),
        out_shape=jax.ShapeDtypeStruct((n // LANES, m, LANES), out_dtype),
        compiler_params=_params("parallel", "parallel"), name=name,
    )(a, w)


def _mm_res_ln_kernel(a_ref, w_ref, h_ref, g_ref, b_ref, of_ref, ob_ref):
    mix = jnp.dot(a_ref[...], w_ref[...], preferred_element_type=F32)
    y = _layer_norm_rows(DN_ALPHA * h_ref[...] + mix, g_ref[...], b_ref[...])
    of_ref[...] = y
    ob_ref[...] = y.astype(BF16)


def _mm_res_ln_call(a, w, h, g, b, name, tm=512):
    m, k = a.shape
    d = w.shape[1]
    row = pl.BlockSpec((tm, d), lambda i: (i, 0))
    vec = pl.BlockSpec((1, d), lambda i: (0, 0))
    return pl.pallas_call(
        _mm_res_ln_kernel, grid=(m // tm,),
        in_specs=[pl.BlockSpec((tm, k), lambda i: (i, 0)), pl.BlockSpec((k, d), lambda i: (0, 0)), row, vec, vec],
        out_specs=[row, row],
        out_shape=[jax.ShapeDtypeStruct((m, d), F32), jax.ShapeDtypeStruct((m, d), BF16)],
        compiler_params=_params("parallel"), name=name,
    )(a, w, h, g.reshape(1, d), b.reshape(1, d))


def _branch_kernel(y_ref, att_ref, gs_ref, ga_ref, ws_ref, wa_ref, o_ref):
    bs = jnp.dot(y_ref[...], ws_ref[...], preferred_element_type=F32)
    ba = jnp.dot(att_ref[...], wa_ref[...], preferred_element_type=F32)
    merged = _sigmoid(gs_ref[...].astype(F32)) * bs + _sigmoid(ga_ref[...].astype(F32)) * ba
    o_ref[...] = merged.astype(o_ref.dtype)


def _branch_call(y_ssm, att, gates, ws, wa, tm=512, tn=1024):
    m, ks = y_ssm.shape
    ka = att.shape[1]
    d = ws.shape[1]
    nj = d // tn
    return pl.pallas_call(
        _branch_kernel, grid=(nj, m // tm),
        in_specs=[pl.BlockSpec((tm, ks), lambda j, i: (i, 0)),
                  pl.BlockSpec((tm, ka), lambda j, i: (i, 0)),
                  pl.BlockSpec((tm, tn), lambda j, i: (i, j)),
                  pl.BlockSpec((tm, tn), lambda j, i: (i, nj + j)),
                  pl.BlockSpec((ks, tn), lambda j, i: (0, j)),
                  pl.BlockSpec((ka, tn), lambda j, i: (0, j))],
        out_specs=pl.BlockSpec((tm, tn), lambda j, i: (i, j)),
        out_shape=jax.ShapeDtypeStruct((m, d), BF16),
        compiler_params=_params("parallel", "parallel"), name="branch_merge",
    )(y_ssm, att, gates, gates, ws, wa)


def _rotary_kernel(qkv_ref, cos_ref, sin_ref, perm_ref, *out_refs, n_head):
    n_grp = len(DIL_CONFIGS)
    (kh_refs, vh_refs, kt_refs, vt_refs, qt_refs, qv_refs, kv_refs, vv_refs) = [
        out_refs[a * n_grp:(a + 1) * n_grp] for a in range(8)]
    in_tail = pl.program_id(0) >= n_head
    tm = qkv_ref.shape[1]
    half = ROT_DIM // 2
    gw = HEADS_PER_GROUP * ATT_HEAD_DIM
    n_att = n_grp * HEADS_PER_GROUP

    def rot(x, cos, sin):
        lane = lax.broadcasted_iota(jnp.int32, x.shape, 1)
        partner = jnp.where(lane < half, pltpu.roll(x, ATT_HEAD_DIM - half, 1), pltpu.roll(x, half, 1))
        return x * cos + partner * sin

    def rot_view(x, cos, sin):
        partner = jnp.dot(x.astype(BF16), perm_ref[...], preferred_element_type=F32)
        return (x * cos + partner * sin).astype(BF16)

    def emit_token_major(k_refs, v_refs):
        for g in range(n_grp):
            for h in range(HEADS_PER_GROUP):
                hd = g * HEADS_PER_GROUP + h
                k_refs[g][:, h, :] = rot(qkv_ref[n_att + hd], cos_ref[...], sin_ref[...])
                v_refs[g][:, h, :] = qkv_ref[2 * n_att + hd]

    pl.when(jnp.logical_not(in_tail))(lambda: emit_token_major(kh_refs, vh_refs))
    pl.when(in_tail)(lambda: emit_token_major(kt_refs, vt_refs))

    for g, (_, dil) in enumerate(DIL_CONFIGS):
        for r in range(dil):
            rows = pl.ds(r, tm // dil, stride=dil) if dil > 1 else slice(None)
            cos = cos_ref[rows, :]
            sin = sin_ref[rows, :]
            for h in range(HEADS_PER_GROUP):
                hd = g * HEADS_PER_GROUP + h
                view = slice(r * gw + h * ATT_HEAD_DIM, r * gw + (h + 1) * ATT_HEAD_DIM)
                qv_refs[g][:, view] = rot_view(qkv_ref[hd, rows, :], cos, sin)
                kv_refs[g][:, view] = rot_view(qkv_ref[n_att + hd, rows, :], cos, sin)
                vv_refs[g][:, view] = qkv_ref[2 * n_att + hd, rows, :].astype(BF16)

    @pl.when(in_tail)
    def _():
        for g in range(n_grp):
            for h in range(HEADS_PER_GROUP):
                tok = slice(h * ATT_HEAD_DIM, (h + 1) * ATT_HEAD_DIM)
                qt_refs[g][:, tok] = rot(qkv_ref[g * HEADS_PER_GROUP + h], cos_ref[...], sin_ref[...])


def _rotary_call(qkv, cos_t, sin_t, n_head_rows, tm=256):
    n_slab, t, _ = qkv.shape
    gw = HEADS_PER_GROUP * ATT_HEAD_DIM
    n_grp = len(DIL_CONFIGS)
    n_head = n_head_rows // tm
    hshape = (tm, HEADS_PER_GROUP, ATT_HEAD_DIM)
    head_rows = pl.BlockSpec(hshape, lambda i: (jnp.minimum(i, n_head - 1), 0, 0))
    tail_rows = pl.BlockSpec(hshape, lambda i: (jnp.maximum(i - n_head, 0), 0, 0))
    tail = pl.BlockSpec((tm, gw), lambda i: (jnp.maximum(i - n_head, 0), 0))
    tab = pl.BlockSpec((tm, ATT_HEAD_DIM), lambda i: (i, 0))
    views = [pl.BlockSpec((tm // dil, dil * gw), lambda i: (i, 0)) for _, dil in DIL_CONFIGS]
    view_shapes = [jax.ShapeDtypeStruct((t // dil, dil * gw), BF16) for _, dil in DIL_CONFIGS]
    li = jnp.arange(ATT_HEAD_DIM)[:, None]
    lj = jnp.arange(ATT_HEAD_DIM)[None, :]
    half = ROT_DIM // 2
    perm = (((lj < half) & (li == lj + half)) | ((lj >= half) & (lj < ROT_DIM) & (li == lj - half))).astype(BF16)
    return pl.pallas_call(
        functools.partial(_rotary_kernel, n_head=n_head), grid=(t // tm,),
        in_specs=[pl.BlockSpec((n_slab, tm, ATT_HEAD_DIM), lambda i: (0, i, 0)), tab, tab,
                  pl.BlockSpec((ATT_HEAD_DIM, ATT_HEAD_DIM), lambda i: (0, 0))],
        out_specs=[head_rows] * (2 * n_grp) + [tail_rows] * (2 * n_grp) + [tail] * n_grp + views * 3,
        out_shape=([jax.ShapeDtypeStruct((n_head_rows,) + hshape[1:], F32)] * (2 * n_grp)
                   + [jax.ShapeDtypeStruct((t - n_head_rows,) + hshape[1:], F32)] * (2 * n_grp)
                   + [jax.ShapeDtypeStruct((t - n_head_rows, gw), F32)] * n_grp + view_shapes * 3),
        compiler_params=_params("arbitrary"), name="rotary_split",
    )(qkv, cos_t, sin_t, perm)


def _rotary_tables(pos):
    half = ROT_DIM // 2
    inv_freq = jnp.exp(-math.log(ROPE_THETA) * jnp.arange(half, dtype=F32) * (2.0 / ROT_DIM))
    ang = pos.astype(F32)[:, None] * inv_freq[None, :]
    cos, sin = jnp.cos(ang), jnp.sin(ang)
    rest = ATT_HEAD_DIM - ROT_DIM
    ones = jnp.ones((pos.shape[0], rest), F32)
    cos_t = jnp.concatenate([cos, cos, ones], axis=1)
    sin_t = jnp.concatenate([-sin, sin, jnp.zeros_like(ones)], axis=1)
    return cos_t, sin_t


def _band_attn_kernel(q_ref, kc_ref, kp_ref, vc_ref, vp_ref, o_ref, lse_ref, *, dil, span, has_prev):
    n = pl.program_id(1)
    scale = ATT_HEAD_DIM ** -0.5
    qi = lax.broadcasted_iota(jnp.int32, (ATT_BLOCK, ATT_BLOCK), 0)
    ki = lax.broadcasted_iota(jnp.int32, (ATT_BLOCK, ATT_BLOCK), 1)
    rel_c = qi - ki
    valid_c = (rel_c >= 0) & (rel_c <= span)
    rel_p = rel_c + ATT_BLOCK
    valid_p = (rel_p <= span) & (n > 0)
    nt = (((1,), (1,)), ((), ()))
    batch = 2 * HEADS_PER_GROUP
    problems = [(r, h) for r in range(dil) for h in range(HEADS_PER_GROUP)]
    for b0 in range(0, len(problems), batch):
        todo = problems[b0:b0 + batch]
        cols = [slice((r * HEADS_PER_GROUP + h) * ATT_HEAD_DIM, (r * HEADS_PER_GROUP + h + 1) * ATT_HEAD_DIM)
                for r, h in todo]
        qs = [q_ref[:, cs] for cs in cols]
        s_c = [lax.dot_general(q, kc_ref[:, cs], nt, preferred_element_type=F32) for q, cs in zip(qs, cols)]
        s_c = [jnp.where(valid_c, s * scale, -jnp.inf) for s in s_c]
        m = [jnp.max(s, axis=-1, keepdims=True) for s in s_c]
        if has_prev:
            s_p = [lax.dot_general(q, kp_ref[:, cs], nt, preferred_element_type=F32) for q, cs in zip(qs, cols)]
            s_p = [jnp.where(valid_p, s * scale, -jnp.inf) for s in s_p]
            m = [jnp.maximum(mm, jnp.max(s, axis=-1, keepdims=True)) for mm, s in zip(m, s_p)]
        p_c = [jnp.exp(s - mm) for s, mm in zip(s_c, m)]
        den = [jnp.sum(p, axis=-1, keepdims=True) for p in p_c]
        o = [jnp.dot(p.astype(BF16), vc_ref[:, cs], preferred_element_type=F32) for p, cs in zip(p_c, cols)]
        if has_prev:
            p_p = [jnp.exp(s - mm) for s, mm in zip(s_p, m)]
            den = [d + jnp.sum(p, axis=-1, keepdims=True) for d, p in zip(den, p_p)]
            o = [oo + jnp.dot(p.astype(BF16), vp_ref[:, cs], preferred_element_type=F32)
                 for oo, p, cs in zip(o, p_p, cols)]
        for (r, h), oo, d, mm in zip(todo, o, den, m):
            rows = pl.ds(r, ATT_BLOCK, stride=dil) if dil > 1 else slice(None)
            o_ref[h, rows, :] = oo / d
            lse_ref[h, rows, :] = jnp.broadcast_to(mm + jnp.log(d), (ATT_BLOCK, ATT_HEAD_DIM))


def _band_attn_call(q, k, v, bsz, seq, window, dil, g):
    gw = HEADS_PER_GROUP * ATT_HEAD_DIM
    t_pad = q.shape[0] * dil
    span = window // dil
    assert span <= ATT_BLOCK and seq % (dil * ATT_BLOCK) == 0
    nb = seq // (dil * ATT_BLOCK)
    cur = pl.BlockSpec((ATT_BLOCK, dil * gw), lambda b, n: (b * nb + n, 0))
    prev = pl.BlockSpec((ATT_BLOCK, dil * gw), lambda b, n: (b * nb + jnp.maximum(n - 1, 0), 0))
    out = pl.BlockSpec((HEADS_PER_GROUP, ATT_BLOCK * dil, ATT_HEAD_DIM), lambda b, n: (0, b * nb + n, 0))
    kern = functools.partial(_band_attn_kernel, dil=dil, span=span, has_prev=nb > 1)
    return pl.pallas_call(
        kern, grid=(bsz, nb),
        in_specs=[cur, cur, prev, cur, prev], out_specs=[out, out],
        out_shape=[jax.ShapeDtypeStruct((HEADS_PER_GROUP, t_pad, ATT_HEAD_DIM), F32)] * 2,
        compiler_params=_params("parallel", "parallel"), name=f"band_attn_g{g}",
    )(q, k, k, v, v)


def _sample_attn_kernel(q_ref, kc_ref, vc_ref, kn_ref, vn_ref, o_ref, lse_ref, *, dil, lq):
    scale = ATT_HEAD_DIM ** -0.5
    n_cache = kc_ref.shape[1]
    row = lax.broadcasted_iota(jnp.int32, (n_cache, HEADS_PER_GROUP, 1), 0)
    for i in range(lq):
        res = i if dil > 1 else 0
        new_rows = range(i + 1) if dil == 1 else (i,)
        q = q_ref[0, i]
        s_c = jnp.sum(kc_ref[0, :, res] * q[None], axis=-1, keepdims=True) * scale
        if dil == 1:
            s_c = jnp.where(row >= i, s_c, -jnp.inf)
        s_new = [jnp.sum(kn_ref[0, j] * q, axis=-1, keepdims=True) * scale for j in new_rows]
        m = jnp.max(s_c, axis=0)
        for s in s_new:
            m = jnp.maximum(m, s)
        p_c = jnp.exp(s_c - m[None])
        den = jnp.sum(p_c, axis=0)
        o = jnp.sum(p_c * vc_ref[0, :, res], axis=0)
        for j, s in zip(new_rows, s_new):
            p = jnp.exp(s - m)
            den = den + p
            o = o + p * vn_ref[0, j]
        o_ref[0, i] = o / den
        lse_ref[0, i] = jnp.broadcast_to(m + jnp.log(den), (HEADS_PER_GROUP, ATT_HEAD_DIM))


def _decode_mem_attn_kernel(q_ref, k_ref, v_ref, o_ref):
    scale = q_ref.shape[-1] ** -0.5
    for i in range(q_ref.shape[1]):
        q = q_ref[0, i]
        s = jnp.sum(k_ref[0] * q[None], axis=-1, keepdims=True) * scale
        m = jnp.max(s, axis=0)
        p = jnp.exp(s - m[None])
        den = jnp.sum(p, axis=0)
        o_ref[0, i] = (jnp.sum(p * v_ref[0], axis=0) / den).astype(o_ref.dtype)


def _decode_mem_attn_call(q, mem_k, mem_v):
    b, lq, nh, e = q.shape
    qs = pl.BlockSpec((1, lq, nh, e), lambda i: (i, 0, 0, 0))
    ms = pl.BlockSpec((1, mem_k.shape[1], nh, e), lambda i: (i, 0, 0, 0))
    return pl.pallas_call(
        _decode_mem_attn_kernel, grid=(b,), in_specs=[qs, ms, ms], out_specs=qs,
        out_shape=jax.ShapeDtypeStruct((b, lq, nh, e), F32),
        compiler_params=_params("parallel"), name="mem_attn_sample",
    )(q, mem_k, mem_v)


def _sample_attn_call(q, k_new, v_new, k_buf, v_buf, window, dil, g):
    b, lq, nh, e = q.shape
    assert k_buf.shape[1] == window and window % dil == 0 and (dil == 1 or dil >= lq) and lq <= window // dil
    n_res = min(dil, lq)
    n_cache = window // dil
    cache = lambda a: a.reshape(b, n_cache, dil, nh, e)
    new = pl.BlockSpec((1, lq, nh, e), lambda i: (i, 0, 0, 0))
    buf = pl.BlockSpec((1, n_cache, n_res, nh, e), lambda i: (i, 0, 0, 0, 0))
    kern = functools.partial(_sample_attn_kernel, dil=dil, lq=lq)
    return pl.pallas_call(
        kern, grid=(b,), in_specs=[new, buf, buf, new, new], out_specs=[new, new],
        out_shape=[jax.ShapeDtypeStruct((b, lq, nh, e), F32)] * 2,
        compiler_params=_params("parallel"), name=f"sample_attn_g{g}",
    )(q, cache(k_buf), cache(v_buf), k_new, v_new)


def _combine_kernel(*refs):
    n_grp = len(DIL_CONFIGS)
    o_refs, l_refs, out_ref = refs[:n_grp], refs[n_grp:2 * n_grp], refs[2 * n_grp]
    for h in range(HEADS_PER_GROUP):
        ls = [r[h] for r in l_refs]
        m = functools.reduce(jnp.maximum, ls)
        ws = [jnp.exp(l - m) for l in ls]
        tot = functools.reduce(lambda a, b: a + b, ws)
        acc = functools.reduce(lambda a, b: a + b, [w * r[h] for w, r in zip(ws, o_refs)])
        out_ref[:, h * ATT_HEAD_DIM:(h + 1) * ATT_HEAD_DIM] = (acc / tot).astype(out_ref.dtype)


def _combine_call(outs, lses, tm=512):
    nh, t, e = outs[0].shape
    row = pl.BlockSpec((nh, tm, e), lambda i: (0, i, 0))
    return pl.pallas_call(
        _combine_kernel, grid=(t // tm,), in_specs=[row] * (2 * len(outs)),
        out_specs=pl.BlockSpec((tm, nh * e), lambda i: (i, 0)),
        out_shape=jax.ShapeDtypeStruct((t, nh * e), BF16),
        compiler_params=_params("parallel"), name="combine_groups",
    )(*outs, *lses)


def _ssd_kernel(*refs, n_chunks, valid_len, has_state):
    if has_state:
        (z_ref, xbc_ref, dt_ref, h0_ref, cprev_ref, cw_ref, cb_ref, dtb_ref, alog_ref, dskip_ref, nw_ref, expand_ref,
         y_ref, hfin_ref, ctail_ref, ht_ref, xext_ref) = refs
    else:
        (z_ref, xbc_ref, dt_ref, cw_ref, cb_ref, dtb_ref, alog_ref, dskip_ref, nw_ref, expand_ref,
         y_ref, hfin_ref, ctail_ref, ht_ref, xext_ref) = refs
    c = pl.program_id(1)
    lc = SSM_CHUNK
    n_st = SSM_STATE
    gw = ht_ref.shape[1] // SSM_GROUPS
    d_inner = ht_ref.shape[1]
    heads_per_group = gw // SSM_HEAD_DIM
    n_tr = d_inner // LANES

    @pl.when(c == 0)
    def _():
        if has_state:
            xext_ref[0:SUBLANES, :] = cprev_ref[0]
            for k in range(n_tr):
                ht_ref[:, k * LANES:(k + 1) * LANES] = h0_ref[0, k * LANES:(k + 1) * LANES, :].T
        else:
            xext_ref[0:SUBLANES, :] = jnp.zeros((SUBLANES, xext_ref.shape[1]), F32)
            ht_ref[...] = jnp.zeros(ht_ref.shape, F32)

    xext_ref[SUBLANES:SUBLANES + lc, :] = xbc_ref[...]

    def conv_silu(c0, width):
        window = xext_ref[0:SUBLANES + lc, c0:c0 + width]
        acc = cb_ref[:, c0:c0 + width] + cw_ref[CONV_W - 1:CONV_W, c0:c0 + width] * window[SUBLANES:]
        for s in range(1, CONV_W):
            shifted = pltpu.roll(window, s, 0)[SUBLANES:]
            acc = acc + cw_ref[CONV_W - 1 - s:CONV_W - s, c0:c0 + width] * shifted
        return acc * _sigmoid(acc)

    dtr = dt_ref[...] + dtb_ref[...]
    dt = jnp.maximum(dtr, 0.0) + jnp.log(1.0 + jnp.exp(-jnp.abs(dtr)))
    if valid_len < lc:
        trow = lax.broadcasted_iota(jnp.int32, dt.shape, 0)
        dt = jnp.where(trow < valid_len, dt, 0.0)
    a = dt * (-jnp.exp(alog_ref[...]))
    ti = lax.broadcasted_iota(jnp.int32, (lc, lc), 0)
    si = lax.broadcasted_iota(jnp.int32, (lc, lc), 1)
    tri = si <= ti
    f32_dot = lambda lhs, rhs: jnp.dot(lhs, rhs, preferred_element_type=F32)
    tri_b = tri.astype(BF16)
    acum = functools.reduce(lambda u, v: u + v, [f32_dot(tri_b, p) for p in _split3(a)])
    tri_t = (ti <= si).astype(BF16)
    acum_t = functools.reduce(lambda u, v: u + v, [f32_dot(p, tri_t) for p in _split3(a.T)])
    dt3, ac3 = _split3(dt), _split3(acum)
    lane = lax.broadcasted_iota(jnp.int32, (lc, LANES), 1)
    nt = (((1,), (1,)), ((), ()))

    for g in range(SSM_GROUPS):
        c0 = g * gw
        xs = conv_silu(c0, gw)
        bm = conv_silu(d_inner + g * n_st, n_st)
        cm = conv_silu(d_inner + SSM_GROUPS * n_st + g * n_st, n_st)
        expand = expand_ref[:, c0:c0 + gw]
        dt_x = functools.reduce(lambda u, v: u + v, [f32_dot(p, expand) for p in dt3])
        ac_x = functools.reduce(lambda u, v: u + v, [f32_dot(p, expand) for p in ac3])
        xdt = xs * dt_x
        xdt_b = xdt.astype(BF16)
        cm_b = cm.astype(BF16)
        cb = lax.dot_general(cm_b, bm.astype(BF16), nt, preferred_element_type=F32)
        h_old = ht_ref[:, c0:c0 + gw]
        y = jnp.dot(cm_b, h_old.astype(BF16), preferred_element_type=F32) * jnp.exp(ac_x)
        diag = []
        for k in range(gw // LANES):
            x_pair = xdt_b[:, k * LANES:(k + 1) * LANES]
            y_pair = None
            for hh in range(LANES // SSM_HEAD_DIM):
                h = g * heads_per_group + k * (LANES // SSM_HEAD_DIM) + hh
                seg = acum[:, h:h + 1] - acum_t[h:h + 1, :]
                lmat = jnp.exp(jnp.where(tri, seg, -jnp.inf))
                m_b = (cb * lmat).astype(BF16)
                in_head = (lane >= hh * SSM_HEAD_DIM) & (lane < (hh + 1) * SSM_HEAD_DIM)
                part = jnp.dot(m_b, jnp.where(in_head, x_pair, jnp.zeros_like(x_pair)), preferred_element_type=F32)
                y_pair = part if y_pair is None else y_pair + part
            diag.append(y_pair)
        y = y + jnp.concatenate(diag, axis=1) + dskip_ref[:, c0:c0 + gw] * xs
        zg = z_ref[:, c0:c0 + gw].astype(F32)
        y = y * (zg * _sigmoid(zg))
        y = y * lax.rsqrt(jnp.mean(y * y, axis=-1, keepdims=True) + RMS_EPS)
        y_ref[:, c0:c0 + gw] = (y * nw_ref[:, c0:c0 + gw]).astype(y_ref.dtype)
        a_last = ac_x[lc - 1:lc, :]
        xw = (xdt * jnp.exp(a_last - ac_x)).astype(BF16)
        ht_ref[:, c0:c0 + gw] = h_old * jnp.exp(a_last) + jnp.dot(bm.T.astype(BF16), xw, preferred_element_type=F32)

    xext_ref[0:SUBLANES, :] = xext_ref[lc:lc + SUBLANES, :]

    @pl.when(c == n_chunks - 1)
    def _():
        ctail_ref[0] = xext_ref[0:SUBLANES, :]
        for k in range(n_tr):
            hfin_ref[0, k * LANES:(k + 1) * LANES, :] = ht_ref[:, k * LANES:(k + 1) * LANES].T


def _ssd_call(z, xbc, dt_raw, n_seq, n_chunks, valid_len, state, conv_w, conv_b, dt_bias, a_log, d_skip, norm_w,
              out_rows, name):
    d_inner = z.shape[1]
    conv_dim = xbc.shape[1]
    n_heads = d_inner // SSM_HEAD_DIM
    lc = SSM_CHUNK
    pad_h = lambda v: jnp.pad(v.astype(F32), (0, LANES - n_heads)).reshape(1, LANES)
    rows = lambda w: pl.BlockSpec((lc, w), lambda b, c: (b * n_chunks + c, 0))
    const = lambda r, w: pl.BlockSpec((r, w), lambda b, c: (0, 0))
    per_seq = lambda r, w: pl.BlockSpec((1, r, w), lambda b, c: (b, 0, 0))
    args = [z, xbc, dt_raw]
    specs = [rows(d_inner), rows(conv_dim), rows(LANES)]
    if state is not None:
        args += list(state)
        specs += [per_seq(d_inner, SSM_STATE), per_seq(SUBLANES, conv_dim)]
    args += [conv_w, conv_b.reshape(1, conv_dim), pad_h(dt_bias), pad_h(a_log),
             jnp.repeat(d_skip.astype(F32), SSM_HEAD_DIM).reshape(1, d_inner), norm_w.reshape(1, d_inner),
             (jnp.arange(LANES)[:, None] == jnp.arange(d_inner)[None, :] // SSM_HEAD_DIM).astype(BF16)]
    specs += [const(CONV_W, conv_dim), const(1, conv_dim), const(1, LANES), const(1, LANES),
              const(1, d_inner), const(1, d_inner), const(LANES, d_inner)]
    kern = functools.partial(_ssd_kernel, n_chunks=n_chunks, valid_len=valid_len, has_state=state is not None)
    return pl.pallas_call(
        kern, grid=(n_seq, n_chunks), in_specs=specs,
        out_specs=[rows(d_inner), per_seq(d_inner, SSM_STATE), per_seq(SUBLANES, conv_dim)],
        out_shape=[jax.ShapeDtypeStruct((out_rows, d_inner), BF16),
                   jax.ShapeDtypeStruct((n_seq, d_inner, SSM_STATE), F32),
                   jax.ShapeDtypeStruct((n_seq, SUBLANES, conv_dim), F32)],
        scratch_shapes=[pltpu.VMEM((SSM_STATE, d_inner), F32), pltpu.VMEM((lc + 2 * SUBLANES, conv_dim), F32)],
        compiler_params=_params("parallel", "arbitrary"), name=name,
    )(*args)


def _mem_attn_kernel(q_ref, k_ref, v_ref, o_ref):
    hd = q_ref.shape[1] // MEM_HEADS
    scale = hd ** -0.5
    cols = [slice(h * hd, (h + 1) * hd) for h in range(MEM_HEADS)]
    nt = (((1,), (1,)), ((), ()))
    s = [lax.dot_general(q_ref[:, c], k_ref[:, c].astype(BF16), nt, preferred_element_type=F32) * scale for c in cols]
    m = [jnp.max(x, axis=-1, keepdims=True) for x in s]
    p = [jnp.exp(x - mm) for x, mm in zip(s, m)]
    den = [jnp.sum(x, axis=-1, keepdims=True) for x in p]
    o = [jnp.dot(x.astype(BF16), v_ref[:, c].astype(BF16), preferred_element_type=F32) for x, c in zip(p, cols)]
    for c, oo, dd in zip(cols, o, den):
        o_ref[:, c] = (oo / dd).astype(o_ref.dtype)


def _mem_attn_call(q, mem_k, mem_v, n_seq, lq, tq, out_rows, name):
    d = q.shape[1]
    n_mem = mem_k.shape[0] // n_seq
    nq = lq // tq
    qs = pl.BlockSpec((tq, d), lambda b, i: (b * nq + i, 0))
    ks = pl.BlockSpec((n_mem, d), lambda b, i: (b, 0))
    return pl.pallas_call(
        _mem_attn_kernel, grid=(n_seq, nq), in_specs=[qs, ks, ks], out_specs=qs,
        out_shape=jax.ShapeDtypeStruct((out_rows, d), BF16),
        compiler_params=_params("parallel", "parallel"), name=name,
    )(q, mem_k, mem_v)


def _router_kernel(x_ref, wh_ref, wm_ref, b_ref, idx_ref, gate_ref):
    x = x_ref[...]
    xh = x.astype(BF16)
    xm = (x - xh.astype(F32)).astype(BF16)
    f32_dot = lambda a, b: jnp.dot(a, b, preferred_element_type=F32)
    logits = f32_dot(xh, wh_ref[...]) + (f32_dot(xh, wm_ref[...]) + f32_dot(xm, wh_ref[...])) + b_ref[...]
    lane = lax.broadcasted_iota(jnp.int32, logits.shape, 1)
    idx_out = jnp.zeros(logits.shape, jnp.int32)
    vals = []
    for k in range(TOP_K):
        m = jnp.max(logits, axis=-1, keepdims=True)
        pick = jnp.min(jnp.where(logits == m, lane, LANES), axis=-1, keepdims=True)
        idx_out = jnp.where(lane == k, pick, idx_out)
        logits = jnp.where(lane == pick, -jnp.inf, logits)
        vals.append(m)
    exps = [jnp.exp(v - vals[0]) for v in vals]
    tot = functools.reduce(lambda a, b: a + b, exps)
    gate_out = jnp.zeros(logits.shape, F32)
    for k in range(TOP_K):
        gate_out = jnp.where(lane == k, exps[k] / tot, gate_out)
    idx_ref[...] = idx_out
    gate_ref[...] = gate_out


def _router_call(x, w_router, b_router, tm=256):
    t, d = x.shape
    n_exp = w_router.shape[1]
    w = jnp.pad(w_router.astype(F32), ((0, 0), (0, LANES - n_exp)))
    w_hi = w.astype(BF16)
    w_mid = (w - w_hi.astype(F32)).astype(BF16)
    b = jnp.pad(b_router.astype(F32), (0, LANES - n_exp), constant_values=-jnp.inf).reshape(1, LANES)
    out = pl.BlockSpec((tm, LANES), lambda i: (i, 0))
    wspec = pl.BlockSpec((d, LANES), lambda i: (0, 0))
    return pl.pallas_call(
        _router_kernel, grid=(t // tm,),
        in_specs=[pl.BlockSpec((tm, d), lambda i: (i, 0)), wspec, wspec, pl.BlockSpec((1, LANES), lambda i: (0, 0))],
        out_specs=[out, out],
        out_shape=[jax.ShapeDtypeStruct((t, LANES), jnp.int32), jax.ShapeDtypeStruct((t, LANES), F32)],
        compiler_params=_params("parallel"), name="router",
    )(x, w_hi, w_mid, b)


def _dispatch_kernel(pend_ref, slot_ref, x_ref, xg_hbm, sbuf, zbuf, sem, zsem, *, n_exp, block_rows, n_tiles):
    i = pl.program_id(0)
    s = i % 2
    tm = x_ref.shape[0]

    def zero_copy(e):
        start = pl.multiple_of(pend_ref[e] - block_rows, block_rows)
        return pltpu.make_async_copy(zbuf, xg_hbm.at[pl.ds(start, block_rows)], zsem)

    def has_rows(e):
        return pend_ref[e] > (pend_ref[e - 1] if e else 0)

    def wait_rows(s_):
        for _ in range(TOP_K):
            pltpu.make_async_copy(sbuf.at[s_], xg_hbm.at[pl.ds(0, tm)], sem.at[s_]).wait()

    @pl.when(i == 0)
    def _():
        zbuf[...] = jnp.zeros(zbuf.shape, F32)
        for e in range(n_exp):
            pl.when(has_rows(e))(lambda e=e: zero_copy(e).start())
        for e in range(n_exp):
            pl.when(has_rows(e))(lambda e=e: zero_copy(e).wait())

    @pl.when(i >= 2)
    def _():
        wait_rows(s)

    sbuf[s] = x_ref[...]
    for r in range(tm):
        for k in range(TOP_K):
            pltpu.make_async_copy(sbuf.at[s, pl.ds(r, 1)], xg_hbm.at[pl.ds(slot_ref[0, 0, r * TOP_K + k], 1)],
                                  sem.at[s]).start()

    @pl.when(i == n_tiles - 1)
    def _():
        wait_rows(s)
        if n_tiles > 1:
            wait_rows(1 - s)


def _dispatch_call(x, slot, pad_end, n_slots, tm=256):
    t, d = x.shape
    n_tiles = t // tm
    grid_spec = pltpu.PrefetchScalarGridSpec(
        num_scalar_prefetch=1, grid=(n_tiles,),
        in_specs=[pl.BlockSpec((1, 1, tm * TOP_K), lambda i, pe: (i, 0, 0), memory_space=pltpu.SMEM),
                  pl.BlockSpec((tm, d), lambda i, pe: (i, 0))],
        out_specs=pl.BlockSpec(memory_space=pl.ANY),
        scratch_shapes=[pltpu.VMEM((2, tm, d), F32), pltpu.VMEM((MOE_TM, d), F32),
                        pltpu.SemaphoreType.DMA((2,)), pltpu.SemaphoreType.DMA(())],
    )
    kern = functools.partial(_dispatch_kernel, n_exp=pad_end.shape[0], block_rows=MOE_TM, n_tiles=n_tiles)
    return pl.pallas_call(
        kern, grid_spec=grid_spec, out_shape=jax.ShapeDtypeStruct((n_slots, d), F32),
        compiler_params=_params("arbitrary"), name="moe_dispatch",
    )(pad_end, slot.reshape(n_tiles, 1, tm * TOP_K), x)


def _moe_kernel(nused_ref, bexp_ref, x_ref, wg_ref, wu_ref, wd_ref, bg_ref, bu_ref, bd_ref, y_ref, *, nj):
    i = pl.program_id(0)
    j = pl.program_id(1)

    @pl.when(i < nused_ref[0])
    def _():
        @pl.when(j == 0)
        def _():
            y_ref[...] = jnp.broadcast_to(bd_ref[0], y_ref.shape)

        x = x_ref[...].astype(BF16)
        gate = jnp.minimum(jnp.dot(x, wg_ref[0], preferred_element_type=F32) + bg_ref[0], SWIGLU_LIMIT)
        up = jnp.clip(jnp.dot(x, wu_ref[0], preferred_element_type=F32) + bu_ref[0], -SWIGLU_LIMIT, SWIGLU_LIMIT)
        act = (up + 1.0) * gate * _sigmoid(gate * SWIGLU_ALPHA)
        y_ref[...] += jnp.dot(act.astype(BF16), wd_ref[0], preferred_element_type=F32)


def _moe_call(x, top_idx, w_gate_up, b_gate_up, w_down, b_down):
    t, d = x.shape
    n_exp, _, two_h = w_gate_up.shape
    d_exp = two_h // 2
    tm, tn = MOE_TM, MOE_TN
    nj = d_exp // tn
    n_assign = t * TOP_K
    n_blocks = n_assign // tm + n_exp
    n_slots = n_blocks * tm

    flat_e = top_idx.reshape(n_assign)
    onehot = (flat_e[:, None] == jnp.arange(n_exp, dtype=jnp.int32)[None, :]).astype(jnp.int32)
    counts = jnp.sum(onehot, axis=0)
    rank = jnp.take_along_axis(jnp.cumsum(onehot, axis=0), flat_e[:, None], axis=1)[:, 0] - 1
    padded = (counts + tm - 1) // tm * tm
    pad_end = jnp.cumsum(padded)
    slot = (pad_end - padded)[flat_e] + rank
    n_used = (pad_end[-1] // tm).astype(jnp.int32)
    blk = jnp.minimum(jnp.arange(n_blocks, dtype=jnp.int32), n_used - 1) * tm
    block_expert = jnp.minimum(jnp.sum((pad_end[None, :] <= blk[:, None]).astype(jnp.int32), axis=1), n_exp - 1)
    xg = _dispatch_call(x, slot, pad_end.astype(jnp.int32), n_slots)

    live = lambda i, nu: i < nu[0]
    rows = pl.BlockSpec((tm, d), lambda i, j, nu, be: (jnp.minimum(i, nu[0] - 1), 0))
    grid_spec = pltpu.PrefetchScalarGridSpec(
        num_scalar_prefetch=2, grid=(n_blocks, nj),
        in_specs=[
            rows,
            pl.BlockSpec((1, d, tn), lambda i, j, nu, be: (be[i], 0, jnp.where(live(i, nu), j, nj - 1))),
            pl.BlockSpec((1, d, tn), lambda i, j, nu, be: (be[i], 0, nj + jnp.where(live(i, nu), j, nj - 1))),
            pl.BlockSpec((1, tn, d), lambda i, j, nu, be: (be[i], jnp.where(live(i, nu), j, nj - 1), 0)),
            pl.BlockSpec((1, 1, tn), lambda i, j, nu, be: (be[i], 0, jnp.where(live(i, nu), j, nj - 1))),
            pl.BlockSpec((1, 1, tn), lambda i, j, nu, be: (be[i], 0, nj + jnp.where(live(i, nu), j, nj - 1))),
            pl.BlockSpec((1, 1, d), lambda i, j, nu, be: (be[i], 0, 0)),
        ],
        out_specs=rows,
    )
    bgu = b_gate_up.astype(F32).reshape(n_exp, 1, two_h)
    y_slots = pl.pallas_call(
        functools.partial(_moe_kernel, nj=nj), grid_spec=grid_spec,
        out_shape=jax.ShapeDtypeStruct((n_slots, d), F32),
        compiler_params=_params("arbitrary", "arbitrary"), name="moe_experts",
    )(n_used.reshape(1), block_expert, xg, w_gate_up, w_gate_up, w_down, bgu, bgu,
      b_down.astype(F32).reshape(n_exp, 1, d))
    return y_slots, slot


def _moe_out_kernel(slotc_ref, slotn_ref, y_hbm, gate_ref, h_ref, g_ref, b_ref, head_ref, tail_ref, ybuf, sem,
                    *, n_head, n_tiles):
    i = pl.program_id(0)
    s = i % 2
    tm = h_ref.shape[0]

    def fetch(tab_ref, s_):
        for r in range(TOP_K * tm):
            pltpu.make_async_copy(y_hbm.at[pl.ds(tab_ref[0, 0, r], 1)], ybuf.at[s_, pl.ds(r, 1)], sem.at[s_]).start()

    @pl.when(i == 0)
    def _():
        fetch(slotc_ref, 0)

    @pl.when(i + 1 < n_tiles)
    def _():
        fetch(slotn_ref, 1 - s)

    pltpu.make_async_copy(y_hbm.at[pl.ds(0, TOP_K * tm)], ybuf.at[s], sem.at[s]).wait()
    gates = gate_ref[...]
    y = gates[:, 0:1] * ybuf[s, 0:tm]
    for k in range(1, TOP_K):
        y = y + gates[:, k:k + 1] * ybuf[s, k * tm:(k + 1) * tm]
    out = _layer_norm_rows(DN_ALPHA * h_ref[...] + y, g_ref[...], b_ref[...])

    @pl.when(pl.program_id(0) < n_head)
    def _():
        head_ref[...] = out

    @pl.when(pl.program_id(0) >= n_head)
    def _():
        tail_ref[...] = out


def _moe_out_call(y_slots, slot, gates, h, g, b, n_head_rows, tm=256):
    t, d = h.shape
    nb = t // tm
    n_head = n_head_rows // tm
    row = pl.BlockSpec((tm, d), lambda i: (i, 0))
    vec = pl.BlockSpec((1, d), lambda i: (0, 0))
    table = jnp.transpose(slot.reshape(nb, tm, TOP_K), (0, 2, 1)).reshape(nb, 1, TOP_K * tm)
    smem = lambda imap: pl.BlockSpec((1, 1, TOP_K * tm), imap, memory_space=pltpu.SMEM)
    return pl.pallas_call(
        functools.partial(_moe_out_kernel, n_head=n_head, n_tiles=nb), grid=(nb,),
        in_specs=[smem(lambda i: (i, 0, 0)), smem(lambda i: (jnp.minimum(i + 1, nb - 1), 0, 0)),
                  pl.BlockSpec(memory_space=pl.ANY), pl.BlockSpec((tm, LANES), lambda i: (i, 0)), row, vec, vec],
        out_specs=[pl.BlockSpec((tm, d), lambda i: (jnp.minimum(i, n_head - 1), 0)),
                   pl.BlockSpec((tm, d), lambda i: (jnp.maximum(i - n_head, 0), 0))],
        out_shape=[jax.ShapeDtypeStruct((n_head_rows, d), F32), jax.ShapeDtypeStruct((t - n_head_rows, d), F32)],
        scratch_shapes=[pltpu.VMEM((2, TOP_K * tm, d), F32), pltpu.SemaphoreType.DMA((2,))],
        compiler_params=_params("arbitrary"), name="moe_combine_ln",
    )(table, table, y_slots, gates, h, g.reshape(1, d), b.reshape(1, d))


def kernel(x_prompt, x_sample, state_conv, state_ssm, cache_k_w128, cache_v_w128, cache_k_w512, cache_v_w512, cache_k_w2048, cache_v_w2048, cache_mem_k, cache_mem_v, mem_prompt, ln_in_g, ln_in_b, w_in, conv_w, conv_b, dt_bias, a_log, d_skip, ssm_norm_w, w_branch_ssm, w_branch_att, w_mix_out, ln1_g, ln1_b, w_mem_q, w_mem_k, w_mem_v, w_mem_o, ln2_g, ln2_b, w_router, b_router, w_gate_up, b_gate_up, w_down, b_down, ln3_g, ln3_b):
    assert w_in.shape[0] == DEPTH
    bp, lp, d = x_prompt.shape
    bs, ls, _ = x_sample.shape
    n_p, n_s = bp * lp, bs * ls
    t_real = n_p + n_s
    t_pad = -(-t_real // ROW_TILE) * ROW_TILE
    d_inner = ssm_norm_w.shape[1]
    conv_dim = conv_w.shape[2]
    n_heads = d_inner // SSM_HEAD_DIM
    gw = HEADS_PER_GROUP * ATT_HEAD_DIM
    att_w = len(DIL_CONFIGS) * gw
    n_mem = mem_prompt.shape[1]
    lc = SSM_CHUNK
    assert lp % lc == 0 and ls <= lc and n_s % SUBLANES == 0

    def sample_rows(a):
        return a[n_p:n_p + n_s]

    def with_sample_rows(a, rows):
        tail = jnp.concatenate([rows.astype(a.dtype), jnp.zeros((t_pad - t_real, a.shape[1]), a.dtype)], axis=0)
        return lax.dynamic_update_slice(a, tail, (n_p, 0))

    def with_sample_heads(a, rows):
        nh, _, e = a.shape
        rows = jnp.transpose(rows.reshape(n_s, nh, e), (1, 0, 2)).astype(a.dtype)
        tail = jnp.concatenate([rows, jnp.zeros((nh, t_pad - t_real, e), a.dtype)], axis=1)
        return lax.dynamic_update_slice(a, tail, (0, n_p, 0))

    x_tail = jnp.concatenate([x_sample.reshape(n_s, d), jnp.zeros((t_pad - t_real, d), x_sample.dtype)], axis=0)
    h0, h0_b = _ln_call(x_prompt.reshape(n_p, d), x_tail, ln_in_g, ln_in_b)

    o_z, o_xbc, o_dt, o_qkv, o_g = 0, d_inner, d_inner + conv_dim, d_inner + conv_dim + n_heads, \
        d_inner + conv_dim + n_heads + 3 * att_w
    w_cols = lambda a, b: w_in[0, :, a:b].astype(BF16)
    n_exp, _, two_h = w_gate_up.shape[1:]
    wgu_src = w_gate_up[0].reshape(n_exp * d, two_h)
    wd_src = w_down[0].reshape(n_exp * (two_h // 2), d)
    steps = lambda n_cols, tn: (n_cols // tn) * (t_pad // 512)
    gu_rows, wd_rows = 128, 256
    gu_chunks, wd_chunks = wgu_src.shape[0] // gu_rows, wd_src.shape[0] // wd_rows
    n_a = min(steps(conv_dim, 1024), gu_chunks)
    n_b = min(steps(d_inner, 1024), gu_chunks - n_a)
    n_c = min(steps(2 * d, 1024), wd_chunks)
    xbc, wgu_b = _mm_call(h0_b, w_cols(o_xbc, o_dt), F32, 512, 1024, "in_xbc", (wgu_src, None, 0, n_a, gu_rows))
    if n_b:
        z, wgu_b = _mm_call(h0_b, w_cols(o_z, o_xbc), BF16, 512, 1024, "in_z", (wgu_src, wgu_b, n_a, n_b, gu_rows))
    else:
        z = _mm_call(h0_b, w_cols(o_z, o_xbc), BF16, 512, 1024, "in_z")
    if gu_chunks - n_a - n_b:
        wgu_b = _cast_call(wgu_src, wgu_b, n_a + n_b, gu_chunks - n_a - n_b, gu_rows)
    w_dt = jnp.pad(w_cols(o_dt, o_qkv), ((0, 0), (0, LANES - n_heads)))
    dt_raw = _mm_call(h0_b, w_dt, F32, 512, LANES, "in_dt")
    qkv = _mm_heads_call(h0_b, w_cols(o_qkv, o_g), F32, 512, att_w, "in_qkv")
    gates, wd_b = _mm_call(h0_b, w_cols(o_g, w_in.shape[2]), BF16, 512, 1024, "in_gates",
                           (wd_src, None, 0, n_c, wd_rows))
    if wd_chunks - n_c:
        wd_b = _cast_call(wd_src, wd_b, n_c, wd_chunks - n_c, wd_rows)
    wgu_b = wgu_b.reshape(n_exp, d, two_h)
    wd_b = wd_b.reshape(n_exp, two_h // 2, d)

    ssd_w = (conv_w[0], conv_b[0], dt_bias[0], a_log[0], d_skip[0], ssm_norm_w[0])
    y_ssm, ssm_p, conv_tail = _ssd_call(z, xbc, dt_raw, bp, lp // lc, lc, None, *ssd_w, out_rows=t_pad,
                                        name="ssd_prompt")

    def pad_seq(a):
        return jnp.pad(sample_rows(a).reshape(bs, ls, a.shape[1]), ((0, 0), (0, lc - ls), (0, 0))).reshape(bs * lc, a.shape[1])

    conv_prev = jnp.pad(state_conv[0], ((0, 0), (SUBLANES - (CONV_W - 1), 0), (0, 0)))
    y_s, ssm_s, _ = _ssd_call(pad_seq(z), pad_seq(xbc), pad_seq(dt_raw), bs, 1, ls,
                              (state_ssm[0].reshape(bs, d_inner, SSM_STATE), conv_prev), *ssd_w,
                              out_rows=bs * lc, name="ssd_sample")
    y_ssm = with_sample_rows(y_ssm, y_s.reshape(bs, lc, d_inner)[:, :ls].reshape(n_s, d_inner))

    pos = jnp.concatenate([jnp.tile(jnp.arange(lp, dtype=jnp.int32), bp),
                           jnp.tile(PAST_LEN + jnp.arange(ls, dtype=jnp.int32), bs),
                           jnp.zeros((t_pad - t_real,), jnp.int32)])
    cos_t, sin_t = _rotary_tables(pos)
    split = _rotary_call(qkv, cos_t, sin_t, n_p)
    n_grp = len(DIL_CONFIGS)
    k_head, v_head, k_tail, v_tail, q_tail, q_view, k_view, v_view = [split[a * n_grp:(a + 1) * n_grp]
                                                                      for a in range(8)]
    caches = ((cache_k_w128, cache_v_w128), (cache_k_w512, cache_v_w512), (cache_k_w2048, cache_v_w2048))
    outs, lses, kv_p, kv_s = [], [], [], []
    heads = lambda a: a.reshape(bs, ls, HEADS_PER_GROUP, ATT_HEAD_DIM)
    for g, (window, dil) in enumerate(DIL_CONFIGS):
        o_p, l_p = _band_attn_call(q_view[g], k_view[g], v_view[g], bp, lp, window, dil, g)
        k_new, v_new = heads(k_tail[g][:n_s]), heads(v_tail[g][:n_s])
        o_s, l_s = _sample_attn_call(heads(q_tail[g][:n_s]), k_new, v_new, caches[g][0][0], caches[g][1][0],
                                     window, dil, g)
        outs.append(with_sample_heads(o_p, o_s))
        lses.append(with_sample_heads(l_p, l_s))
        keep = min(window, lp)
        for a in (k_head[g], v_head[g]):
            kv_p.append(a.reshape(bp, lp, HEADS_PER_GROUP, ATT_HEAD_DIM)[:, lp - keep:][None])
        kv_s += [k_new[None], v_new[None]]
    att = _combine_call(outs, lses)

    merged = _branch_call(y_ssm, att, gates, w_branch_ssm[0].astype(BF16), w_branch_att[0].astype(BF16))
    h1, h1_b = _mm_res_ln_call(merged, w_mix_out[0].astype(BF16), h0, ln1_g[0], ln1_b[0], "mix_out_ln1")

    mem_b = mem_prompt.reshape(bp * n_mem, d).astype(BF16)
    mem_k_p = _mm_call(mem_b, w_mem_k[0].astype(BF16), F32, 512, 1024, "mem_k")
    mem_v_p = _mm_call(mem_b, w_mem_v[0].astype(BF16), F32, 512, 1024, "mem_v")
    q_mem = _mm_call(h1_b, w_mem_q[0].astype(BF16), BF16, 512, 1024, "mem_q")
    o_mem = _mem_attn_call(q_mem, mem_k_p, mem_v_p, bp, lp, 512, t_pad, "mem_attn_prompt")
    q_s = sample_rows(q_mem).reshape(bs, ls, MEM_HEADS, d // MEM_HEADS).astype(F32)
    o_mem_s = _decode_mem_attn_call(q_s, cache_mem_k[0], cache_mem_v[0])
    o_mem = with_sample_rows(o_mem, o_mem_s.reshape(n_s, d))
    h2, _ = _mm_res_ln_call(o_mem, w_mem_o[0].astype(BF16), h1, ln2_g[0], ln2_b[0], "mem_o_ln2")

    idx_t, gate_t = _router_call(h2, w_router[0], b_router[0])
    y_slots, slot = _moe_call(h2, idx_t[:, :TOP_K], wgu_b, b_gate_up[0], wd_b, b_down[0])
    h3_head, h3_tail = _moe_out_call(y_slots, slot, gate_t, h2, ln3_g[0], ln3_b[0], n_p)

    y_prompt = h3_head.reshape(bp, lp, d)
    y_sample = h3_tail[:n_s].reshape(bs, ls, d)
    conv_p = conv_tail[:, SUBLANES - (CONV_W - 1):][None]
    xp_s = jnp.concatenate([state_conv[0].astype(xbc.dtype), sample_rows(xbc).reshape(bs, ls, conv_dim)], axis=1)
    conv_s = xp_s[:, -(CONV_W - 1):][None]
    state_shape = (n_heads, SSM_HEAD_DIM, SSM_STATE)
    ssm_p = ssm_p.reshape(1, bp, *state_shape)
    ssm_s = ssm_s.reshape(1, bs, *state_shape)
    mem_shape = (1, bp, n_mem, MEM_HEADS, d // MEM_HEADS)
    return (y_prompt, y_sample, conv_p, ssm_p, *kv_p, mem_k_p.reshape(mem_shape), mem_v_p.reshape(mem_shape),
            conv_s, ssm_s, *kv_s)
```
